```python
import jax, jax.numpy as jnp
from jax import lax
import numpy as np

D_MODEL = 1024
BATCH = 8
SEQ = 4096
DEPTH = 4

N_MIXERS = 3
ROPE_THETA = 10000.0
NORM_EPS = 1e-6
NEG_INF = -1e30
ATTN_BLOCK = 128
D_FF = 4 * D_MODEL

MLA_HEADS = 16
MLA_Q_RANK = 384
MLA_KV_RANK = 256
MLA_NOPE_DIM = 64
MLA_ROPE_DIM = 32
MLA_V_DIM = 64

FOX_HEADS = 16
FOX_HEAD_DIM = 64
FORGET_BIAS_CENTER = 2.0

DIL_PATTERNS = ((128, 1), (512, 4), (2048, 16))
DIL_GROUPS = len(DIL_PATTERNS)
DIL_HEADS = 16
DIL_HEAD_DIM = 64
DIL_KEYS = max(w // d for w, d in DIL_PATTERNS) + 1
DIL_BLOCK = 32

kernel_name = "hybrid_mla_fox_dilated_interleaved"


def rmsnorm(x, g):
    xf = x.astype(jnp.float32)
    y = xf * lax.rsqrt(jnp.mean(xf * xf, axis=-1, keepdims=True) + NORM_EPS)
    return (y * g.astype(jnp.float32)).astype(x.dtype)


def rope_tables(seq_len, dim):
    inv = 1.0 / (ROPE_THETA ** (jnp.arange(0, dim, 2, dtype=jnp.float32) / dim))
    ang = jnp.arange(seq_len, dtype=jnp.float32)[:, None] * inv[None, :]
    return jnp.cos(ang), jnp.sin(ang)


def apply_rope(x, cos, sin):
    half = x.shape[-1] // 2
    x1 = x[..., :half].astype(jnp.float32)
    x2 = x[..., half:].astype(jnp.float32)
    c = cos[None, :, None, :]
    s = sin[None, :, None, :]
    return jnp.concatenate([x1 * c - x2 * s, x1 * s + x2 * c], axis=-1).astype(x.dtype)


def causal_block_attention(q, k, v, scale, cum_log_forget=None):
    seq = q.shape[1]
    outs = []
    for start in range(0, seq, ATTN_BLOCK):
        end = start + ATTN_BLOCK
        s = jnp.einsum('bqhd,bkhd->bhqk', q[:, start:end], k[:, :end]).astype(jnp.float32) * scale
        if cum_log_forget is not None:
            d_q = jnp.transpose(cum_log_forget[:, start:end], (0, 2, 1))
            d_k = jnp.transpose(cum_log_forget[:, :end], (0, 2, 1))
            s = s + (d_q[..., :, None] - d_k[..., None, :])
        q_pos = jnp.arange(start, end)
        k_pos = jnp.arange(end)
        s = jnp.where(k_pos[None, :] <= q_pos[:, None], s, NEG_INF)
        p = jax.nn.softmax(s, axis=-1).astype(v.dtype)
        outs.append(jnp.einsum('bhqk,bkhd->bqhd', p, v[:, :end]))
    return jnp.concatenate(outs, axis=1)


def mla_mixer(h, cos, sin, wq_a, q_norm, wq_b, wkv_a, kv_norm, wkv_b, wo):
    b, s, _ = h.shape
    c_q = rmsnorm(h @ wq_a, q_norm)
    q = (c_q @ wq_b).reshape(b, s, MLA_HEADS, MLA_NOPE_DIM + MLA_ROPE_DIM)
    q = jnp.concatenate([q[..., :MLA_NOPE_DIM], apply_rope(q[..., MLA_NOPE_DIM:], cos, sin)], axis=-1)
    kv_a = h @ wkv_a
    c_kv = rmsnorm(kv_a[..., :MLA_KV_RANK], kv_norm)
    k_pe = apply_rope(kv_a[..., None, MLA_KV_RANK:], cos, sin)
    kv = (c_kv @ wkv_b).reshape(b, s, MLA_HEADS, MLA_NOPE_DIM + MLA_V_DIM)
    k = jnp.concatenate([kv[..., :MLA_NOPE_DIM],
                         jnp.broadcast_to(k_pe, (b, s, MLA_HEADS, MLA_ROPE_DIM))], axis=-1)
    v = kv[..., MLA_NOPE_DIM:]
    o = causal_block_attention(q, k, v, (MLA_NOPE_DIM + MLA_ROPE_DIM) ** -0.5)
    return o.reshape(b, s, MLA_HEADS * MLA_V_DIM) @ wo


def fox_mixer(h, w_qkv, w_f, b_f, wo):
    b, s, _ = h.shape
    qkv = (h @ w_qkv).reshape(b, s, 3, FOX_HEADS, FOX_HEAD_DIM)
    q, k, v = qkv[:, :, 0], qkv[:, :, 1], qkv[:, :, 2]
    log_f = jax.nn.log_sigmoid((h @ w_f).astype(jnp.float32) + b_f.astype(jnp.float32))
    cum = lax.cumsum(log_f, axis=1)
    o = causal_block_attention(q, k, v, FOX_HEAD_DIM ** -0.5, cum)
    return o.reshape(b, s, FOX_HEADS * FOX_HEAD_DIM) @ wo


def dilated_mixer(h, cos, sin, w_qkv, wo):
    b, s, _ = h.shape
    g_n, hd, dh = DIL_GROUPS, DIL_HEADS, DIL_HEAD_DIM
    qkv = (h @ w_qkv).reshape(b, s, 3, g_n, hd, dh)
    q = apply_rope(qkv[:, :, 0].reshape(b, s, g_n * hd, dh), cos, sin).reshape(b, s, g_n, hd, dh)
    k = apply_rope(qkv[:, :, 1].reshape(b, s, g_n * hd, dh), cos, sin).reshape(b, s, g_n, hd, dh)
    v = qkv[:, :, 2]
    dil = jnp.array([d for _, d in DIL_PATTERNS], dtype=jnp.int32)
    win = jnp.array([w for w, _ in DIL_PATTERNS], dtype=jnp.int32)
    jj = jnp.arange(DIL_KEYS, dtype=jnp.int32)
    offsets = jj[None, :] * dil[:, None]
    in_window = offsets <= win[:, None]
    g_idx = jnp.arange(g_n, dtype=jnp.int32)[:, None, None]
    scale = dh ** -0.5

    def block(start):
        qb = lax.dynamic_slice_in_dim(q, start, DIL_BLOCK, axis=1)
        t = start + jnp.arange(DIL_BLOCK, dtype=jnp.int32)
        idx = t[None, :, None] - offsets[:, None, :]
        valid = (idx >= 0) & in_window[:, None, :]
        idx = jnp.maximum(idx, 0)
        kg = k[:, idx, g_idx]
        vg = v[:, idx, g_idx]
        sc = jnp.einsum('bqghd,bgqjhd->bghqj', qb, kg).astype(jnp.float32) * scale
        sc = jnp.where(valid[None, :, None], sc, NEG_INF)
        m = jnp.max(sc, axis=-1, keepdims=True)
        e = jnp.exp(sc - m)
        den = jnp.sum(e, axis=-1, keepdims=True)
        lse = (m + jnp.log(den))[..., 0]
        o_g = jnp.einsum('bghqj,bgqjhd->bqghd', (e / den).astype(v.dtype), vg)
        w_g = jax.nn.softmax(lse, axis=1)
        return jnp.einsum('bghq,bqghd->bqhd', w_g.astype(v.dtype), o_g)

    starts = jnp.arange(0, s, DIL_BLOCK, dtype=jnp.int32)
    o = lax.map(block, starts)
    o = jnp.moveaxis(o, 0, 1).reshape(b, s, hd * dh)
    return o @ wo


def squared_relu_mlp(h, w_up, w_down):
    return jnp.square(jax.nn.relu(h @ w_up)) @ w_down


def _dense(key, fan_in, fan_out):
    return jax.random.normal(key, (fan_in, fan_out), jnp.float32) * fan_in ** -0.5


def _gain(key, n):
    return 1.0 + 0.05 * jax.random.normal(key, (n,), jnp.float32)


def setup_inputs(seed: int = 0) -> dict:
    key = jax.random.key(seed)
    keys = jax.random.split(key, DEPTH + 2)
    p = {'x': jax.random.normal(keys[0], (BATCH, SEQ, D_MODEL), jnp.float32)}
    for i in range(DEPTH):
        lk = jax.random.split(keys[i + 1], 12)
        pre = 'l%d_' % i
        p[pre + 'attn_norm'] = _gain(lk[0], D_MODEL)
        kind = i % N_MIXERS
        if kind == 0:
            p[pre + 'mla_wq_a'] = _dense(lk[1], D_MODEL, MLA_Q_RANK)
            p[pre + 'mla_q_norm'] = _gain(lk[2], MLA_Q_RANK)
            p[pre + 'mla_wq_b'] = _dense(lk[3], MLA_Q_RANK, MLA_HEADS * (MLA_NOPE_DIM + MLA_ROPE_DIM))
            p[pre + 'mla_wkv_a'] = _dense(lk[4], D_MODEL, MLA_KV_RANK + MLA_ROPE_DIM)
            p[pre + 'mla_kv_norm'] = _gain(lk[5], MLA_KV_RANK)
            p[pre + 'mla_wkv_b'] = _dense(lk[6], MLA_KV_RANK, MLA_HEADS * (MLA_NOPE_DIM + MLA_V_DIM))
            p[pre + 'mla_wo'] = _dense(lk[7], MLA_HEADS * MLA_V_DIM, D_MODEL)
        elif kind == 1:
            p[pre + 'fox_w_qkv'] = _dense(lk[1], D_MODEL, 3 * FOX_HEADS * FOX_HEAD_DIM)
            p[pre + 'fox_w_f'] = _dense(lk[2], D_MODEL, FOX_HEADS)
            p[pre + 'fox_b_f'] = FORGET_BIAS_CENTER + 0.5 * jax.random.normal(lk[3], (FOX_HEADS,), jnp.float32)
            p[pre + 'fox_wo'] = _dense(lk[4], FOX_HEADS * FOX_HEAD_DIM, D_MODEL)
        else:
            p[pre + 'dil_w_qkv'] = _dense(lk[1], D_MODEL, 3 * DIL_GROUPS * DIL_HEADS * DIL_HEAD_DIM)
            p[pre + 'dil_wo'] = _dense(lk[2], DIL_HEADS * DIL_HEAD_DIM, D_MODEL)
        p[pre + 'mlp_norm'] = _gain(lk[8], D_MODEL)
        p[pre + 'w_up'] = _dense(lk[9], D_MODEL, D_FF)
        p[pre + 'w_down'] = _dense(lk[10], D_FF, D_MODEL)
    p['final_norm'] = _gain(keys[DEPTH + 1], D_MODEL)
    return p


def reference(x,
              l0_attn_norm, l0_mla_wq_a, l0_mla_q_norm, l0_mla_wq_b, l0_mla_wkv_a, l0_mla_kv_norm,
              l0_mla_wkv_b, l0_mla_wo, l0_mlp_norm, l0_w_up, l0_w_down,
              l1_attn_norm, l1_fox_w_qkv, l1_fox_w_f, l1_fox_b_f, l1_fox_wo,
              l1_mlp_norm, l1_w_up, l1_w_down,
              l2_attn_norm, l2_dil_w_qkv, l2_dil_wo, l2_mlp_norm, l2_w_up, l2_w_down,
              l3_attn_norm, l3_mla_wq_a, l3_mla_q_norm, l3_mla_wq_b, l3_mla_wkv_a, l3_mla_kv_norm,
              l3_mla_wkv_b, l3_mla_wo, l3_mlp_norm, l3_w_up, l3_w_down,
              final_norm):
    seq = x.shape[1]
    cos_mla, sin_mla = rope_tables(seq, MLA_ROPE_DIM)
    cos_dil, sin_dil = rope_tables(seq, DIL_HEAD_DIM)
    layers = [
        (l0_attn_norm, (l0_mla_wq_a, l0_mla_q_norm, l0_mla_wq_b, l0_mla_wkv_a, l0_mla_kv_norm,
                        l0_mla_wkv_b, l0_mla_wo), l0_mlp_norm, l0_w_up, l0_w_down),
        (l1_attn_norm, (l1_fox_w_qkv, l1_fox_w_f, l1_fox_b_f, l1_fox_wo), l1_mlp_norm, l1_w_up, l1_w_down),
        (l2_attn_norm, (l2_dil_w_qkv, l2_dil_wo), l2_mlp_norm, l2_w_up, l2_w_down),
        (l3_attn_norm, (l3_mla_wq_a, l3_mla_q_norm, l3_mla_wq_b, l3_mla_wkv_a, l3_mla_kv_norm,
                        l3_mla_wkv_b, l3_mla_wo), l3_mlp_norm, l3_w_up, l3_w_down),
    ]
    h = x
    for i in range(DEPTH):
        attn_norm, mixer_params, mlp_norm, w_up, w_down = layers[i]
        a = rmsnorm(h, attn_norm)
        kind = i % N_MIXERS
        if kind == 0:
            a = mla_mixer(a, cos_mla, sin_mla, *mixer_params)
        elif kind == 1:
            a = fox_mixer(a, *mixer_params)
        else:
            a = dilated_mixer(a, cos_dil, sin_dil, *mixer_params)
        h = h + a
        h = h + squared_relu_mlp(rmsnorm(h, mlp_norm), w_up, w_down)
    return rmsnorm(h, final_norm)
```

```python
import functools

import jax
import jax.numpy as jnp
from jax import lax
from jax.experimental import pallas as pl
from jax.experimental.pallas import tpu as pltpu

F32 = jnp.float32
BF16 = jnp.bfloat16

LANES = 128
NORM_EPS = 1e-6
NEG_INF = -1e30
ROPE_THETA = 10000.0

MLA_HEADS = 16
MLA_Q_RANK = 384
MLA_KV_RANK = 256
MLA_NOPE = 64
MLA_ROPE = 32
MLA_V = 64
FOX_HEADS = 16
FOX_DIM = 64
DIL_PATTERNS = ((128, 1), (512, 4), (2048, 16))
DIL_HEADS = 16
DIL_DIM = 64
DIL_WINDOW_KEYS = 128
DIL_TOKENS = 2048

VMEM_LIMIT = 56 * 2**20


def _params(*sem):
    return pltpu.CompilerParams(dimension_semantics=sem, vmem_limit_bytes=VMEM_LIMIT)


def _rms(x, g):
    ms = jnp.mean(x * x, axis=-1, keepdims=True)
    return x * lax.rsqrt(ms + NORM_EPS) * g


def _dot(a, b):
    return jnp.dot(a, b, preferred_element_type=F32)


def _dot_nt(a, b):
    return lax.dot_general(a, b, (((1,), (1,)), ((), ())), preferred_element_type=F32)


def _rope(blk, cos, sin):
    return blk * cos + pltpu.roll(blk, 64, axis=1) * sin


def _norm_proj_kernel(x_ref, g_ref, w_ref, cos_ref, sin_ref, o_ref, xn_ref, *, n_rope_tiles, tn):
    j = pl.program_id(1)

    @pl.when(j == 0)
    def _():
        xn_ref[...] = _rms(x_ref[...], g_ref[...]).astype(BF16)

    acc = _dot(xn_ref[...], w_ref[...])

    def store(rope):
        for c in range(tn // LANES):
            blk = acc[:, c * LANES:(c + 1) * LANES]
            if rope:
                blk = _rope(blk, cos_ref[...], sin_ref[...])
            o_ref[c] = blk.astype(o_ref.dtype)

    if n_rope_tiles == 0:
        store(False)
    else:
        pl.when(j < n_rope_tiles)(lambda: store(True))
        pl.when(j >= n_rope_tiles)(lambda: store(False))


def _norm_proj(h, gain, w, cos_t, sin_t, *, seq, n_rope_tiles, tm=1024, tn=1024):
    t, d = h.shape
    n = w.shape[1]
    ns = seq // tm
    return pl.pallas_call(
        functools.partial(_norm_proj_kernel, n_rope_tiles=n_rope_tiles, tn=tn),
        grid=(t // tm, n // tn),
        in_specs=[
            pl.BlockSpec((tm, d), lambda i, j: (i, 0)),
            pl.BlockSpec((1, d), lambda i, j: (0, 0)),
            pl.BlockSpec((d, tn), lambda i, j: (0, j)),
            pl.BlockSpec((tm, LANES), lambda i, j: (i % ns, 0)),
            pl.BlockSpec((tm, LANES), lambda i, j: (i % ns, 0)),
        ],
        out_specs=pl.BlockSpec((tn // LANES, tm, LANES), lambda i, j: (j, i, 0)),
        out_shape=jax.ShapeDtypeStruct((n // LANES, t, LANES), BF16),
        scratch_shapes=[pltpu.VMEM((tm, d), BF16)],
        compiler_params=_params("parallel", "arbitrary"),
        name="norm_proj",
    )(h, gain.reshape(1, d), w, cos_t, sin_t)


def _mla_a_kernel(x_ref, g_ref, w_ref, qn_ref, kvn_ref, cos_ref, sin_ref, cq_ref, ckv_ref, kpe_ref):
    xn = _rms(x_ref[...], g_ref[...]).astype(BF16)
    y = _dot(xn, w_ref[...])
    cq_ref[...] = _rms(y[:, :MLA_Q_RANK], qn_ref[...]).astype(BF16)
    ckv_ref[...] = _rms(y[:, MLA_Q_RANK:MLA_Q_RANK + MLA_KV_RANK], kvn_ref[...]).astype(BF16)
    kpe_ref[...] = _rope(y[:, MLA_Q_RANK + MLA_KV_RANK:], cos_ref[...], sin_ref[...]).astype(BF16)


def _mla_a(h, gain, w_a, q_norm, kv_norm, cos_t, sin_t, *, seq, tm=1024):
    t, d = h.shape
    n = w_a.shape[1]
    ns = seq // tm
    row = lambda i: (i, 0)
    fixed = lambda i: (0, 0)
    tab = lambda i: (i % ns, 0)
    return pl.pallas_call(
        _mla_a_kernel,
        grid=(t // tm,),
        in_specs=[
            pl.BlockSpec((tm, d), row),
            pl.BlockSpec((1, d), fixed),
            pl.BlockSpec((d, n), fixed),
            pl.BlockSpec((1, MLA_Q_RANK), fixed),
            pl.BlockSpec((1, MLA_KV_RANK), fixed),
            pl.BlockSpec((tm, LANES), tab),
            pl.BlockSpec((tm, LANES), tab),
        ],
        out_specs=[
            pl.BlockSpec((tm, MLA_Q_RANK), row),
            pl.BlockSpec((tm, MLA_KV_RANK), row),
            pl.BlockSpec((tm, LANES), row),
        ],
        out_shape=[
            jax.ShapeDtypeStruct((t, MLA_Q_RANK), BF16),
            jax.ShapeDtypeStruct((t, MLA_KV_RANK), BF16),
            jax.ShapeDtypeStruct((t, LANES), BF16),
        ],
        compiler_params=_params("parallel"),
        name="mla_a",
    )(h, gain.reshape(1, d), w_a, q_norm.reshape(1, -1), kv_norm.reshape(1, -1), cos_t, sin_t)


def _mla_b_kernel(cq_ref, ckv_ref, kpe_ref, wq_ref, wk_ref, wv_ref, cos_ref, sin_ref, o_ref):
    q = _dot(cq_ref[...], wq_ref[...])
    for c in range(MLA_HEADS):
        blk = q[:, c * LANES:(c + 1) * LANES]
        o_ref[c] = _rope(blk, cos_ref[...], sin_ref[...]).astype(BF16)
    k = _dot(ckv_ref[...], wk_ref[...])
    kpe = kpe_ref[...].astype(F32)
    for c in range(MLA_HEADS):
        o_ref[MLA_HEADS + c] = (k[:, c * LANES:(c + 1) * LANES] + kpe).astype(BF16)
    v = _dot(ckv_ref[...], wv_ref[...])
    for c in range(MLA_HEADS // 2):
        o_ref[2 * MLA_HEADS + c] = v[:, c * LANES:(c + 1) * LANES].astype(BF16)


def _mla_b(cq, ckv, kpe, wq, wk, wv, cos_t, sin_t, *, seq, tm=512):
    t = cq.shape[0]
    ns = seq // tm
    nb = 2 * MLA_HEADS + MLA_HEADS // 2
    row = lambda i: (i, 0)
    fixed = lambda i: (0, 0)
    tab = lambda i: (i % ns, 0)
    return pl.pallas_call(
        _mla_b_kernel,
        grid=(t // tm,),
        in_specs=[
            pl.BlockSpec((tm, MLA_Q_RANK), row),
            pl.BlockSpec((tm, MLA_KV_RANK), row),
            pl.BlockSpec((tm, LANES), row),
            pl.BlockSpec(wq.shape, fixed),
            pl.BlockSpec(wk.shape, fixed),
            pl.BlockSpec(wv.shape, fixed),
            pl.BlockSpec((tm, LANES), tab),
            pl.BlockSpec((tm, LANES), tab),
        ],
        out_specs=pl.BlockSpec((nb, tm, LANES), lambda i: (0, i, 0)),
        out_shape=jax.ShapeDtypeStruct((nb, t, LANES), BF16),
        compiler_params=_params("parallel"),
        name="mla_b",
    )(cq, ckv, kpe, wq, wk, wv, cos_t, sin_t)


def _fox_gate_kernel(x_ref, g_ref, wft_ref, b_ref, o_ref, carry_ref, *, tm):
    @pl.when(pl.program_id(1) == 0)
    def _():
        carry_ref[...] = jnp.zeros_like(carry_ref)

    xn = _rms(x_ref[...], g_ref[...]).astype(BF16)
    z = _dot_nt(wft_ref[...], xn) + b_ref[...]
    logf = jnp.minimum(z, 0.0) - jnp.log(1.0 + jnp.exp(-jnp.abs(z)))
    upper = (lax.broadcasted_iota(jnp.int32, (tm, tm), 0)
             <= lax.broadcasted_iota(jnp.int32, (tm, tm), 1)).astype(BF16)
    hi = logf.astype(BF16)
    r1 = logf - hi.astype(F32)
    mid = r1.astype(BF16)
    lo = (r1 - mid.astype(F32)).astype(BF16)
    cum = _dot(hi, upper) + _dot(mid, upper) + _dot(lo, upper) + carry_ref[...]
    o_ref[0] = -cum
    carry_ref[...] = cum[:, tm - 1:tm]


def _fox_gate(h, gain, w_f, b_f, *, batch, seq, tm=512):
    t, d = h.shape
    nh = w_f.shape[1]
    ns = seq // tm
    return pl.pallas_call(
        functools.partial(_fox_gate_kernel, tm=tm),
        grid=(batch, ns),
        in_specs=[
            pl.BlockSpec((tm, d), lambda b, i: (b * ns + i, 0)),
            pl.BlockSpec((1, d), lambda b, i: (0, 0)),
            pl.BlockSpec((nh, d), lambda b, i: (0, 0)),
            pl.BlockSpec((nh, 1), lambda b, i: (0, 0)),
        ],
        out_specs=pl.BlockSpec((1, nh, tm), lambda b, i: (b, 0, i)),
        out_shape=jax.ShapeDtypeStruct((batch, nh, seq), F32),
        scratch_shapes=[pltpu.VMEM((nh, 1), F32)],
        compiler_params=_params("parallel", "arbitrary"),
        name="fox_gate",
    )(h, gain.reshape(1, d), w_f.T.astype(BF16), b_f.reshape(nh, 1).astype(F32))


def _flash_kernel(*refs, sep, has_bias, tq):
    if has_bias:
        q_ref, k_ref, v_ref, b_ref, o_ref, qs_ref, acc_ref = refs
    else:
        q_ref, k_ref, v_ref, o_ref, qs_ref, acc_ref = refs
        b_ref = None
    hp = pl.program_id(1)
    i = pl.program_id(2)
    lane = lax.broadcasted_iota(jnp.int32, (tq, LANES), 1)
    first = lane < LANES // 2
    if not sep:
        q = q_ref[0]
        zero = jnp.zeros_like(q)
        qs_ref[0] = jnp.where(first, q, zero)
        qs_ref[1] = jnp.where(first, zero, q)
    row = lax.broadcasted_iota(jnp.int32, (tq, tq), 0)
    col = lax.broadcasted_iota(jnp.int32, (tq, tq), 1)
    acc_ref[...] = jnp.zeros_like(acc_ref)

    def block(start, carry, diag):
        v_blk = v_ref[0, pl.ds(start, tq), :]
        out = []
        for hh in range(2):
            m_old, l_old = carry[2 * hh], carry[2 * hh + 1]
            q_h = q_ref[hh] if sep else qs_ref[hh]
            k_h = k_ref[hh if sep else 0, pl.ds(start, tq), :]
            s = _dot_nt(q_h, k_h)
            if has_bias:
                s = s + b_ref[0, pl.ds(2 * hp + hh, 1), pl.ds(start, tq)]
            if diag:
                s = jnp.where(col <= row, s, NEG_INF)
            m_new = jnp.maximum(m_old, jnp.max(s, axis=-1, keepdims=True))
            alpha = jnp.exp(m_old - m_new)
            p = jnp.exp(s - m_new)
            l_new = alpha * l_old + jnp.sum(p, axis=-1, keepdims=True)
            acc_ref[hh] = alpha * acc_ref[hh] + _dot(p.astype(BF16), v_blk)
            out += [m_new, l_new]
        return tuple(out)

    m0 = jnp.full((tq, 1), NEG_INF, F32)
    l0 = jnp.zeros((tq, 1), F32)
    carry = lax.fori_loop(
        0, i, lambda kb, c: block(pl.multiple_of(kb * tq, tq), c, False), (m0, l0, m0, l0))
    _, l_a, _, l_b = block(pl.multiple_of(i * tq, tq), carry, True)
    o_ref[0] = jnp.where(first, acc_ref[0] / l_a, acc_ref[1] / l_b).astype(o_ref.dtype)


def _flash(qkv, bias, *, batch, seq, n_pairs, q_blk, k_blk, v_blk, sep, tq=512):
    t = qkv.shape[1]
    nq = seq // tq
    w = 2 if sep else 1
    in_specs = [
        pl.BlockSpec((w, tq, LANES), lambda b, hp, i: (q_blk // w + hp, b * nq + i, 0)),
        pl.BlockSpec((w, seq, LANES), lambda b, hp, i: (k_blk // w + hp, b, 0)),
        pl.BlockSpec((1, seq, LANES), lambda b, hp, i: (v_blk + hp, b, 0)),
    ]
    args = [qkv, qkv, qkv]
    if bias is not None:
        in_specs.append(pl.BlockSpec((1, bias.shape[1], seq), lambda b, hp, i: (b, 0, 0)))
        args.append(bias)
    return pl.pallas_call(
        functools.partial(_flash_kernel, sep=sep, has_bias=bias is not None, tq=tq),
        grid=(batch, n_pairs, nq),
        in_specs=in_specs,
        out_specs=pl.BlockSpec((1, tq, LANES), lambda b, hp, i: (hp, b * nq + i, 0)),
        out_shape=jax.ShapeDtypeStruct((n_pairs, t, LANES), BF16),
        scratch_shapes=[pltpu.VMEM((2, tq, LANES), BF16), pltpu.VMEM((2, tq, LANES), F32)],
        compiler_params=_params("parallel", "parallel", "arbitrary"),
        name="flash_sep" if sep else "flash_shared",
    )(*args)


def _dil_kernel(q_ref, k_ref, kp_ref, v_ref, vp_ref, o_ref, lse_ref, *, d, rows):
    i = pl.program_id(2)
    w = DIL_WINDOW_KEYS
    lane = lax.broadcasted_iota(jnp.int32, (w, LANES), 1)
    qk_first = (lane % 64) < 32
    v_first = lane < 64
    qi = lax.broadcasted_iota(jnp.int32, (w, 2 * w), 0)
    kk = lax.broadcasted_iota(jnp.int32, (w, 2 * w), 1)
    band = jnp.logical_and(kk >= qi, kk <= qi + w)
    band_first = jnp.logical_and(band, kk >= jnp.where(i > 0, 0, w))
    for r in range(d):
        ls = slice(r * LANES, (r + 1) * LANES)
        for sub in range(rows // w):
            q_u = q_ref[0, 0, sub * w:(sub + 1) * w, ls]
            if sub == 0:
                k_u = jnp.concatenate([kp_ref[0, 0, :, ls], k_ref[0, 0, 0:w, ls]], axis=0)
                v_u = jnp.concatenate([vp_ref[0, 0, :, ls], v_ref[0, 0, 0:w, ls]], axis=0)
                msk = band_first
            else:
                k_u = k_ref[0, 0, (sub - 1) * w:(sub + 1) * w, ls]
                v_u = v_ref[0, 0, (sub - 1) * w:(sub + 1) * w, ls]
                msk = band
            zero = jnp.zeros_like(q_u)
            res = []
            for hh in range(2):
                q_h = jnp.where(qk_first, q_u, zero) if hh == 0 else jnp.where(qk_first, zero, q_u)
                s = jnp.where(msk, _dot_nt(q_h, k_u), NEG_INF)
                m = jnp.max(s, axis=-1, keepdims=True)
                p = jnp.exp(s - m)
                l = jnp.sum(p, axis=-1, keepdims=True)
                o = _dot(p.astype(BF16), v_u) / l
                res.append((o, m + jnp.log(l)))
            o_ref[0, 0, sub * w:(sub + 1) * w, ls] = jnp.where(v_first, res[0][0], res[1][0]).astype(o_ref.dtype)
            lse_ref[0, 0, sub * w:(sub + 1) * w, ls] = jnp.where(
                v_first, jnp.broadcast_to(res[0][1], (w, LANES)), jnp.broadcast_to(res[1][1], (w, LANES)))


def _dil_attn(qkv, *, batch, seq, group, d):
    nb, t, _ = qkv.shape
    n_pairs = DIL_HEADS // 2
    rows = DIL_TOKENS // d
    view = qkv.reshape(nb, batch, seq // d, d * LANES)
    sub_per_blk = rows // DIL_WINDOW_KEYS
    qb, kb, vb = group * n_pairs, (3 + group) * n_pairs, (6 + group) * n_pairs
    cur = lambda base: pl.BlockSpec((1, 1, rows, d * LANES), lambda b, hp, i: (base + hp, b, i, 0))
    prev = lambda base: pl.BlockSpec(
        (1, 1, DIL_WINDOW_KEYS, d * LANES),
        lambda b, hp, i: (base + hp, b, jnp.maximum(i * sub_per_blk - 1, 0), 0))
    out_spec = pl.BlockSpec((1, 1, rows, d * LANES), lambda b, hp, i: (hp, b, i, 0))
    o, lse = pl.pallas_call(
        functools.partial(_dil_kernel, d=d, rows=rows),
        grid=(batch, n_pairs, seq // DIL_TOKENS),
        in_specs=[cur(qb), cur(kb), prev(kb), cur(vb), prev(vb)],
        out_specs=[out_spec, out_spec],
        out_shape=[
            jax.ShapeDtypeStruct((n_pairs, batch, seq // d, d * LANES), BF16),
            jax.ShapeDtypeStruct((n_pairs, batch, seq // d, d * LANES), F32),
        ],
        compiler_params=_params("parallel", "parallel", "arbitrary"),
        name="dil_attn_d%d" % d,
    )(view, view, view, view, view)
    return o.reshape(n_pairs, t, LANES), lse.reshape(n_pairs, t, LANES)


def _out_proj_kernel(h_ref, o_ref, w_ref, out_ref):
    nb = o_ref.shape[0]
    a = jnp.concatenate([o_ref[c] for c in range(nb)], axis=-1)
    out_ref[...] = h_ref[...] + _dot(a, w_ref[...])


def _dil_out_proj_kernel(h_ref, o0_ref, o1_ref, o2_ref, l0_ref, l1_ref, l2_ref, w_ref, out_ref):
    nb = o0_ref.shape[0]
    cols = []
    for c in range(nb):
        l0, l1, l2 = l0_ref[c], l1_ref[c], l2_ref[c]
        m = jnp.maximum(jnp.maximum(l0, l1), l2)
        e0, e1, e2 = jnp.exp(l0 - m), jnp.exp(l1 - m), jnp.exp(l2 - m)
        num = e0 * o0_ref[c].astype(F32) + e1 * o1_ref[c].astype(F32) + e2 * o2_ref[c].astype(F32)
        cols.append((num / (e0 + e1 + e2)).astype(BF16))
    out_ref[...] = h_ref[...] + _dot(jnp.concatenate(cols, axis=-1), w_ref[...])


def _out_proj(h, outs, lses, wo, *, tm=512):
    t, d = h.shape
    nb = outs[0].shape[0]
    row = pl.BlockSpec((tm, d), lambda i: (i, 0))
    blk = pl.BlockSpec((nb, tm, LANES), lambda i: (0, i, 0))
    kern = _dil_out_proj_kernel if lses else _out_proj_kernel
    n_blk = len(outs) + len(lses)
    return pl.pallas_call(
        kern,
        grid=(t // tm,),
        in_specs=[row] + [blk] * n_blk + [pl.BlockSpec(wo.shape, lambda i: (0, 0))],
        out_specs=row,
        out_shape=jax.ShapeDtypeStruct((t, d), F32),
        compiler_params=_params("parallel"),
        name="dil_out_proj" if lses else "out_proj",
    )(h, *outs, *lses, wo)


def _mlp_kernel(h_ref, g_ref, wu_ref, wd_ref, fg_ref, out_ref, xn_ref, acc_ref, *, final_norm):
    f = pl.program_id(1)

    @pl.when(f == 0)
    def _():
        xn_ref[...] = _rms(h_ref[...], g_ref[...]).astype(BF16)
        acc_ref[...] = h_ref[...]

    u = jnp.maximum(_dot(xn_ref[...], wu_ref[...]), 0.0)
    acc_ref[...] += _dot((u * u).astype(BF16), wd_ref[...])

    @pl.when(f == pl.num_programs(1) - 1)
    def _():
        y = acc_ref[...]
        out_ref[...] = _rms(y, fg_ref[...]) if final_norm else y


def _mlp(h, gain, w_up, w_down, final_gain, *, final_norm, tm=1024, tf=1024):
    t, d = h.shape
    dff = w_up.shape[1]
    return pl.pallas_call(
        functools.partial(_mlp_kernel, final_norm=final_norm),
        grid=(t // tm, dff // tf),
        in_specs=[
            pl.BlockSpec((tm, d), lambda i, f: (i, 0)),
            pl.BlockSpec((1, d), lambda i, f: (0, 0)),
            pl.BlockSpec((d, tf), lambda i, f: (0, f)),
            pl.BlockSpec((tf, d), lambda i, f: (f, 0)),
            pl.BlockSpec((1, d), lambda i, f: (0, 0)),
        ],
        out_specs=pl.BlockSpec((tm, d), lambda i, f: (i, 0)),
        out_shape=jax.ShapeDtypeStruct((t, d), F32),
        scratch_shapes=[pltpu.VMEM((tm, d), BF16), pltpu.VMEM((tm, d), F32)],
        compiler_params=_params("parallel", "arbitrary"),
        name="mlp_final" if final_norm else "mlp",
    )(h, gain.reshape(1, d), w_up, w_down, final_gain.reshape(1, d))


def _rope_halves(seq, dim):
    inv = 1.0 / (ROPE_THETA ** (jnp.arange(0, dim, 2, dtype=F32) / dim))
    ang = jnp.arange(seq, dtype=F32)[:, None] * inv[None, :]
    return jnp.cos(ang), jnp.sin(ang)


def _mla_tables(seq):
    cos, sin = _rope_halves(seq, MLA_ROPE)
    one = lambda n: jnp.ones((seq, n), F32)
    zero = lambda n: jnp.zeros((seq, n), F32)
    cos_t = jnp.concatenate([one(32), cos, one(16), one(32), cos, one(16)], axis=1)
    sin_t = jnp.concatenate([zero(32), -sin, zero(16), zero(32), sin, zero(16)], axis=1)
    return cos_t, sin_t


def _dil_tables(seq):
    cos, sin = _rope_halves(seq, DIL_DIM)
    return jnp.concatenate([cos] * 4, axis=1), jnp.concatenate([-sin, -sin, sin, sin], axis=1)


def _mla_weights(wq_a, wq_b, wkv_a, wkv_b):
    d = wq_a.shape[0]
    z = lambda r, n: jnp.zeros((r, n), F32)
    kpe = wkv_a[:, MLA_KV_RANK:]
    w_a = jnp.concatenate(
        [wq_a, wkv_a[:, :MLA_KV_RANK], z(d, 32), kpe[:, :16], z(d, 48), kpe[:, 16:], z(d, 16)], axis=1)
    scale = (MLA_NOPE + MLA_ROPE) ** -0.5
    q3 = wq_b.reshape(MLA_Q_RANK, MLA_HEADS, MLA_NOPE + MLA_ROPE) * scale
    zq = jnp.zeros((MLA_Q_RANK, MLA_HEADS, 16), F32)
    wq = jnp.concatenate(
        [q3[..., :32], q3[..., 64:80], zq, q3[..., 32:64], q3[..., 80:96], zq], axis=-1)
    kv3 = wkv_b.reshape(MLA_KV_RANK, MLA_HEADS, MLA_NOPE + MLA_V)
    zk = jnp.zeros((MLA_KV_RANK, MLA_HEADS, 32), F32)
    wk = jnp.concatenate([kv3[..., :32], zk, kv3[..., 32:64], zk], axis=-1)
    wv = kv3[..., MLA_NOPE:]
    return (w_a.astype(BF16), wq.reshape(MLA_Q_RANK, -1).astype(BF16),
            wk.reshape(MLA_KV_RANK, -1).astype(BF16), wv.reshape(MLA_KV_RANK, -1).astype(BF16))


def _dil_weights(w_qkv):
    d = w_qkv.shape[0]
    n_qk = 2 * len(DIL_PATTERNS) * DIL_HEADS * DIL_DIM
    qk = w_qkv[:, :n_qk].reshape(d, 2, n_qk // (2 * LANES), 2, 2, 32)
    qk = qk * jnp.array([DIL_DIM ** -0.5, 1.0], F32).reshape(1, 2, 1, 1, 1, 1)
    qk = qk.transpose(0, 1, 2, 4, 3, 5).reshape(d, n_qk)
    return jnp.concatenate([qk, w_qkv[:, n_qk:]], axis=1).astype(BF16)


def _mla_layer(h, attn_norm, wq_a, q_norm, wq_b, wkv_a, kv_norm, wkv_b, wo, tables, *, batch, seq):
    w_a, wq, wk, wv = _mla_weights(wq_a, wq_b, wkv_a, wkv_b)
    cq, ckv, kpe = _mla_a(h, attn_norm, w_a, q_norm, kv_norm, *tables, seq=seq)
    qkv = _mla_b(cq, ckv, kpe, wq, wk, wv, *tables, seq=seq)
    o = _flash(qkv, None, batch=batch, seq=seq, n_pairs=MLA_HEADS // 2,
               q_blk=0, k_blk=MLA_HEADS, v_blk=2 * MLA_HEADS, sep=True)
    return _out_proj(h, [o], [], wo.astype(BF16))


def _fox_layer(h, attn_norm, w_qkv, w_f, b_f, wo, dummy_tables, *, batch, seq):
    n_q = FOX_HEADS * FOX_DIM
    w = jnp.concatenate([w_qkv[:, :n_q] * FOX_DIM ** -0.5, w_qkv[:, n_q:]], axis=1).astype(BF16)
    qkv = _norm_proj(h, attn_norm, w, *dummy_tables, seq=seq, n_rope_tiles=0)
    neg_cum = _fox_gate(h, attn_norm, w_f, b_f, batch=batch, seq=seq)
    n_pairs = FOX_HEADS // 2
    o = _flash(qkv, neg_cum, batch=batch, seq=seq, n_pairs=n_pairs,
               q_blk=0, k_blk=n_pairs, v_blk=2 * n_pairs, sep=False)
    return _out_proj(h, [o], [], wo.astype(BF16))


def _dil_layer(h, attn_norm, w_qkv, wo, tables, *, batch, seq):
    w = _dil_weights(w_qkv)
    n_rope_tiles = 2 * len(DIL_PATTERNS) * DIL_HEADS * DIL_DIM // 1024
    qkv = _norm_proj(h, attn_norm, w, *tables, seq=seq, n_rope_tiles=n_rope_tiles)
    outs, lses = [], []
    for g, (_, d) in enumerate(DIL_PATTERNS):
        o, lse = _dil_attn(qkv, batch=batch, seq=seq, group=g, d=d)
        outs.append(o)
        lses.append(lse)
    return _out_proj(h, outs, lses, wo.astype(BF16))


def kernel(x, l0_attn_norm, l0_mla_wq_a, l0_mla_q_norm, l0_mla_wq_b, l0_mla_wkv_a, l0_mla_kv_norm, l0_mla_wkv_b, l0_mla_wo, l0_mlp_norm, l0_w_up, l0_w_down, l1_attn_norm, l1_fox_w_qkv, l1_fox_w_f, l1_fox_b_f, l1_fox_wo, l1_mlp_norm, l1_w_up, l1_w_down, l2_attn_norm, l2_dil_w_qkv, l2_dil_wo, l2_mlp_norm, l2_w_up, l2_w_down, l3_attn_norm, l3_mla_wq_a, l3_mla_q_norm, l3_mla_wq_b, l3_mla_wkv_a, l3_mla_kv_norm, l3_mla_wkv_b, l3_mla_wo, l3_mlp_norm, l3_w_up, l3_w_down, final_norm):
    batch, seq, d = x.shape
    assert seq % DIL_TOKENS == 0 and (batch * seq) % 1024 == 0
    for window, dil in DIL_PATTERNS:
        assert window // dil == DIL_WINDOW_KEYS
    mla_t = _mla_tables(seq)
    dil_t = _dil_tables(seq)
    kw = dict(batch=batch, seq=seq)
    h = x.reshape(batch * seq, d)

    h = _mla_layer(h, l0_attn_norm, l0_mla_wq_a, l0_mla_q_norm, l0_mla_wq_b, l0_mla_wkv_a,
                   l0_mla_kv_norm, l0_mla_wkv_b, l0_mla_wo, mla_t, **kw)
    h = _mlp(h, l0_mlp_norm, l0_w_up.astype(BF16), l0_w_down.astype(BF16), final_norm, final_norm=False)

    h = _fox_layer(h, l1_attn_norm, l1_fox_w_qkv, l1_fox_w_f, l1_fox_b_f, l1_fox_wo, mla_t, **kw)
    h = _mlp(h, l1_mlp_norm, l1_w_up.astype(BF16), l1_w_down.astype(BF16), final_norm, final_norm=False)

    h = _dil_layer(h, l2_attn_norm, l2_dil_w_qkv, l2_dil_wo, dil_t, **kw)
    h = _mlp(h, l2_mlp_norm, l2_w_up.astype(BF16), l2_w_down.astype(BF16), final_norm, final_norm=False)

    h = _mla_layer(h, l3_attn_norm, l3_mla_wq_a, l3_mla_q_norm, l3_mla_wq_b, l3_mla_wkv_a,
                   l3_mla_kv_norm, l3_mla_wkv_b, l3_mla_wo, mla_t, **kw)
    h = _mlp(h, l3_mlp_norm, l3_w_up.astype(BF16), l3_w_down.astype(BF16), final_norm, final_norm=True)
    return h.reshape(batch, seq, d)
```

```python
import functools
import math

import numpy as np
import jax
import jax.numpy as jnp
from jax import lax
from jax.experimental import pallas as pl
from jax.experimental.pallas import tpu as pltpu

F32 = jnp.float32
BF16 = jnp.bfloat16

LANES = 128
NORM_EPS = 1e-6
NEG_INF = -1e30
ROPE_THETA = 10000.0
LOG2E = math.log2(math.e)

MLA_HEADS = 16
MLA_Q_RANK = 384
MLA_KV_RANK = 256
MLA_NOPE = 64
MLA_ROPE = 32
MLA_V = 64
FOX_HEADS = 16
FOX_DIM = 64
DIL_PATTERNS = ((128, 1), (512, 4), (2048, 16))
DIL_HEADS = 16
DIL_DIM = 64
DIL_WINDOW_KEYS = 128
DIL_TOKENS = 2048
GATE_PARTS = 3

VMEM_LIMIT = 56 * 2**20


def _params(*sem):
    return pltpu.CompilerParams(dimension_semantics=sem, vmem_limit_bytes=VMEM_LIMIT)


def _rms(x, g):
    ms = jnp.mean(x * x, axis=-1, keepdims=True)
    return x * lax.rsqrt(ms + NORM_EPS) * g


def _dot(a, b):
    return jnp.dot(a, b, preferred_element_type=F32)


def _dot_nt(a, b):
    return lax.dot_general(a, b, (((1,), (1,)), ((), ())), preferred_element_type=F32)


def _rope(blk, cos, sin):
    return blk * cos + pltpu.roll(blk, 64, axis=1) * sin


def _colmax(st):
    n, c = st.shape
    return jnp.max(jnp.max(st.reshape(8, n // 8, c), axis=0), axis=0, keepdims=True)


def _split3(x):
    hi = x.astype(BF16)
    r1 = x - hi.astype(F32)
    mid = r1.astype(BF16)
    lo = (r1 - mid.astype(F32)).astype(BF16)
    return hi, mid, lo


def _norm_proj_kernel(x_ref, g_ref, w_ref, cos_ref, sin_ref, o_ref, xn_ref, *, n_rope_tiles, tn):
    j = pl.program_id(1)

    @pl.when(j == 0)
    def _():
        xn_ref[...] = _rms(x_ref[...], g_ref[...]).astype(BF16)

    acc = _dot(xn_ref[...], w_ref[...])

    def store(rope):
        for c in range(tn // LANES):
            blk = acc[:, c * LANES:(c + 1) * LANES]
            if rope:
                blk = _rope(blk, cos_ref[...], sin_ref[...])
            o_ref[c] = blk.astype(o_ref.dtype)

    if n_rope_tiles == 0:
        store(False)
    else:
        pl.when(j < n_rope_tiles)(lambda: store(True))
        pl.when(j >= n_rope_tiles)(lambda: store(False))


def _norm_proj(h, gain, w, cos_t, sin_t, *, seq, n_rope_tiles, tm=1024, tn=1024):
    t, d = h.shape
    n = w.shape[1]
    ns = seq // tm
    return pl.pallas_call(
        functools.partial(_norm_proj_kernel, n_rope_tiles=n_rope_tiles, tn=tn),
        grid=(t // tm, n // tn),
        in_specs=[
            pl.BlockSpec((tm, d), lambda i, j: (i, 0)),
            pl.BlockSpec((1, d), lambda i, j: (0, 0)),
            pl.BlockSpec((d, tn), lambda i, j: (0, j)),
            pl.BlockSpec((tm, LANES), lambda i, j: (i % ns, 0)),
            pl.BlockSpec((tm, LANES), lambda i, j: (i % ns, 0)),
        ],
        out_specs=pl.BlockSpec((tn // LANES, tm, LANES), lambda i, j: (j, i, 0)),
        out_shape=jax.ShapeDtypeStruct((n // LANES, t, LANES), BF16),
        scratch_shapes=[pltpu.VMEM((tm, d), BF16)],
        compiler_params=_params("parallel", "arbitrary"),
        name="norm_proj",
    )(h, gain.reshape(1, d), w, cos_t, sin_t)


def _mla_a_kernel(x_ref, g_ref, w_ref, qn_ref, kvn_ref, cos_ref, sin_ref, cq_ref, ckv_ref, kpe_ref):
    xn = _rms(x_ref[...], g_ref[...]).astype(BF16)
    y = _dot(xn, w_ref[...])
    cq_ref[...] = _rms(y[:, :MLA_Q_RANK], qn_ref[...]).astype(BF16)
    ckv_ref[...] = _rms(y[:, MLA_Q_RANK:MLA_Q_RANK + MLA_KV_RANK], kvn_ref[...]).astype(BF16)
    kpe_ref[...] = _rope(y[:, MLA_Q_RANK + MLA_KV_RANK:], cos_ref[...], sin_ref[...]).astype(BF16)


def _mla_a(h, gain, w_a, q_norm, kv_norm, cos_t, sin_t, *, seq, tm=1024):
    t, d = h.shape
    n = w_a.shape[1]
    ns = seq // tm
    row = lambda i: (i, 0)
    fixed = lambda i: (0, 0)
    tab = lambda i: (i % ns, 0)
    return pl.pallas_call(
        _mla_a_kernel,
        grid=(t // tm,),
        in_specs=[
            pl.BlockSpec((tm, d), row),
            pl.BlockSpec((1, d), fixed),
            pl.BlockSpec((d, n), fixed),
            pl.BlockSpec((1, MLA_Q_RANK), fixed),
            pl.BlockSpec((1, MLA_KV_RANK), fixed),
            pl.BlockSpec((tm, LANES), tab),
            pl.BlockSpec((tm, LANES), tab),
        ],
        out_specs=[
            pl.BlockSpec((tm, MLA_Q_RANK), row),
            pl.BlockSpec((tm, MLA_KV_RANK), row),
            pl.BlockSpec((tm, LANES), row),
        ],
        out_shape=[
            jax.ShapeDtypeStruct((t, MLA_Q_RANK), BF16),
            jax.ShapeDtypeStruct((t, MLA_KV_RANK), BF16),
            jax.ShapeDtypeStruct((t, LANES), BF16),
        ],
        compiler_params=_params("parallel"),
        name="mla_a",
    )(h, gain.reshape(1, d), w_a, q_norm.reshape(1, -1), kv_norm.reshape(1, -1), cos_t, sin_t)


def _mla_b_kernel(cq_ref, ckv_ref, kpe_ref, wq_ref, wk_ref, wvt_ref, cos_ref, sin_ref, qk_ref, vt_ref):
    q = _dot(cq_ref[...], wq_ref[...])
    for c in range(MLA_HEADS):
        blk = q[:, c * LANES:(c + 1) * LANES]
        qk_ref[c] = _rope(blk, cos_ref[...], sin_ref[...]).astype(BF16)
    k = _dot(ckv_ref[...], wk_ref[...])
    kpe = kpe_ref[...].astype(F32)
    for c in range(MLA_HEADS):
        qk_ref[MLA_HEADS + c] = (k[:, c * LANES:(c + 1) * LANES] + kpe).astype(BF16)
    vt = _dot_nt(wvt_ref[...], ckv_ref[...])
    for c in range(MLA_HEADS // 2):
        vt_ref[c] = vt[c * LANES:(c + 1) * LANES, :].astype(BF16)


def _mla_b(cq, ckv, kpe, wq, wk, wvt, cos_t, sin_t, *, seq, tm=512):
    t = cq.shape[0]
    ns = seq // tm
    row = lambda i: (i, 0)
    fixed = lambda i: (0, 0)
    tab = lambda i: (i % ns, 0)
    return pl.pallas_call(
        _mla_b_kernel,
        grid=(t // tm,),
        in_specs=[
            pl.BlockSpec((tm, MLA_Q_RANK), row),
            pl.BlockSpec((tm, MLA_KV_RANK), row),
            pl.BlockSpec((tm, LANES), row),
            pl.BlockSpec(wq.shape, fixed),
            pl.BlockSpec(wk.shape, fixed),
            pl.BlockSpec(wvt.shape, fixed),
            pl.BlockSpec((tm, LANES), tab),
            pl.BlockSpec((tm, LANES), tab),
        ],
        out_specs=[
            pl.BlockSpec((2 * MLA_HEADS, tm, LANES), lambda i: (0, i, 0)),
            pl.BlockSpec((MLA_HEADS // 2, LANES, tm), lambda i: (0, 0, i)),
        ],
        out_shape=[
            jax.ShapeDtypeStruct((2 * MLA_HEADS, t, LANES), BF16),
            jax.ShapeDtypeStruct((MLA_HEADS // 2, LANES, t), BF16),
        ],
        compiler_params=_params("parallel"),
        name="mla_b",
    )(cq, ckv, kpe, wq, wk, wvt, cos_t, sin_t)


def _fox_proj_kernel(x_ref, g_ref, wqk_ref, wvt_ref, wf_ref, bf_ref, tri_ref, scat_ref,
                     qk_ref, vt_ref, kd_ref, carry_ref, *, tm):
    @pl.when(pl.program_id(1) == 0)
    def _():
        carry_ref[...] = jnp.zeros_like(carry_ref)

    xn = _rms(x_ref[...], g_ref[...]).astype(BF16)
    qk = _dot(xn, wqk_ref[...])
    for c in range(qk_ref.shape[0]):
        qk_ref[c] = qk[:, c * LANES:(c + 1) * LANES].astype(BF16)
    vt = _dot_nt(wvt_ref[...], xn)
    for c in range(vt_ref.shape[0]):
        vt_ref[c] = vt[c * LANES:(c + 1) * LANES, :].astype(BF16)

    z = _dot(xn, wf_ref[...]) + bf_ref[...]
    logf = jnp.minimum(z, 0.0) - jnp.log(1.0 + jnp.exp(-jnp.abs(z)))
    tri = tri_ref[...]
    cum = carry_ref[...] + sum(_dot(tri, part) for part in _split3(logf))
    carry_ref[...] = cum[tm - 1:tm, :]
    kd = sum(_dot(part, scat_ref[p]) for p, part in enumerate(_split3(cum * (-LOG2E))))
    for c in range(kd_ref.shape[0]):
        kd_ref[c] = kd[:, c * LANES:(c + 1) * LANES].astype(BF16)


def _fox_proj(h, gain, wqk, wvt, wf, bf, *, batch, seq, tm=512):
    t, d = h.shape
    ns = seq // tm
    n_pairs = FOX_HEADS // 2
    tri = jnp.tril(jnp.ones((tm, tm), BF16))
    scat = np.zeros((GATE_PARTS, LANES, n_pairs * LANES), np.float32)
    for p in range(GATE_PARTS):
        for hd in range(FOX_HEADS):
            scat[p, hd, (hd // 2) * LANES + GATE_PARTS * (hd % 2) + p] = 1.0
    fixed2 = lambda b, i: (0, 0)
    rows = lambda b, i: (0, b * ns + i, 0)
    return pl.pallas_call(
        functools.partial(_fox_proj_kernel, tm=tm),
        grid=(batch, ns),
        in_specs=[
            pl.BlockSpec((tm, d), lambda b, i: (b * ns + i, 0)),
            pl.BlockSpec((1, d), fixed2),
            pl.BlockSpec(wqk.shape, fixed2),
            pl.BlockSpec(wvt.shape, fixed2),
            pl.BlockSpec(wf.shape, fixed2),
            pl.BlockSpec((1, LANES), fixed2),
            pl.BlockSpec((tm, tm), fixed2),
            pl.BlockSpec(scat.shape, lambda b, i: (0, 0, 0)),
        ],
        out_specs=[
            pl.BlockSpec((2 * n_pairs, tm, LANES), rows),
            pl.BlockSpec((n_pairs, LANES, tm), lambda b, i: (0, 0, b * ns + i)),
            pl.BlockSpec((n_pairs, tm, LANES), rows),
        ],
        out_shape=[
            jax.ShapeDtypeStruct((2 * n_pairs, t, LANES), BF16),
            jax.ShapeDtypeStruct((n_pairs, LANES, t), BF16),
            jax.ShapeDtypeStruct((n_pairs, t, LANES), BF16),
        ],
        scratch_shapes=[pltpu.VMEM((1, LANES), F32)],
        compiler_params=_params("arbitrary", "arbitrary"),
        name="fox_proj",
    )(h, gain.reshape(1, d), wqk, wvt, wf, bf, tri, jnp.asarray(scat, BF16))


def _flash_kernel(*refs, sep, has_bias, tq):
    if has_bias:
        q_ref, k_ref, vt_ref, kd_ref, o_ref, qs_ref, vta_ref, acc_ref, s_ref, mx_ref = refs
    else:
        q_ref, k_ref, vt_ref, o_ref, qs_ref, vta_ref, acc_ref, s_ref, mx_ref = refs
        kd_ref = None
    i = pl.program_id(2)
    half = LANES // 2

    @pl.when(i == 0)
    def _():
        vt = vt_ref[0]
        r = lax.broadcasted_iota(jnp.int32, vt.shape, 0)
        one = jnp.ones_like(vt)
        vta_ref[0] = jnp.where(r < half, vt, one)
        vta_ref[1] = jnp.where(r < half, one, vt)

    lane = lax.broadcasted_iota(jnp.int32, (tq, LANES), 1)
    for hh in range(2):
        if sep:
            q_h = q_ref[hh]
        else:
            q = q_ref[0]
            own = (lane < half) if hh == 0 else (lane >= half)
            q_h = jnp.where(own, q, jnp.zeros_like(q))
        qs_ref[hh, :, 0:LANES] = q_h
        if has_bias:
            sel = jnp.logical_and(lane >= GATE_PARTS * hh, lane < GATE_PARTS * (hh + 1))
            qs_ref[hh, :, LANES:2 * LANES] = jnp.where(sel, 1.0, 0.0).astype(BF16)
    key = lax.broadcasted_iota(jnp.int32, (tq, tq), 0)
    qry = lax.broadcasted_iota(jnp.int32, (tq, tq), 1)
    acc_ref[...] = jnp.zeros_like(acc_ref)

    def scores(kb, slot):
        start = pl.multiple_of(kb * tq, tq)
        for hh in range(2):
            k_blk = k_ref[hh if sep else 0, pl.ds(start, tq), :]
            if has_bias:
                k_blk = jnp.concatenate([k_blk, kd_ref[0, pl.ds(start, tq), :]], axis=1)
            st = _dot_nt(k_blk, qs_ref[hh])
            s_ref[slot, hh] = st
            mx_ref[slot, hh] = _colmax(st)

    def consume(kb, slot, m_old, diag):
        start = pl.multiple_of(kb * tq, tq)
        m_out = []
        for hh in range(2):
            st = s_ref[slot, hh]
            if diag:
                st = jnp.where(key <= qry, st, NEG_INF)
                mx = _colmax(st)
            else:
                mx = mx_ref[slot, hh]
            m_new = jnp.maximum(m_old[hh], mx)
            alpha = jnp.exp2(m_old[hh] - m_new)
            pt = jnp.exp2(st - m_new).astype(BF16)
            acc_ref[hh] = alpha * acc_ref[hh] + _dot(vta_ref[hh, :, pl.ds(start, tq)], pt)
            m_out.append(m_new)
        return tuple(m_out)

    def body(j, m_old):
        scores(2 * j + 1, 1)
        m_mid = consume(2 * j, 0, m_old, False)
        scores(2 * j + 2, 0)
        return consume(2 * j + 1, 1, m_mid, False)

    def write_out():
        acc_a, acc_b = acc_ref[0], acc_ref[1]
        r = lax.broadcasted_iota(jnp.int32, (LANES, tq), 0)
        o_t = jnp.where(r < half, acc_a / acc_a[half:half + 1, :], acc_b / acc_b[0:1, :])
        o_ref[0] = o_t.T.astype(o_ref.dtype)

    scores(0, 0)
    m0 = jnp.full((1, tq), NEG_INF, F32)
    m = lax.fori_loop(0, lax.shift_right_logical(i, 1), body, (m0, m0))

    @pl.when(i % 2 == 0)
    def _():
        consume(i, 0, m, True)
        write_out()

    @pl.when(i % 2 == 1)
    def _():
        scores(i, 1)
        consume(i, 1, consume(i - 1, 0, m, False), True)
        write_out()


def _flash(qk, vt, kd, *, batch, seq, n_pairs, q_blk, k_blk, sep, tq=512):
    t = qk.shape[1]
    nq = seq // tq
    w = 2 if sep else 1
    kc = 2 * LANES if kd is not None else LANES
    in_specs = [
        pl.BlockSpec((w, tq, LANES), lambda b, hp, i: (q_blk // w + hp, b * nq + i, 0)),
        pl.BlockSpec((w, seq, LANES), lambda b, hp, i: (k_blk // w + hp, b, 0)),
        pl.BlockSpec((1, LANES, seq), lambda b, hp, i: (hp, 0, b)),
    ]
    args = [qk, qk, vt]
    if kd is not None:
        in_specs.append(pl.BlockSpec((1, seq, LANES), lambda b, hp, i: (hp, b, 0)))
        args.append(kd)
    return pl.pallas_call(
        functools.partial(_flash_kernel, sep=sep, has_bias=kd is not None, tq=tq),
        grid=(batch, n_pairs, nq),
        in_specs=in_specs,
        out_specs=pl.BlockSpec((1, tq, LANES), lambda b, hp, i: (hp, b * nq + i, 0)),
        out_shape=jax.ShapeDtypeStruct((n_pairs, t, LANES), BF16),
        scratch_shapes=[
            pltpu.VMEM((2, tq, kc), BF16),
            pltpu.VMEM((2, LANES, seq), BF16),
            pltpu.VMEM((2, LANES, tq), F32),
            pltpu.VMEM((2, 2, tq, tq), F32),
            pltpu.VMEM((2, 2, 1, tq), F32),
        ],
        compiler_params=_params("arbitrary", "arbitrary", "arbitrary"),
        name="flash_sep" if sep else "flash_shared",
    )(*args)


def _dil_kernel(q_ref, k_ref, kp_ref, v_ref, vp_ref, o_ref, lse_ref, *, d, rows):
    i = pl.program_id(2)
    w = DIL_WINDOW_KEYS
    lane = lax.broadcasted_iota(jnp.int32, (w, LANES), 1)
    qk_first = (lane % 64) < 32
    v_first = lane < 64
    qi = lax.broadcasted_iota(jnp.int32, (w, 2 * w), 0)
    kk = lax.broadcasted_iota(jnp.int32, (w, 2 * w), 1)
    band = jnp.logical_and(kk >= qi, kk <= qi + w)
    band_first = jnp.logical_and(band, kk >= jnp.where(i > 0, 0, w))
    for r in range(d):
        ls = slice(r * LANES, (r + 1) * LANES)
        for sub in range(rows // w):
            q_u = q_ref[0, 0, sub * w:(sub + 1) * w, ls]
            if sub == 0:
                k_u = jnp.concatenate([kp_ref[0, 0, :, ls], k_ref[0, 0, 0:w, ls]], axis=0)
                v_u = jnp.concatenate([vp_ref[0, 0, :, ls], v_ref[0, 0, 0:w, ls]], axis=0)
                msk = band_first
            else:
                k_u = k_ref[0, 0, (sub - 1) * w:(sub + 1) * w, ls]
                v_u = v_ref[0, 0, (sub - 1) * w:(sub + 1) * w, ls]
                msk = band
            zero = jnp.zeros_like(q_u)
            res = []
            for hh in range(2):
                q_h = jnp.where(qk_first, q_u, zero) if hh == 0 else jnp.where(qk_first, zero, q_u)
                s = jnp.where(msk, _dot_nt(q_h, k_u), NEG_INF)
                m = jnp.max(s, axis=-1, keepdims=True)
                p = jnp.exp(s - m)
                l = jnp.sum(p, axis=-1, keepdims=True)
                o = _dot(p.astype(BF16), v_u) / l
                res.append((o, m + jnp.log(l)))
            o_ref[0, 0, sub * w:(sub + 1) * w, ls] = jnp.where(v_first, res[0][0], res[1][0]).astype(o_ref.dtype)
            lse_ref[0, 0, sub * w:(sub + 1) * w, ls] = jnp.where(
                v_first, jnp.broadcast_to(res[0][1], (w, LANES)), jnp.broadcast_to(res[1][1], (w, LANES)))


def _dil_attn(qkv, *, batch, seq, group, d):
    nb, t, _ = qkv.shape
    n_pairs = DIL_HEADS // 2
    rows = DIL_TOKENS // d
    view = qkv.reshape(nb, batch, seq // d, d * LANES)
    sub_per_blk = rows // DIL_WINDOW_KEYS
    qb, kb, vb = group * n_pairs, (3 + group) * n_pairs, (6 + group) * n_pairs
    cur = lambda base: pl.BlockSpec((1, 1, rows, d * LANES), lambda b, hp, i: (base + hp, b, i, 0))
    prev = lambda base: pl.BlockSpec(
        (1, 1, DIL_WINDOW_KEYS, d * LANES),
        lambda b, hp, i: (base + hp, b, jnp.maximum(i * sub_per_blk - 1, 0), 0))
    out_spec = pl.BlockSpec((1, 1, rows, d * LANES), lambda b, hp, i: (hp, b, i, 0))
    o, lse = pl.pallas_call(
        functools.partial(_dil_kernel, d=d, rows=rows),
        grid=(batch, n_pairs, seq // DIL_TOKENS),
        in_specs=[cur(qb), cur(kb), prev(kb), cur(vb), prev(vb)],
        out_specs=[out_spec, out_spec],
        out_shape=[
            jax.ShapeDtypeStruct((n_pairs, batch, seq // d, d * LANES), BF16),
            jax.ShapeDtypeStruct((n_pairs, batch, seq // d, d * LANES), F32),
        ],
        compiler_params=_params("parallel", "parallel", "arbitrary"),
        name="dil_attn_d%d" % d,
    )(view, view, view, view, view)
    return o.reshape(n_pairs, t, LANES), lse.reshape(n_pairs, t, LANES)


def _out_proj_kernel(h_ref, o_ref, w_ref, out_ref):
    nb = o_ref.shape[0]
    a = jnp.concatenate([o_ref[c] for c in range(nb)], axis=-1)
    out_ref[...] = h_ref[...] + _dot(a, w_ref[...])


def _dil_out_proj_kernel(h_ref, o0_ref, o1_ref, o2_ref, l0_ref, l1_ref, l2_ref, w_ref, out_ref):
    nb = o0_ref.shape[0]
    cols = []
    for c in range(nb):
        l0, l1, l2 = l0_ref[c], l1_ref[c], l2_ref[c]
        m = jnp.maximum(jnp.maximum(l0, l1), l2)
        e0, e1, e2 = jnp.exp(l0 - m), jnp.exp(l1 - m), jnp.exp(l2 - m)
        num = e0 * o0_ref[c].astype(F32) + e1 * o1_ref[c].astype(F32) + e2 * o2_ref[c].astype(F32)
        cols.append((num / (e0 + e1 + e2)).astype(BF16))
    out_ref[...] = h_ref[...] + _dot(jnp.concatenate(cols, axis=-1), w_ref[...])


def _out_proj(h, outs, lses, wo, *, tm=512):
    t, d = h.shape
    nb = outs[0].shape[0]
    row = pl.BlockSpec((tm, d), lambda i: (i, 0))
    blk = pl.BlockSpec((nb, tm, LANES), lambda i: (0, i, 0))
    kern = _dil_out_proj_kernel if lses else _out_proj_kernel
    n_blk = len(outs) + len(lses)
    return pl.pallas_call(
        kern,
        grid=(t // tm,),
        in_specs=[row] + [blk] * n_blk + [pl.BlockSpec(wo.shape, lambda i: (0, 0))],
        out_specs=row,
        out_shape=jax.ShapeDtypeStruct((t, d), F32),
        compiler_params=_params("parallel"),
        name="dil_out_proj" if lses else "out_proj",
    )(h, *outs, *lses, wo)


def _mlp_kernel(h_ref, g_ref, wu_ref, wd_ref, fg_ref, out_ref, xn_ref, acc_ref, *, final_norm):
    f = pl.program_id(1)

    @pl.when(f == 0)
    def _():
        xn_ref[...] = _rms(h_ref[...], g_ref[...]).astype(BF16)
        acc_ref[...] = h_ref[...]

    u = jnp.maximum(_dot(xn_ref[...], wu_ref[...]), 0.0)
    acc_ref[...] += _dot((u * u).astype(BF16), wd_ref[...])

    @pl.when(f == pl.num_programs(1) - 1)
    def _():
        y = acc_ref[...]
        out_ref[...] = _rms(y, fg_ref[...]) if final_norm else y


def _mlp(h, gain, w_up, w_down, final_gain, *, final_norm, tm=1024, tf=1024):
    t, d = h.shape
    dff = w_up.shape[1]
    return pl.pallas_call(
        functools.partial(_mlp_kernel, final_norm=final_norm),
        grid=(t // tm, dff // tf),
        in_specs=[
            pl.BlockSpec((tm, d), lambda i, f: (i, 0)),
            pl.BlockSpec((1, d), lambda i, f: (0, 0)),
            pl.BlockSpec((d, tf), lambda i, f: (0, f)),
            pl.BlockSpec((tf, d), lambda i, f: (f, 0)),
            pl.BlockSpec((1, d), lambda i, f: (0, 0)),
        ],
        out_specs=pl.BlockSpec((tm, d), lambda i, f: (i, 0)),
        out_shape=jax.ShapeDtypeStruct((t, d), F32),
        scratch_shapes=[pltpu.VMEM((tm, d), BF16), pltpu.VMEM((tm, d), F32)],
        compiler_params=_params("parallel", "arbitrary"),
        name="mlp_final" if final_norm else "mlp",
    )(h, gain.reshape(1, d), w_up, w_down, final_gain.reshape(1, d))


def _rope_halves(seq, dim):
    inv = 1.0 / (ROPE_THETA ** (jnp.arange(0, dim, 2, dtype=F32) / dim))
    ang = jnp.arange(seq, dtype=F32)[:, None] * inv[None, :]
    return jnp.cos(ang), jnp.sin(ang)


def _mla_tables(seq):
    cos, sin = _rope_halves(seq, MLA_ROPE)
    one = lambda n: jnp.ones((seq, n), F32)
    zero = lambda n: jnp.zeros((seq, n), F32)
    cos_t = jnp.concatenate([one(32), cos, one(16), one(32), cos, one(16)], axis=1)
    sin_t = jnp.concatenate([zero(32), -sin, zero(16), zero(32), sin, zero(16)], axis=1)
    return cos_t, sin_t


def _dil_tables(seq):
    cos, sin = _rope_halves(seq, DIL_DIM)
    return jnp.concatenate([cos] * 4, axis=1), jnp.concatenate([-sin, -sin, sin, sin], axis=1)


def _mla_weights(wq_a, wq_b, wkv_a, wkv_b):
    d = wq_a.shape[0]
    z = lambda r, n: jnp.zeros((r, n), F32)
    kpe = wkv_a[:, MLA_KV_RANK:]
    w_a = jnp.concatenate(
        [wq_a, wkv_a[:, :MLA_KV_RANK], z(d, 32), kpe[:, :16], z(d, 48), kpe[:, 16:], z(d, 16)], axis=1)
    scale = (MLA_NOPE + MLA_ROPE) ** -0.5 * LOG2E
    q3 = wq_b.reshape(MLA_Q_RANK, MLA_HEADS, MLA_NOPE + MLA_ROPE) * scale
    zq = jnp.zeros((MLA_Q_RANK, MLA_HEADS, 16), F32)
    wq = jnp.concatenate(
        [q3[..., :32], q3[..., 64:80], zq, q3[..., 32:64], q3[..., 80:96], zq], axis=-1)
    kv3 = wkv_b.reshape(MLA_KV_RANK, MLA_HEADS, MLA_NOPE + MLA_V)
    zk = jnp.zeros((MLA_KV_RANK, MLA_HEADS, 32), F32)
    wk = jnp.concatenate([kv3[..., :32], zk, kv3[..., 32:64], zk], axis=-1)
    wvt = kv3[..., MLA_NOPE:].reshape(MLA_KV_RANK, -1).T
    return (w_a.astype(BF16), wq.reshape(MLA_Q_RANK, -1).astype(BF16),
            wk.reshape(MLA_KV_RANK, -1).astype(BF16), wvt.astype(BF16))


def _dil_weights(w_qkv):
    d = w_qkv.shape[0]
    n_qk = 2 * len(DIL_PATTERNS) * DIL_HEADS * DIL_DIM
    qk = w_qkv[:, :n_qk].reshape(d, 2, n_qk // (2 * LANES), 2, 2, 32)
    qk = qk * jnp.array([DIL_DIM ** -0.5, 1.0], F32).reshape(1, 2, 1, 1, 1, 1)
    qk = qk.transpose(0, 1, 2, 4, 3, 5).reshape(d, n_qk)
    return jnp.concatenate([qk, w_qkv[:, n_qk:]], axis=1).astype(BF16)


def _mla_layer(h, attn_norm, wq_a, q_norm, wq_b, wkv_a, kv_norm, wkv_b, wo, tables, *, batch, seq):
    w_a, wq, wk, wvt = _mla_weights(wq_a, wq_b, wkv_a, wkv_b)
    cq, ckv, kpe = _mla_a(h, attn_norm, w_a, q_norm, kv_norm, *tables, seq=seq)
    qk, vt = _mla_b(cq, ckv, kpe, wq, wk, wvt, *tables, seq=seq)
    o = _flash(qk, vt, None, batch=batch, seq=seq, n_pairs=MLA_HEADS // 2,
               q_blk=0, k_blk=MLA_HEADS, sep=True)
    return _out_proj(h, [o], [], wo.astype(BF16))


def _fox_layer(h, attn_norm, w_qkv, w_f, b_f, wo, *, batch, seq):
    n = FOX_HEADS * FOX_DIM
    d = w_qkv.shape[0]
    wqk = jnp.concatenate([w_qkv[:, :n] * (FOX_DIM ** -0.5 * LOG2E), w_qkv[:, n:2 * n]], axis=1)
    wvt = w_qkv[:, 2 * n:].T
    wf = jnp.concatenate([w_f, jnp.zeros((d, LANES - FOX_HEADS), F32)], axis=1)
    bf = jnp.concatenate([b_f.astype(F32), jnp.zeros((LANES - FOX_HEADS,), F32)]).reshape(1, LANES)
    qk, vt, kd = _fox_proj(h, attn_norm, wqk.astype(BF16), wvt.astype(BF16), wf.astype(BF16), bf,
                           batch=batch, seq=seq)
    n_pairs = FOX_HEADS // 2
    o = _flash(qk, vt, kd, batch=batch, seq=seq, n_pairs=n_pairs, q_blk=0, k_blk=n_pairs, sep=False)
    return _out_proj(h, [o], [], wo.astype(BF16))


def _dil_layer(h, attn_norm, w_qkv, wo, tables, *, batch, seq):
    w = _dil_weights(w_qkv)
    n_rope_tiles = 2 * len(DIL_PATTERNS) * DIL_HEADS * DIL_DIM // 1024
    qkv = _norm_proj(h, attn_norm, w, *tables, seq=seq, n_rope_tiles=n_rope_tiles)
    outs, lses = [], []
    for g, (_, d) in enumerate(DIL_PATTERNS):
        o, lse = _dil_attn(qkv, batch=batch, seq=seq, group=g, d=d)
        outs.append(o)
        lses.append(lse)
    return _out_proj(h, outs, lses, wo.astype(BF16))


def kernel(x, l0_attn_norm, l0_mla_wq_a, l0_mla_q_norm, l0_mla_wq_b, l0_mla_wkv_a, l0_mla_kv_norm, l0_mla_wkv_b, l0_mla_wo, l0_mlp_norm, l0_w_up, l0_w_down, l1_attn_norm, l1_fox_w_qkv, l1_fox_w_f, l1_fox_b_f, l1_fox_wo, l1_mlp_norm, l1_w_up, l1_w_down, l2_attn_norm, l2_dil_w_qkv, l2_dil_wo, l2_mlp_norm, l2_w_up, l2_w_down, l3_attn_norm, l3_mla_wq_a, l3_mla_q_norm, l3_mla_wq_b, l3_mla_wkv_a, l3_mla_kv_norm, l3_mla_wkv_b, l3_mla_wo, l3_mlp_norm, l3_w_up, l3_w_down, final_norm):
    batch, seq, d = x.shape
    assert seq % DIL_TOKENS == 0 and (batch * seq) % 1024 == 0
    for window, dil in DIL_PATTERNS:
        assert window // dil == DIL_WINDOW_KEYS
    mla_t = _mla_tables(seq)
    dil_t = _dil_tables(seq)
    kw = dict(batch=batch, seq=seq)
    h = x.reshape(batch * seq, d)

    h = _mla_layer(h, l0_attn_norm, l0_mla_wq_a, l0_mla_q_norm, l0_mla_wq_b, l0_mla_wkv_a,
                   l0_mla_kv_norm, l0_mla_wkv_b, l0_mla_wo, mla_t, **kw)
    h = _mlp(h, l0_mlp_norm, l0_w_up.astype(BF16), l0_w_down.astype(BF16), final_norm, final_norm=False)

    h = _fox_layer(h, l1_attn_norm, l1_fox_w_qkv, l1_fox_w_f, l1_fox_b_f, l1_fox_wo, **kw)
    h = _mlp(h, l1_mlp_norm, l1_w_up.astype(BF16), l1_w_down.astype(BF16), final_norm, final_norm=False)

    h = _dil_layer(h, l2_attn_norm, l2_dil_w_qkv, l2_dil_wo, dil_t, **kw)
    h = _mlp(h, l2_mlp_norm, l2_w_up.astype(BF16), l2_w_down.astype(BF16), final_norm, final_norm=False)

    h = _mla_layer(h, l3_attn_norm, l3_mla_wq_a, l3_mla_q_norm, l3_mla_wq_b, l3_mla_wkv_a,
                   l3_mla_kv_norm, l3_mla_wkv_b, l3_mla_wo, mla_t, **kw)
    h = _mlp(h, l3_mlp_norm, l3_w_up.astype(BF16), l3_w_down.astype(BF16), final_norm, final_norm=True)
    return h.reshape(batch, seq, d)
```

```python
import functools
import math

import numpy as np
import jax
import jax.numpy as jnp
from jax import lax
from jax.experimental import pallas as pl
from jax.experimental.pallas import tpu as pltpu

F32 = jnp.float32
BF16 = jnp.bfloat16

LANES = 128
NORM_EPS = 1e-6
NEG_INF = -1e30
ROPE_THETA = 10000.0
LOG2E = math.log2(math.e)

MLA_HEADS = 16
MLA_Q_RANK = 384
MLA_KV_RANK = 256
MLA_NOPE = 64
MLA_ROPE = 32
MLA_V = 64
FOX_HEADS = 16
FOX_DIM = 64
DIL_PATTERNS = ((128, 1), (512, 4), (2048, 16))
DIL_HEADS = 16
DIL_DIM = 64
DIL_WINDOW_KEYS = 128
DIL_TOKENS = 2048
GATE_PARTS = 3

VMEM_LIMIT = 56 * 2**20


def _params(*sem):
    return pltpu.CompilerParams(dimension_semantics=sem, vmem_limit_bytes=VMEM_LIMIT)


def _rms(x, g):
    ms = jnp.mean(x * x, axis=-1, keepdims=True)
    return x * lax.rsqrt(ms + NORM_EPS) * g


def _dot(a, b):
    return jnp.dot(a, b, preferred_element_type=F32)


def _dot_nt(a, b):
    return lax.dot_general(a, b, (((1,), (1,)), ((), ())), preferred_element_type=F32)


def _rope(blk, cos, sin):
    return blk * cos + pltpu.roll(blk, 64, axis=1) * sin


def _colmax(st):
    n, c = st.shape
    return jnp.max(jnp.max(st.reshape(8, n // 8, c), axis=0), axis=0, keepdims=True)


def _split3(x):
    hi = x.astype(BF16)
    r1 = x - hi.astype(F32)
    mid = r1.astype(BF16)
    lo = (r1 - mid.astype(F32)).astype(BF16)
    return hi, mid, lo


def _dil_proj_kernel(x_ref, g_ref, w_ref, cos_ref, sin_ref, o_ref, xs_ref, xn_ref, *, d, tm):
    j = pl.program_id(2)
    rpr = tm // d
    nc = x_ref.shape[1] // LANES

    @pl.when(j == 0)
    def _():
        xn = _rms(x_ref[...], g_ref[...])
        if d == 1:
            xn_ref[...] = xn.astype(BF16)
        else:
            for c in range(nc):
                xs_ref[c] = xn[:, c * LANES:(c + 1) * LANES]
            for c in range(nc):
                for r in range(d):
                    xn_ref[r * rpr:(r + 1) * rpr, c * LANES:(c + 1) * LANES] = (
                        xs_ref[c, pl.ds(r, rpr, stride=d), :].astype(BF16))

    acc = _dot(xn_ref[...], w_ref[...])

    def store(rope):
        for c in range(acc.shape[1] // LANES):
            for r in range(d):
                rs = slice(r * rpr, (r + 1) * rpr)
                blk = acc[rs, c * LANES:(c + 1) * LANES]
                if rope:
                    blk = _rope(blk, cos_ref[rs, :], sin_ref[rs, :])
                o_ref[c, 0, :, r * LANES:(r + 1) * LANES] = blk.astype(o_ref.dtype)

    pl.when(j < 2)(lambda: store(True))
    pl.when(j >= 2)(lambda: store(False))


def _dil_proj(h, gain, w, cos_t, sin_t, *, batch, seq, d, tm=1024):
    t, dm = h.shape
    ns = seq // tm
    tn = w.shape[1] // 3
    nb = tn // LANES
    return pl.pallas_call(
        functools.partial(_dil_proj_kernel, d=d, tm=tm),
        grid=(batch, ns, 3),
        in_specs=[
            pl.BlockSpec((tm, dm), lambda b, i, j: (b * ns + i, 0)),
            pl.BlockSpec((1, dm), lambda b, i, j: (0, 0)),
            pl.BlockSpec((dm, tn), lambda b, i, j: (0, j)),
            pl.BlockSpec((tm, LANES), lambda b, i, j: (i, 0)),
            pl.BlockSpec((tm, LANES), lambda b, i, j: (i, 0)),
        ],
        out_specs=pl.BlockSpec((nb, 1, tm // d, d * LANES), lambda b, i, j: (j, b, i, 0)),
        out_shape=jax.ShapeDtypeStruct((3 * nb, batch, seq // d, d * LANES), BF16),
        scratch_shapes=[pltpu.VMEM((dm // LANES, tm, LANES), F32), pltpu.VMEM((tm, dm), BF16)],
        compiler_params=_params("parallel", "parallel", "arbitrary"),
        name="dil_proj_d%d" % d,
    )(h, gain.reshape(1, dm), w, cos_t, sin_t)


def _mla_a_kernel(x_ref, g_ref, w_ref, qn_ref, kvn_ref, cos_ref, sin_ref, cq_ref, ckv_ref, kpe_ref):
    xn = _rms(x_ref[...], g_ref[...]).astype(BF16)
    y = _dot(xn, w_ref[...])
    cq_ref[...] = _rms(y[:, :MLA_Q_RANK], qn_ref[...]).astype(BF16)
    ckv_ref[...] = _rms(y[:, MLA_Q_RANK:MLA_Q_RANK + MLA_KV_RANK], kvn_ref[...]).astype(BF16)
    kpe_ref[...] = _rope(y[:, MLA_Q_RANK + MLA_KV_RANK:], cos_ref[...], sin_ref[...]).astype(BF16)


def _mla_a(h, gain, w_a, q_norm, kv_norm, cos_t, sin_t, *, seq, tm=1024):
    t, d = h.shape
    n = w_a.shape[1]
    ns = seq // tm
    row = lambda i: (i, 0)
    fixed = lambda i: (0, 0)
    tab = lambda i: (i % ns, 0)
    return pl.pallas_call(
        _mla_a_kernel,
        grid=(t // tm,),
        in_specs=[
            pl.BlockSpec((tm, d), row),
            pl.BlockSpec((1, d), fixed),
            pl.BlockSpec((d, n), fixed),
            pl.BlockSpec((1, MLA_Q_RANK), fixed),
            pl.BlockSpec((1, MLA_KV_RANK), fixed),
            pl.BlockSpec((tm, LANES), tab),
            pl.BlockSpec((tm, LANES), tab),
        ],
        out_specs=[
            pl.BlockSpec((tm, MLA_Q_RANK), row),
            pl.BlockSpec((tm, MLA_KV_RANK), row),
            pl.BlockSpec((tm, LANES), row),
        ],
        out_shape=[
            jax.ShapeDtypeStruct((t, MLA_Q_RANK), BF16),
            jax.ShapeDtypeStruct((t, MLA_KV_RANK), BF16),
            jax.ShapeDtypeStruct((t, LANES), BF16),
        ],
        compiler_params=_params("parallel"),
        name="mla_a",
    )(h, gain.reshape(1, d), w_a, q_norm.reshape(1, -1), kv_norm.reshape(1, -1), cos_t, sin_t)


def _mla_b_kernel(cq_ref, ckv_ref, kpe_ref, wq_ref, wk_ref, wvt_ref, cos_ref, sin_ref, qk_ref, vt_ref):
    q = _dot(cq_ref[...], wq_ref[...])
    for c in range(MLA_HEADS):
        blk = q[:, c * LANES:(c + 1) * LANES]
        qk_ref[c] = _rope(blk, cos_ref[...], sin_ref[...]).astype(BF16)
    k = _dot(ckv_ref[...], wk_ref[...])
    kpe = kpe_ref[...].astype(F32)
    for c in range(MLA_HEADS):
        qk_ref[MLA_HEADS + c] = (k[:, c * LANES:(c + 1) * LANES] + kpe).astype(BF16)
    vt = _dot_nt(wvt_ref[...], ckv_ref[...])
    for c in range(MLA_HEADS // 2):
        vt_ref[c] = vt[c * LANES:(c + 1) * LANES, :].astype(BF16)


def _mla_b(cq, ckv, kpe, wq, wk, wvt, cos_t, sin_t, *, seq, tm=512):
    t = cq.shape[0]
    ns = seq // tm
    row = lambda i: (i, 0)
    fixed = lambda i: (0, 0)
    tab = lambda i: (i % ns, 0)
    return pl.pallas_call(
        _mla_b_kernel,
        grid=(t // tm,),
        in_specs=[
            pl.BlockSpec((tm, MLA_Q_RANK), row),
            pl.BlockSpec((tm, MLA_KV_RANK), row),
            pl.BlockSpec((tm, LANES), row),
            pl.BlockSpec(wq.shape, fixed),
            pl.BlockSpec(wk.shape, fixed),
            pl.BlockSpec(wvt.shape, fixed),
            pl.BlockSpec((tm, LANES), tab),
            pl.BlockSpec((tm, LANES), tab),
        ],
        out_specs=[
            pl.BlockSpec((2 * MLA_HEADS, tm, LANES), lambda i: (0, i, 0)),
            pl.BlockSpec((MLA_HEADS // 2, LANES, tm), lambda i: (0, 0, i)),
        ],
        out_shape=[
            jax.ShapeDtypeStruct((2 * MLA_HEADS, t, LANES), BF16),
            jax.ShapeDtypeStruct((MLA_HEADS // 2, LANES, t), BF16),
        ],
        compiler_params=_params("parallel"),
        name="mla_b",
    )(cq, ckv, kpe, wq, wk, wvt, cos_t, sin_t)


def _fox_proj_kernel(x_ref, g_ref, wqk_ref, wvt_ref, wf_ref, bf_ref, tri_ref, scat_ref,
                     qk_ref, vt_ref, kd_ref, carry_ref, *, tm):
    @pl.when(pl.program_id(1) == 0)
    def _():
        carry_ref[...] = jnp.zeros_like(carry_ref)

    xn = _rms(x_ref[...], g_ref[...]).astype(BF16)
    qk = _dot(xn, wqk_ref[...])
    for c in range(qk_ref.shape[0]):
        qk_ref[c] = qk[:, c * LANES:(c + 1) * LANES].astype(BF16)
    vt = _dot_nt(wvt_ref[...], xn)
    for c in range(vt_ref.shape[0]):
        vt_ref[c] = vt[c * LANES:(c + 1) * LANES, :].astype(BF16)

    z = _dot(xn, wf_ref[...]) + bf_ref[...]
    logf = jnp.minimum(z, 0.0) - jnp.log(1.0 + jnp.exp(-jnp.abs(z)))
    tri = tri_ref[...]
    cum = carry_ref[...] + sum(_dot(tri, part) for part in _split3(logf))
    carry_ref[...] = cum[tm - 1:tm, :]
    kd = sum(_dot(part, scat_ref[p]) for p, part in enumerate(_split3(cum * (-LOG2E))))
    for c in range(kd_ref.shape[0]):
        kd_ref[c] = kd[:, c * LANES:(c + 1) * LANES].astype(BF16)


def _fox_proj(h, gain, wqk, wvt, wf, bf, *, batch, seq, tm=512):
    t, d = h.shape
    ns = seq // tm
    n_pairs = FOX_HEADS // 2
    tri = jnp.tril(jnp.ones((tm, tm), BF16))
    scat = np.zeros((GATE_PARTS, LANES, n_pairs * LANES), np.float32)
    for p in range(GATE_PARTS):
        for hd in range(FOX_HEADS):
            scat[p, hd, (hd // 2) * LANES + GATE_PARTS * (hd % 2) + p] = 1.0
    fixed2 = lambda b, i: (0, 0)
    rows = lambda b, i: (0, b * ns + i, 0)
    return pl.pallas_call(
        functools.partial(_fox_proj_kernel, tm=tm),
        grid=(batch, ns),
        in_specs=[
            pl.BlockSpec((tm, d), lambda b, i: (b * ns + i, 0)),
            pl.BlockSpec((1, d), fixed2),
            pl.BlockSpec(wqk.shape, fixed2),
            pl.BlockSpec(wvt.shape, fixed2),
            pl.BlockSpec(wf.shape, fixed2),
            pl.BlockSpec((1, LANES), fixed2),
            pl.BlockSpec((tm, tm), fixed2),
            pl.BlockSpec(scat.shape, lambda b, i: (0, 0, 0)),
        ],
        out_specs=[
            pl.BlockSpec((2 * n_pairs, tm, LANES), rows),
            pl.BlockSpec((n_pairs, LANES, tm), lambda b, i: (0, 0, b * ns + i)),
            pl.BlockSpec((n_pairs, tm, LANES), rows),
        ],
        out_shape=[
            jax.ShapeDtypeStruct((2 * n_pairs, t, LANES), BF16),
            jax.ShapeDtypeStruct((n_pairs, LANES, t), BF16),
            jax.ShapeDtypeStruct((n_pairs, t, LANES), BF16),
        ],
        scratch_shapes=[pltpu.VMEM((1, LANES), F32)],
        compiler_params=_params("arbitrary", "arbitrary"),
        name="fox_proj",
    )(h, gain.reshape(1, d), wqk, wvt, wf, bf, tri, jnp.asarray(scat, BF16))


def _flash_kernel(*refs, sep, has_bias, tq):
    if has_bias:
        q_ref, k_ref, vt_ref, kd_ref, o_ref, qs_ref, vta_ref, acc_ref, s_ref, mx_ref = refs
    else:
        q_ref, k_ref, vt_ref, o_ref, qs_ref, vta_ref, acc_ref, s_ref, mx_ref = refs
        kd_ref = None
    i = pl.program_id(2)
    half = LANES // 2

    @pl.when(i == 0)
    def _():
        vt = vt_ref[0]
        r = lax.broadcasted_iota(jnp.int32, vt.shape, 0)
        one = jnp.ones_like(vt)
        vta_ref[0] = jnp.where(r < half, vt, one)
        vta_ref[1] = jnp.where(r < half, one, vt)

    lane = lax.broadcasted_iota(jnp.int32, (tq, LANES), 1)
    for hh in range(2):
        if sep:
            q_h = q_ref[hh]
        else:
            q = q_ref[0]
            own = (lane < half) if hh == 0 else (lane >= half)
            q_h = jnp.where(own, q, jnp.zeros_like(q))
        qs_ref[hh, :, 0:LANES] = q_h
        if has_bias:
            sel = jnp.logical_and(lane >= GATE_PARTS * hh, lane < GATE_PARTS * (hh + 1))
            qs_ref[hh, :, LANES:2 * LANES] = jnp.where(sel, 1.0, 0.0).astype(BF16)
    key = lax.broadcasted_iota(jnp.int32, (tq, tq), 0)
    qry = lax.broadcasted_iota(jnp.int32, (tq, tq), 1)
    acc_ref[...] = jnp.zeros_like(acc_ref)

    def scores(kb, slot):
        start = pl.multiple_of(kb * tq, tq)
        for hh in range(2):
            k_blk = k_ref[hh if sep else 0, pl.ds(start, tq), :]
            if has_bias:
                k_blk = jnp.concatenate([k_blk, kd_ref[0, pl.ds(start, tq), :]], axis=1)
            st = _dot_nt(k_blk, qs_ref[hh])
            s_ref[slot, hh] = st
            mx_ref[slot, hh] = _colmax(st)

    def consume(kb, slot, m_old, diag):
        start = pl.multiple_of(kb * tq, tq)
        m_out = []
        for hh in range(2):
            st = s_ref[slot, hh]
            if diag:
                st = jnp.where(key <= qry, st, NEG_INF)
                mx = _colmax(st)
            else:
                mx = mx_ref[slot, hh]
            m_new = jnp.maximum(m_old[hh], mx)
            alpha = jnp.exp2(m_old[hh] - m_new)
            pt = jnp.exp2(st - m_new).astype(BF16)
            acc_ref[hh] = alpha * acc_ref[hh] + _dot(vta_ref[hh, :, pl.ds(start, tq)], pt)
            m_out.append(m_new)
        return tuple(m_out)

    def body(j, m_old):
        scores(2 * j + 1, 1)
        m_mid = consume(2 * j, 0, m_old, False)
        scores(2 * j + 2, 0)
        return consume(2 * j + 1, 1, m_mid, False)

    def write_out():
        acc_a, acc_b = acc_ref[0], acc_ref[1]
        r = lax.broadcasted_iota(jnp.int32, (LANES, tq), 0)
        o_t = jnp.where(r < half, acc_a / acc_a[half:half + 1, :], acc_b / acc_b[0:1, :])
        o_ref[0] = o_t.T.astype(o_ref.dtype)

    scores(0, 0)
    m0 = jnp.full((1, tq), NEG_INF, F32)
    m = lax.fori_loop(0, lax.shift_right_logical(i, 1), body, (m0, m0))

    @pl.when(i % 2 == 0)
    def _():
        consume(i, 0, m, True)
        write_out()

    @pl.when(i % 2 == 1)
    def _():
        scores(i, 1)
        consume(i, 1, consume(i - 1, 0, m, False), True)
        write_out()


def _flash(qk, vt, kd, *, batch, seq, n_pairs, q_blk, k_blk, sep, tq=512):
    t = qk.shape[1]
    nq = seq // tq
    w = 2 if sep else 1
    kc = 2 * LANES if kd is not None else LANES
    in_specs = [
        pl.BlockSpec((w, tq, LANES), lambda b, hp, i: (q_blk // w + hp, b * nq + i, 0)),
        pl.BlockSpec((w, seq, LANES), lambda b, hp, i: (k_blk // w + hp, b, 0)),
        pl.BlockSpec((1, LANES, seq), lambda b, hp, i: (hp, 0, b)),
    ]
    args = [qk, qk, vt]
    if kd is not None:
        in_specs.append(pl.BlockSpec((1, seq, LANES), lambda b, hp, i: (hp, b, 0)))
        args.append(kd)
    return pl.pallas_call(
        functools.partial(_flash_kernel, sep=sep, has_bias=kd is not None, tq=tq),
        grid=(batch, n_pairs, nq),
        in_specs=in_specs,
        out_specs=pl.BlockSpec((1, tq, LANES), lambda b, hp, i: (hp, b * nq + i, 0)),
        out_shape=jax.ShapeDtypeStruct((n_pairs, t, LANES), BF16),
        scratch_shapes=[
            pltpu.VMEM((2, tq, kc), BF16),
            pltpu.VMEM((2, LANES, seq), BF16),
            pltpu.VMEM((2, LANES, tq), F32),
            pltpu.VMEM((2, 2, tq, tq), F32),
            pltpu.VMEM((2, 2, 1, tq), F32),
        ],
        compiler_params=_params("arbitrary", "arbitrary", "arbitrary"),
        name="flash_sep" if sep else "flash_shared",
    )(*args)


def _dil_attn_kernel(*refs):
    n_in = 5 * len(DIL_PATTERNS)
    o_ref, og_ref, lg_ref = refs[n_in:]
    i = pl.program_id(2)
    w = DIL_WINDOW_KEYS
    lane = lax.broadcasted_iota(jnp.int32, (w, LANES), 1)
    qk_first = (lane % 64) < 32
    v_first = lane < 64
    qi = lax.broadcasted_iota(jnp.int32, (w, 2 * w), 0)
    kk = lax.broadcasted_iota(jnp.int32, (w, 2 * w), 1)
    band = jnp.logical_and(kk >= qi, kk <= qi + w)
    band_first = jnp.logical_and(band, kk >= jnp.where(i > 0, 0, w))
    for g, (_, d) in enumerate(DIL_PATTERNS):
        q_ref, k_ref, kp_ref, v_ref, vp_ref = refs[5 * g:5 * g + 5]
        for r in range(d):
            ls = slice(r * LANES, (r + 1) * LANES)
            for sub in range(DIL_TOKENS // d // w):
                q_u = q_ref[0, 0, sub * w:(sub + 1) * w, ls]
                if sub == 0:
                    k_u = jnp.concatenate([kp_ref[0, 0, :, ls], k_ref[0, 0, 0:w, ls]], axis=0)
                    v_u = jnp.concatenate([vp_ref[0, 0, :, ls], v_ref[0, 0, 0:w, ls]], axis=0)
                    msk = band_first
                else:
                    k_u = k_ref[0, 0, (sub - 1) * w:(sub + 1) * w, ls]
                    v_u = v_ref[0, 0, (sub - 1) * w:(sub + 1) * w, ls]
                    msk = band
                zero = jnp.zeros_like(q_u)
                res = []
                for hh in range(2):
                    q_h = jnp.where(qk_first, q_u, zero) if hh == 0 else jnp.where(qk_first, zero, q_u)
                    s = jnp.where(msk, _dot_nt(q_h, k_u), NEG_INF)
                    m = jnp.max(s, axis=-1, keepdims=True)
                    p = jnp.exp(s - m)
                    l = jnp.sum(p, axis=-1, keepdims=True)
                    o = _dot(p.astype(BF16), v_u) / l
                    res.append((o, m + jnp.log(l)))
                tok = pl.ds(sub * w * d + r, w, stride=d) if d > 1 else pl.ds(sub * w, w)
                og_ref[g, tok, :] = jnp.where(v_first, res[0][0], res[1][0])
                lg_ref[g, tok, :] = jnp.where(
                    v_first, jnp.broadcast_to(res[0][1], (w, LANES)), jnp.broadcast_to(res[1][1], (w, LANES)))
    lse = [lg_ref[g] for g in range(len(DIL_PATTERNS))]
    top = functools.reduce(jnp.maximum, lse)
    e = [jnp.exp(x - top) for x in lse]
    num = sum(e[g] * og_ref[g] for g in range(len(DIL_PATTERNS)))
    o_ref[0] = (num / sum(e)).astype(o_ref.dtype)


def _dil_attn(groups, *, batch, seq):
    n_pairs = DIL_HEADS // 2
    n_tok_blk = seq // DIL_TOKENS
    in_specs, args = [], []
    for arr, (_, d) in zip(groups, DIL_PATTERNS):
        rows = DIL_TOKENS // d
        sub_per_blk = rows // DIL_WINDOW_KEYS

        def cur(base, rows=rows, d=d):
            return pl.BlockSpec((1, 1, rows, d * LANES), lambda b, hp, i: (base + hp, b, i, 0))

        def prev(base, spb=sub_per_blk, d=d):
            return pl.BlockSpec((1, 1, DIL_WINDOW_KEYS, d * LANES),
                                lambda b, hp, i: (base + hp, b, jnp.maximum(i * spb - 1, 0), 0))

        in_specs += [cur(0), cur(n_pairs), prev(n_pairs), cur(2 * n_pairs), prev(2 * n_pairs)]
        args += [arr] * 5
    n_g = len(DIL_PATTERNS)
    return pl.pallas_call(
        _dil_attn_kernel,
        grid=(batch, n_pairs, n_tok_blk),
        in_specs=in_specs,
        out_specs=pl.BlockSpec((1, DIL_TOKENS, LANES), lambda b, hp, i: (hp, b * n_tok_blk + i, 0)),
        out_shape=jax.ShapeDtypeStruct((n_pairs, batch * seq, LANES), BF16),
        scratch_shapes=[pltpu.VMEM((n_g, DIL_TOKENS, LANES), F32), pltpu.VMEM((n_g, DIL_TOKENS, LANES), F32)],
        compiler_params=_params("parallel", "parallel", "arbitrary"),
        name="dil_attn",
    )(*args)


def _out_proj_kernel(h_ref, o_ref, w_ref, out_ref):
    nb = o_ref.shape[0]
    a = jnp.concatenate([o_ref[c] for c in range(nb)], axis=-1)
    out_ref[...] = h_ref[...] + _dot(a, w_ref[...])


def _out_proj(h, o, wo, *, tm=512):
    t, d = h.shape
    nb = o.shape[0]
    row = pl.BlockSpec((tm, d), lambda i: (i, 0))
    return pl.pallas_call(
        _out_proj_kernel,
        grid=(t // tm,),
        in_specs=[row, pl.BlockSpec((nb, tm, LANES), lambda i: (0, i, 0)),
                  pl.BlockSpec(wo.shape, lambda i: (0, 0))],
        out_specs=row,
        out_shape=jax.ShapeDtypeStruct((t, d), F32),
        compiler_params=_params("parallel"),
        name="out_proj",
    )(h, o, wo)


def _mlp_kernel(h_ref, g_ref, wu_ref, wd_ref, fg_ref, out_ref, xn_ref, acc_ref, *, final_norm):
    f = pl.program_id(1)

    @pl.when(f == 0)
    def _():
        xn_ref[...] = _rms(h_ref[...], g_ref[...]).astype(BF16)
        acc_ref[...] = h_ref[...]

    u = jnp.maximum(_dot(xn_ref[...], wu_ref[...]), 0.0)
    acc_ref[...] += _dot((u * u).astype(BF16), wd_ref[...])

    @pl.when(f == pl.num_programs(1) - 1)
    def _():
        y = acc_ref[...]
        out_ref[...] = _rms(y, fg_ref[...]) if final_norm else y


def _mlp(h, gain, w_up, w_down, final_gain, *, final_norm, tm=1024, tf=1024):
    t, d = h.shape
    dff = w_up.shape[1]
    return pl.pallas_call(
        functools.partial(_mlp_kernel, final_norm=final_norm),
        grid=(t // tm, dff // tf),
        in_specs=[
            pl.BlockSpec((tm, d), lambda i, f: (i, 0)),
            pl.BlockSpec((1, d), lambda i, f: (0, 0)),
            pl.BlockSpec((d, tf), lambda i, f: (0, f)),
            pl.BlockSpec((tf, d), lambda i, f: (f, 0)),
            pl.BlockSpec((1, d), lambda i, f: (0, 0)),
        ],
        out_specs=pl.BlockSpec((tm, d), lambda i, f: (i, 0)),
        out_shape=jax.ShapeDtypeStruct((t, d), F32),
        scratch_shapes=[pltpu.VMEM((tm, d), BF16), pltpu.VMEM((tm, d), F32)],
        compiler_params=_params("parallel", "arbitrary"),
        name="mlp_final" if final_norm else "mlp",
    )(h, gain.reshape(1, d), w_up, w_down, final_gain.reshape(1, d))


def _rope_halves(seq, dim):
    inv = 1.0 / (ROPE_THETA ** (jnp.arange(0, dim, 2, dtype=F32) / dim))
    ang = jnp.arange(seq, dtype=F32)[:, None] * inv[None, :]
    return jnp.cos(ang), jnp.sin(ang)


def _mla_tables(seq):
    cos, sin = _rope_halves(seq, MLA_ROPE)
    one = lambda n: jnp.ones((seq, n), F32)
    zero = lambda n: jnp.zeros((seq, n), F32)
    cos_t = jnp.concatenate([one(32), cos, one(16), one(32), cos, one(16)], axis=1)
    sin_t = jnp.concatenate([zero(32), -sin, zero(16), zero(32), sin, zero(16)], axis=1)
    return cos_t, sin_t


def _dil_tables(seq, d, tm):
    cos, sin = _rope_halves(seq, DIL_DIM)
    order = lambda x: x.reshape(seq // tm, tm // d, d, LANES).transpose(0, 2, 1, 3).reshape(seq, LANES)
    return (order(jnp.concatenate([cos] * 4, axis=1)),
            order(jnp.concatenate([-sin, -sin, sin, sin], axis=1)))


def _mla_weights(wq_a, wq_b, wkv_a, wkv_b):
    d = wq_a.shape[0]
    z = lambda r, n: jnp.zeros((r, n), F32)
    kpe = wkv_a[:, MLA_KV_RANK:]
    w_a = jnp.concatenate(
        [wq_a, wkv_a[:, :MLA_KV_RANK], z(d, 32), kpe[:, :16], z(d, 48), kpe[:, 16:], z(d, 16)], axis=1)
    scale = (MLA_NOPE + MLA_ROPE) ** -0.5 * LOG2E
    q3 = wq_b.reshape(MLA_Q_RANK, MLA_HEADS, MLA_NOPE + MLA_ROPE) * scale
    zq = jnp.zeros((MLA_Q_RANK, MLA_HEADS, 16), F32)
    wq = jnp.concatenate(
        [q3[..., :32], q3[..., 64:80], zq, q3[..., 32:64], q3[..., 80:96], zq], axis=-1)
    kv3 = wkv_b.reshape(MLA_KV_RANK, MLA_HEADS, MLA_NOPE + MLA_V)
    zk = jnp.zeros((MLA_KV_RANK, MLA_HEADS, 32), F32)
    wk = jnp.concatenate([kv3[..., :32], zk, kv3[..., 32:64], zk], axis=-1)
    wvt = kv3[..., MLA_NOPE:].reshape(MLA_KV_RANK, -1).T
    return (w_a.astype(BF16), wq.reshape(MLA_Q_RANK, -1).astype(BF16),
            wk.reshape(MLA_KV_RANK, -1).astype(BF16), wvt.astype(BF16))


def _dil_weights(w_qkv):
    d = w_qkv.shape[0]
    n_g = len(DIL_PATTERNS)
    n = DIL_HEADS * DIL_DIM
    w = w_qkv.reshape(d, 3, n_g, n)
    qk = w[:, :2].reshape(d, 2, n_g, n // LANES, 2, 2, 32)
    qk = qk * jnp.array([DIL_DIM ** -0.5, 1.0], F32).reshape(1, 2, 1, 1, 1, 1, 1)
    qk = qk.transpose(0, 1, 2, 3, 5, 4, 6).reshape(d, 2, n_g, n)
    w = jnp.concatenate([qk, w[:, 2:]], axis=1).astype(BF16)
    return [w[:, :, g].reshape(d, 3 * n) for g in range(n_g)]


def _mla_layer(h, attn_norm, wq_a, q_norm, wq_b, wkv_a, kv_norm, wkv_b, wo, tables, *, batch, seq):
    w_a, wq, wk, wvt = _mla_weights(wq_a, wq_b, wkv_a, wkv_b)
    cq, ckv, kpe = _mla_a(h, attn_norm, w_a, q_norm, kv_norm, *tables, seq=seq)
    qk, vt = _mla_b(cq, ckv, kpe, wq, wk, wvt, *tables, seq=seq)
    o = _flash(qk, vt, None, batch=batch, seq=seq, n_pairs=MLA_HEADS // 2,
               q_blk=0, k_blk=MLA_HEADS, sep=True)
    return _out_proj(h, o, wo.astype(BF16))


def _fox_layer(h, attn_norm, w_qkv, w_f, b_f, wo, *, batch, seq):
    n = FOX_HEADS * FOX_DIM
    d = w_qkv.shape[0]
    wqk = jnp.concatenate([w_qkv[:, :n] * (FOX_DIM ** -0.5 * LOG2E), w_qkv[:, n:2 * n]], axis=1)
    wvt = w_qkv[:, 2 * n:].T
    wf = jnp.concatenate([w_f, jnp.zeros((d, LANES - FOX_HEADS), F32)], axis=1)
    bf = jnp.concatenate([b_f.astype(F32), jnp.zeros((LANES - FOX_HEADS,), F32)]).reshape(1, LANES)
    qk, vt, kd = _fox_proj(h, attn_norm, wqk.astype(BF16), wvt.astype(BF16), wf.astype(BF16), bf,
                           batch=batch, seq=seq)
    n_pairs = FOX_HEADS // 2
    o = _flash(qk, vt, kd, batch=batch, seq=seq, n_pairs=n_pairs, q_blk=0, k_blk=n_pairs, sep=False)
    return _out_proj(h, o, wo.astype(BF16))


def _dil_layer(h, attn_norm, w_qkv, wo, *, batch, seq, tm=1024):
    groups = []
    for w_g, (_, d) in zip(_dil_weights(w_qkv), DIL_PATTERNS):
        groups.append(_dil_proj(h, attn_norm, w_g, *_dil_tables(seq, d, tm),
                                batch=batch, seq=seq, d=d, tm=tm))
    o = _dil_attn(groups, batch=batch, seq=seq)
    return _out_proj(h, o, wo.astype(BF16))


def kernel(x, l0_attn_norm, l0_mla_wq_a, l0_mla_q_norm, l0_mla_wq_b, l0_mla_wkv_a, l0_mla_kv_norm, l0_mla_wkv_b, l0_mla_wo, l0_mlp_norm, l0_w_up, l0_w_down, l1_attn_norm, l1_fox_w_qkv, l1_fox_w_f, l1_fox_b_f, l1_fox_wo, l1_mlp_norm, l1_w_up, l1_w_down, l2_attn_norm, l2_dil_w_qkv, l2_dil_wo, l2_mlp_norm, l2_w_up, l2_w_down, l3_attn_norm, l3_mla_wq_a, l3_mla_q_norm, l3_mla_wq_b, l3_mla_wkv_a, l3_mla_kv_norm, l3_mla_wkv_b, l3_mla_wo, l3_mlp_norm, l3_w_up, l3_w_down, final_norm):
    batch, seq, d = x.shape
    assert seq % DIL_TOKENS == 0 and (batch * seq) % 1024 == 0
    for window, dil in DIL_PATTERNS:
        assert window // dil == DIL_WINDOW_KEYS
    mla_t = _mla_tables(seq)
    kw = dict(batch=batch, seq=seq)
    h = x.reshape(batch * seq, d)

    h = _mla_layer(h, l0_attn_norm, l0_mla_wq_a, l0_mla_q_norm, l0_mla_wq_b, l0_mla_wkv_a,
                   l0_mla_kv_norm, l0_mla_wkv_b, l0_mla_wo, mla_t, **kw)
    h = _mlp(h, l0_mlp_norm, l0_w_up.astype(BF16), l0_w_down.astype(BF16), final_norm, final_norm=False)

    h = _fox_layer(h, l1_attn_norm, l1_fox_w_qkv, l1_fox_w_f, l1_fox_b_f, l1_fox_wo, **kw)
    h = _mlp(h, l1_mlp_norm, l1_w_up.astype(BF16), l1_w_down.astype(BF16), final_norm, final_norm=False)

    h = _dil_layer(h, l2_attn_norm, l2_dil_w_qkv, l2_dil_wo, **kw)
    h = _mlp(h, l2_mlp_norm, l2_w_up.astype(BF16), l2_w_down.astype(BF16), final_norm, final_norm=False)

    h = _mla_layer(h, l3_attn_norm, l3_mla_wq_a, l3_mla_q_norm, l3_mla_wq_b, l3_mla_wkv_a,
                   l3_mla_kv_norm, l3_mla_wkv_b, l3_mla_wo, mla_t, **kw)
    h = _mlp(h, l3_mlp_norm, l3_w_up.astype(BF16), l3_w_down.astype(BF16), final_norm, final_norm=True)
    return h.reshape(batch, seq, d)
```

```python
import functools
import math

import numpy as np
import jax
import jax.numpy as jnp
from jax import lax
from jax.experimental import pallas as pl
from jax.experimental.pallas import tpu as pltpu

F32 = jnp.float32
BF16 = jnp.bfloat16

LANES = 128
NORM_EPS = 1e-6
NEG_INF = -1e30
ROPE_THETA = 10000.0
LOG2E = math.log2(math.e)

MLA_HEADS = 16
MLA_Q_RANK = 384
MLA_KV_RANK = 256
MLA_NOPE = 64
MLA_ROPE = 32
MLA_V = 64
FOX_HEADS = 16
FOX_DIM = 64
DIL_PATTERNS = ((128, 1), (512, 4), (2048, 16))
DIL_HEADS = 16
DIL_DIM = 64
DIL_WINDOW_KEYS = 128
DIL_TOKENS = 2048
GATE_PARTS = 3

VMEM_LIMIT = 56 * 2**20


def _params(*sem):
    return pltpu.CompilerParams(dimension_semantics=sem, vmem_limit_bytes=VMEM_LIMIT)


def _rms(x, g):
    ms = jnp.mean(x * x, axis=-1, keepdims=True)
    return x * lax.rsqrt(ms + NORM_EPS) * g


def _dot(a, b):
    return jnp.dot(a, b, preferred_element_type=F32)


def _dot_nt(a, b):
    return lax.dot_general(a, b, (((1,), (1,)), ((), ())), preferred_element_type=F32)


def _rope(blk, cos, sin):
    return blk * cos + pltpu.roll(blk, 64, axis=1) * sin


def _colmax(st):
    n, c = st.shape
    return jnp.max(jnp.max(st.reshape(8, n // 8, c), axis=0), axis=0, keepdims=True)


def _split3(x):
    hi = x.astype(BF16)
    r1 = x - hi.astype(F32)
    mid = r1.astype(BF16)
    lo = (r1 - mid.astype(F32)).astype(BF16)
    return hi, mid, lo


def _dil_proj_kernel(x_ref, g_ref, w_ref, cos_ref, sin_ref, o_ref, xs_ref, xn_ref, *, d, tm):
    rpr = tm // d
    nc = x_ref.shape[1] // LANES
    xn = _rms(x_ref[...], g_ref[...])
    if d == 1:
        xn_ref[...] = xn.astype(BF16)
    else:
        for c in range(nc):
            xs_ref[c] = xn[:, c * LANES:(c + 1) * LANES]
        for c in range(nc):
            for r in range(d):
                xn_ref[r * rpr:(r + 1) * rpr, c * LANES:(c + 1) * LANES] = (
                    xs_ref[c, pl.ds(r, rpr, stride=d), :].astype(BF16))

    nb = o_ref.shape[0] // 3
    for j in range(3):
        acc = _dot(xn_ref[...], w_ref[:, j * nb * LANES:(j + 1) * nb * LANES])
        for c in range(nb):
            for r in range(d):
                rs = slice(r * rpr, (r + 1) * rpr)
                blk = acc[rs, c * LANES:(c + 1) * LANES]
                if j < 2:
                    blk = _rope(blk, cos_ref[rs, :], sin_ref[rs, :])
                o_ref[j * nb + c, 0, :, r * LANES:(r + 1) * LANES] = blk.astype(o_ref.dtype)


def _dil_proj(h, gain, w, cos_t, sin_t, *, batch, seq, d, tm):
    t, dm = h.shape
    ns = seq // tm
    nb = w.shape[1] // LANES
    return pl.pallas_call(
        functools.partial(_dil_proj_kernel, d=d, tm=tm),
        grid=(batch, ns),
        in_specs=[
            pl.BlockSpec((tm, dm), lambda b, i: (b * ns + i, 0)),
            pl.BlockSpec((1, dm), lambda b, i: (0, 0)),
            pl.BlockSpec(w.shape, lambda b, i: (0, 0)),
            pl.BlockSpec((tm, LANES), lambda b, i: (i, 0)),
            pl.BlockSpec((tm, LANES), lambda b, i: (i, 0)),
        ],
        out_specs=pl.BlockSpec((nb, 1, tm // d, d * LANES), lambda b, i: (0, b, i, 0)),
        out_shape=jax.ShapeDtypeStruct((nb, batch, seq // d, d * LANES), BF16),
        scratch_shapes=[pltpu.VMEM((dm // LANES, tm, LANES), F32), pltpu.VMEM((tm, dm), BF16)],
        compiler_params=_params("parallel", "parallel"),
        name="dil_proj_d%d" % d,
    )(h, gain.reshape(1, dm), w, cos_t, sin_t)


def _mla_a_kernel(x_ref, g_ref, w_ref, qn_ref, kvn_ref, cos_ref, sin_ref, cq_ref, ckv_ref, kpe_ref):
    xn = _rms(x_ref[...], g_ref[...]).astype(BF16)
    y = _dot(xn, w_ref[...])
    cq_ref[...] = _rms(y[:, :MLA_Q_RANK], qn_ref[...]).astype(BF16)
    ckv_ref[...] = _rms(y[:, MLA_Q_RANK:MLA_Q_RANK + MLA_KV_RANK], kvn_ref[...]).astype(BF16)
    kpe_ref[...] = _rope(y[:, MLA_Q_RANK + MLA_KV_RANK:], cos_ref[...], sin_ref[...]).astype(BF16)


def _mla_a(h, gain, w_a, q_norm, kv_norm, cos_t, sin_t, *, seq, tm=1024):
    t, d = h.shape
    n = w_a.shape[1]
    ns = seq // tm
    row = lambda i: (i, 0)
    fixed = lambda i: (0, 0)
    tab = lambda i: (i % ns, 0)
    return pl.pallas_call(
        _mla_a_kernel,
        grid=(t // tm,),
        in_specs=[
            pl.BlockSpec((tm, d), row),
            pl.BlockSpec((1, d), fixed),
            pl.BlockSpec((d, n), fixed),
            pl.BlockSpec((1, MLA_Q_RANK), fixed),
            pl.BlockSpec((1, MLA_KV_RANK), fixed),
            pl.BlockSpec((tm, LANES), tab),
            pl.BlockSpec((tm, LANES), tab),
        ],
        out_specs=[
            pl.BlockSpec((tm, MLA_Q_RANK), row),
            pl.BlockSpec((tm, MLA_KV_RANK), row),
            pl.BlockSpec((tm, LANES), row),
        ],
        out_shape=[
            jax.ShapeDtypeStruct((t, MLA_Q_RANK), BF16),
            jax.ShapeDtypeStruct((t, MLA_KV_RANK), BF16),
            jax.ShapeDtypeStruct((t, LANES), BF16),
        ],
        compiler_params=_params("parallel"),
        name="mla_a",
    )(h, gain.reshape(1, d), w_a, q_norm.reshape(1, -1), kv_norm.reshape(1, -1), cos_t, sin_t)


def _mla_b_kernel(cq_ref, ckv_ref, kpe_ref, wq_ref, wk_ref, wvt_ref, cos_ref, sin_ref, qk_ref, vt_ref):
    q = _dot(cq_ref[...], wq_ref[...])
    for c in range(MLA_HEADS):
        blk = q[:, c * LANES:(c + 1) * LANES]
        qk_ref[c] = _rope(blk, cos_ref[...], sin_ref[...]).astype(BF16)
    k = _dot(ckv_ref[...], wk_ref[...])
    kpe = kpe_ref[...].astype(F32)
    for c in range(MLA_HEADS):
        qk_ref[MLA_HEADS + c] = (k[:, c * LANES:(c + 1) * LANES] + kpe).astype(BF16)
    vt = _dot_nt(wvt_ref[...], ckv_ref[...])
    for c in range(MLA_HEADS // 2):
        vt_ref[c] = vt[c * LANES:(c + 1) * LANES, :].astype(BF16)


def _mla_b(cq, ckv, kpe, wq, wk, wvt, cos_t, sin_t, *, seq, tm=512):
    t = cq.shape[0]
    ns = seq // tm
    row = lambda i: (i, 0)
    fixed = lambda i: (0, 0)
    tab = lambda i: (i % ns, 0)
    return pl.pallas_call(
        _mla_b_kernel,
        grid=(t // tm,),
        in_specs=[
            pl.BlockSpec((tm, MLA_Q_RANK), row),
            pl.BlockSpec((tm, MLA_KV_RANK), row),
            pl.BlockSpec((tm, LANES), row),
            pl.BlockSpec(wq.shape, fixed),
            pl.BlockSpec(wk.shape, fixed),
            pl.BlockSpec(wvt.shape, fixed),
            pl.BlockSpec((tm, LANES), tab),
            pl.BlockSpec((tm, LANES), tab),
        ],
        out_specs=[
            pl.BlockSpec((2 * MLA_HEADS, tm, LANES), lambda i: (0, i, 0)),
            pl.BlockSpec((MLA_HEADS // 2, LANES, tm), lambda i: (0, 0, i)),
        ],
        out_shape=[
            jax.ShapeDtypeStruct((2 * MLA_HEADS, t, LANES), BF16),
            jax.ShapeDtypeStruct((MLA_HEADS // 2, LANES, t), BF16),
        ],
        compiler_params=_params("parallel"),
        name="mla_b",
    )(cq, ckv, kpe, wq, wk, wvt, cos_t, sin_t)


def _fox_proj_kernel(x_ref, g_ref, wqk_ref, wvt_ref, wf_ref, bf_ref, tri_ref, scat_ref,
                     qk_ref, vt_ref, kd_ref, carry_ref, *, tm):
    @pl.when(pl.program_id(1) == 0)
    def _():
        carry_ref[...] = jnp.zeros_like(carry_ref)

    xn = _rms(x_ref[...], g_ref[...]).astype(BF16)
    qk = _dot(xn, wqk_ref[...])
    for c in range(qk_ref.shape[0]):
        qk_ref[c] = qk[:, c * LANES:(c + 1) * LANES].astype(BF16)
    vt = _dot_nt(wvt_ref[...], xn)
    for c in range(vt_ref.shape[0]):
        vt_ref[c] = vt[c * LANES:(c + 1) * LANES, :].astype(BF16)

    z = _dot(xn, wf_ref[...]) + bf_ref[...]
    logf = jnp.minimum(z, 0.0) - jnp.log(1.0 + jnp.exp(-jnp.abs(z)))
    tri = tri_ref[...]
    cum = carry_ref[...] + sum(_dot(tri, part) for part in _split3(logf))
    carry_ref[...] = cum[tm - 1:tm, :]
    kd = sum(_dot(part, scat_ref[p]) for p, part in enumerate(_split3(cum * (-LOG2E))))
    for c in range(kd_ref.shape[0]):
        kd_ref[c] = kd[:, c * LANES:(c + 1) * LANES].astype(BF16)


def _fox_proj(h, gain, wqk, wvt, wf, bf, *, batch, seq, tm=512):
    t, d = h.shape
    ns = seq // tm
    n_pairs = FOX_HEADS // 2
    tri = jnp.tril(jnp.ones((tm, tm), BF16))
    scat = np.zeros((GATE_PARTS, LANES, n_pairs * LANES), np.float32)
    for p in range(GATE_PARTS):
        for hd in range(FOX_HEADS):
            scat[p, hd, (hd // 2) * LANES + GATE_PARTS * (hd % 2) + p] = 1.0
    fixed2 = lambda b, i: (0, 0)
    rows = lambda b, i: (0, b * ns + i, 0)
    return pl.pallas_call(
        functools.partial(_fox_proj_kernel, tm=tm),
        grid=(batch, ns),
        in_specs=[
            pl.BlockSpec((tm, d), lambda b, i: (b * ns + i, 0)),
            pl.BlockSpec((1, d), fixed2),
            pl.BlockSpec(wqk.shape, fixed2),
            pl.BlockSpec(wvt.shape, fixed2),
            pl.BlockSpec(wf.shape, fixed2),
            pl.BlockSpec((1, LANES), fixed2),
            pl.BlockSpec((tm, tm), fixed2),
            pl.BlockSpec(scat.shape, lambda b, i: (0, 0, 0)),
        ],
        out_specs=[
            pl.BlockSpec((2 * n_pairs, tm, LANES), rows),
            pl.BlockSpec((n_pairs, LANES, tm), lambda b, i: (0, 0, b * ns + i)),
            pl.BlockSpec((n_pairs, tm, LANES), rows),
        ],
        out_shape=[
            jax.ShapeDtypeStruct((2 * n_pairs, t, LANES), BF16),
            jax.ShapeDtypeStruct((n_pairs, LANES, t), BF16),
            jax.ShapeDtypeStruct((n_pairs, t, LANES), BF16),
        ],
        scratch_shapes=[pltpu.VMEM((1, LANES), F32)],
        compiler_params=_params("arbitrary", "arbitrary"),
        name="fox_proj",
    )(h, gain.reshape(1, d), wqk, wvt, wf, bf, tri, jnp.asarray(scat, BF16))


def _flash_kernel(*refs, sep, has_bias, tq):
    if has_bias:
        q_ref, k_ref, vt_ref, kd_ref, o_ref, qs_ref, vta_ref, acc_ref, s_ref, mx_ref = refs
    else:
        q_ref, k_ref, vt_ref, o_ref, qs_ref, vta_ref, acc_ref, s_ref, mx_ref = refs
        kd_ref = None
    i = pl.program_id(2)
    half = LANES // 2

    @pl.when(i == 0)
    def _():
        vt = vt_ref[0]
        r = lax.broadcasted_iota(jnp.int32, vt.shape, 0)
        one = jnp.ones_like(vt)
        vta_ref[0] = jnp.where(r < half, vt, one)
        vta_ref[1] = jnp.where(r < half, one, vt)

    lane = lax.broadcasted_iota(jnp.int32, (tq, LANES), 1)
    for hh in range(2):
        if sep:
            q_h = q_ref[hh]
        else:
            q = q_ref[0]
            own = (lane < half) if hh == 0 else (lane >= half)
            q_h = jnp.where(own, q, jnp.zeros_like(q))
        qs_ref[hh, :, 0:LANES] = q_h
        if has_bias:
            sel = jnp.logical_and(lane >= GATE_PARTS * hh, lane < GATE_PARTS * (hh + 1))
            qs_ref[hh, :, LANES:2 * LANES] = jnp.where(sel, 1.0, 0.0).astype(BF16)
    key = lax.broadcasted_iota(jnp.int32, (tq, tq), 0)
    qry = lax.broadcasted_iota(jnp.int32, (tq, tq), 1)
    acc_ref[...] = jnp.zeros_like(acc_ref)

    def scores(kb, slot):
        start = pl.multiple_of(kb * tq, tq)
        for hh in range(2):
            k_blk = k_ref[hh if sep else 0, pl.ds(start, tq), :]
            if has_bias:
                k_blk = jnp.concatenate([k_blk, kd_ref[0, pl.ds(start, tq), :]], axis=1)
            st = _dot_nt(k_blk, qs_ref[hh])
            s_ref[slot, hh] = st
            mx_ref[slot, hh] = _colmax(st)

    def consume(kb, slot, m_old, diag):
        start = pl.multiple_of(kb * tq, tq)
        m_out = []
        for hh in range(2):
            st = s_ref[slot, hh]
            if diag:
                st = jnp.where(key <= qry, st, NEG_INF)
                mx = _colmax(st)
            else:
                mx = mx_ref[slot, hh]
            m_new = jnp.maximum(m_old[hh], mx)
            alpha = jnp.exp2(m_old[hh] - m_new)
            pt = jnp.exp2(st - m_new).astype(BF16)
            acc_ref[hh] = alpha * acc_ref[hh] + _dot(vta_ref[hh, :, pl.ds(start, tq)], pt)
            m_out.append(m_new)
        return tuple(m_out)

    def body(j, m_old):
        scores(2 * j + 1, 1)
        m_mid = consume(2 * j, 0, m_old, False)
        scores(2 * j + 2, 0)
        return consume(2 * j + 1, 1, m_mid, False)

    def write_out():
        acc_a, acc_b = acc_ref[0], acc_ref[1]
        r = lax.broadcasted_iota(jnp.int32, (LANES, tq), 0)
        o_t = jnp.where(r < half, acc_a / acc_a[half:half + 1, :], acc_b / acc_b[0:1, :])
        o_ref[0] = o_t.T.astype(o_ref.dtype)

    scores(0, 0)
    m0 = jnp.full((1, tq), NEG_INF, F32)
    m = lax.fori_loop(0, lax.shift_right_logical(i, 1), body, (m0, m0))

    @pl.when(i % 2 == 0)
    def _():
        consume(i, 0, m, True)
        write_out()

    @pl.when(i % 2 == 1)
    def _():
        scores(i, 1)
        consume(i, 1, consume(i - 1, 0, m, False), True)
        write_out()


def _flash(qk, vt, kd, *, batch, seq, n_pairs, q_blk, k_blk, sep, tq=512):
    t = qk.shape[1]
    nq = seq // tq
    w = 2 if sep else 1
    kc = 2 * LANES if kd is not None else LANES
    in_specs = [
        pl.BlockSpec((w, tq, LANES), lambda b, hp, i: (q_blk // w + hp, b * nq + i, 0)),
        pl.BlockSpec((w, seq, LANES), lambda b, hp, i: (k_blk // w + hp, b, 0)),
        pl.BlockSpec((1, LANES, seq), lambda b, hp, i: (hp, 0, b)),
    ]
    args = [qk, qk, vt]
    if kd is not None:
        in_specs.append(pl.BlockSpec((1, seq, LANES), lambda b, hp, i: (hp, b, 0)))
        args.append(kd)
    return pl.pallas_call(
        functools.partial(_flash_kernel, sep=sep, has_bias=kd is not None, tq=tq),
        grid=(batch, n_pairs, nq),
        in_specs=in_specs,
        out_specs=pl.BlockSpec((1, tq, LANES), lambda b, hp, i: (hp, b * nq + i, 0)),
        out_shape=jax.ShapeDtypeStruct((n_pairs, t, LANES), BF16),
        scratch_shapes=[
            pltpu.VMEM((2, tq, kc), BF16),
            pltpu.VMEM((2, LANES, seq), BF16),
            pltpu.VMEM((2, LANES, tq), F32),
            pltpu.VMEM((2, 2, tq, tq), F32),
            pltpu.VMEM((2, 2, 1, tq), F32),
        ],
        compiler_params=_params("arbitrary", "arbitrary", "arbitrary"),
        name="flash_sep" if sep else "flash_shared",
    )(*args)


def _dil_attn_kernel(*refs):
    n_in = 5 * len(DIL_PATTERNS)
    o_ref, og_ref, lg_ref = refs[n_in:]
    i = pl.program_id(2)
    w = DIL_WINDOW_KEYS
    lane = lax.broadcasted_iota(jnp.int32, (w, LANES), 1)
    qk_first = (lane % 64) < 32
    v_first = lane < 64
    qi = lax.broadcasted_iota(jnp.int32, (w, 2 * w), 0)
    kk = lax.broadcasted_iota(jnp.int32, (w, 2 * w), 1)
    band = jnp.logical_and(kk >= qi, kk <= qi + w)
    band_first = jnp.logical_and(band, kk >= jnp.where(i > 0, 0, w))
    for g, (_, d) in enumerate(DIL_PATTERNS):
        q_ref, k_ref, kp_ref, v_ref, vp_ref = refs[5 * g:5 * g + 5]
        for r in range(d):
            ls = slice(r * LANES, (r + 1) * LANES)
            for sub in range(DIL_TOKENS // d // w):
                q_u = q_ref[0, 0, sub * w:(sub + 1) * w, ls]
                if sub == 0:
                    k_u = jnp.concatenate([kp_ref[0, 0, :, ls], k_ref[0, 0, 0:w, ls]], axis=0)
                    v_u = jnp.concatenate([vp_ref[0, 0, :, ls], v_ref[0, 0, 0:w, ls]], axis=0)
                    msk = band_first
                else:
                    k_u = k_ref[0, 0, (sub - 1) * w:(sub + 1) * w, ls]
                    v_u = v_ref[0, 0, (sub - 1) * w:(sub + 1) * w, ls]
                    msk = band
                zero = jnp.zeros_like(q_u)
                res = []
                for hh in range(2):
                    q_h = jnp.where(qk_first, q_u, zero) if hh == 0 else jnp.where(qk_first, zero, q_u)
                    s = jnp.where(msk, _dot_nt(q_h, k_u), NEG_INF)
                    m = jnp.max(s, axis=-1, keepdims=True)
                    p = jnp.exp(s - m)
                    l = jnp.sum(p, axis=-1, keepdims=True)
                    o = _dot(p.astype(BF16), v_u) / l
                    res.append((o, m + jnp.log(l)))
                tok = pl.ds(sub * w * d + r, w, stride=d) if d > 1 else pl.ds(sub * w, w)
                og_ref[g, tok, :] = jnp.where(v_first, res[0][0], res[1][0])
                lg_ref[g, tok, :] = jnp.where(
                    v_first, jnp.broadcast_to(res[0][1], (w, LANES)), jnp.broadcast_to(res[1][1], (w, LANES)))
    lse = [lg_ref[g] for g in range(len(DIL_PATTERNS))]
    top = functools.reduce(jnp.maximum, lse)
    e = [jnp.exp(x - top) for x in lse]
    num = sum(e[g] * og_ref[g] for g in range(len(DIL_PATTERNS)))
    o_ref[0] = (num / sum(e)).astype(o_ref.dtype)


def _dil_attn(groups, *, batch, seq):
    n_pairs = DIL_HEADS // 2
    n_tok_blk = seq // DIL_TOKENS
    in_specs, args = [], []
    for arr, (_, d) in zip(groups, DIL_PATTERNS):
        rows = DIL_TOKENS // d
        sub_per_blk = rows // DIL_WINDOW_KEYS

        def cur(base, rows=rows, d=d):
            return pl.BlockSpec((1, 1, rows, d * LANES), lambda b, hp, i: (base + hp, b, i, 0))

        def prev(base, spb=sub_per_blk, d=d):
            return pl.BlockSpec((1, 1, DIL_WINDOW_KEYS, d * LANES),
                                lambda b, hp, i: (base + hp, b, jnp.maximum(i * spb - 1, 0), 0))

        in_specs += [cur(0), cur(n_pairs), prev(n_pairs), cur(2 * n_pairs), prev(2 * n_pairs)]
        args += [arr] * 5
    n_g = len(DIL_PATTERNS)
    return pl.pallas_call(
        _dil_attn_kernel,
        grid=(batch, n_pairs, n_tok_blk),
        in_specs=in_specs,
        out_specs=pl.BlockSpec((1, DIL_TOKENS, LANES), lambda b, hp, i: (hp, b * n_tok_blk + i, 0)),
        out_shape=jax.ShapeDtypeStruct((n_pairs, batch * seq, LANES), BF16),
        scratch_shapes=[pltpu.VMEM((n_g, DIL_TOKENS, LANES), F32), pltpu.VMEM((n_g, DIL_TOKENS, LANES), F32)],
        compiler_params=_params("parallel", "parallel", "arbitrary"),
        name="dil_attn",
    )(*args)


def _mlp_kernel(h_ref, o_ref, wo_ref, g_ref, wu_ref, wd_ref, fg_ref, out_ref, xn_ref, acc_ref, *, final_norm):
    f = pl.program_id(1)

    @pl.when(f == 0)
    def _():
        a = jnp.concatenate([o_ref[c] for c in range(o_ref.shape[0])], axis=-1)
        h1 = h_ref[...] + _dot(a, wo_ref[...])
        xn_ref[...] = _rms(h1, g_ref[...]).astype(BF16)
        acc_ref[...] = h1

    u = jnp.maximum(_dot(xn_ref[...], wu_ref[...]), 0.0)
    acc_ref[...] += _dot((u * u).astype(BF16), wd_ref[...])

    @pl.when(f == pl.num_programs(1) - 1)
    def _():
        y = acc_ref[...]
        out_ref[...] = _rms(y, fg_ref[...]) if final_norm else y


def _mlp(h, o, wo, gain, w_up, w_down, final_gain, *, final_norm, tm=1024, tf=1024):
    t, d = h.shape
    dff = w_up.shape[1]
    return pl.pallas_call(
        functools.partial(_mlp_kernel, final_norm=final_norm),
        grid=(t // tm, dff // tf),
        in_specs=[
            pl.BlockSpec((tm, d), lambda i, f: (i, 0)),
            pl.BlockSpec((o.shape[0], tm, LANES), lambda i, f: (0, i, 0)),
            pl.BlockSpec(wo.shape, lambda i, f: (0, 0)),
            pl.BlockSpec((1, d), lambda i, f: (0, 0)),
            pl.BlockSpec((d, tf), lambda i, f: (0, f)),
            pl.BlockSpec((tf, d), lambda i, f: (f, 0)),
            pl.BlockSpec((1, d), lambda i, f: (0, 0)),
        ],
        out_specs=pl.BlockSpec((tm, d), lambda i, f: (i, 0)),
        out_shape=jax.ShapeDtypeStruct((t, d), F32),
        scratch_shapes=[pltpu.VMEM((tm, d), BF16), pltpu.VMEM((tm, d), F32)],
        compiler_params=_params("parallel", "arbitrary"),
        name="mlp_final" if final_norm else "mlp",
    )(h, o, wo, gain.reshape(1, d), w_up, w_down, final_gain.reshape(1, d))


def _rope_halves(seq, dim):
    inv = 1.0 / (ROPE_THETA ** (jnp.arange(0, dim, 2, dtype=F32) / dim))
    ang = jnp.arange(seq, dtype=F32)[:, None] * inv[None, :]
    return jnp.cos(ang), jnp.sin(ang)


def _mla_tables(seq):
    cos, sin = _rope_halves(seq, MLA_ROPE)
    one = lambda n: jnp.ones((seq, n), F32)
    zero = lambda n: jnp.zeros((seq, n), F32)
    cos_t = jnp.concatenate([one(32), cos, one(16), one(32), cos, one(16)], axis=1)
    sin_t = jnp.concatenate([zero(32), -sin, zero(16), zero(32), sin, zero(16)], axis=1)
    return cos_t, sin_t


def _dil_tables(seq, d, tm):
    cos, sin = _rope_halves(seq, DIL_DIM)
    order = lambda x: x.reshape(seq // tm, tm // d, d, LANES).transpose(0, 2, 1, 3).reshape(seq, LANES)
    return (order(jnp.concatenate([cos] * 4, axis=1)),
            order(jnp.concatenate([-sin, -sin, sin, sin], axis=1)))


def _mla_weights(wq_a, wq_b, wkv_a, wkv_b):
    d = wq_a.shape[0]
    z = lambda r, n: jnp.zeros((r, n), F32)
    kpe = wkv_a[:, MLA_KV_RANK:]
    w_a = jnp.concatenate(
        [wq_a, wkv_a[:, :MLA_KV_RANK], z(d, 32), kpe[:, :16], z(d, 48), kpe[:, 16:], z(d, 16)], axis=1)
    scale = (MLA_NOPE + MLA_ROPE) ** -0.5 * LOG2E
    q3 = wq_b.reshape(MLA_Q_RANK, MLA_HEADS, MLA_NOPE + MLA_ROPE) * scale
    zq = jnp.zeros((MLA_Q_RANK, MLA_HEADS, 16), F32)
    wq = jnp.concatenate(
        [q3[..., :32], q3[..., 64:80], zq, q3[..., 32:64], q3[..., 80:96], zq], axis=-1)
    kv3 = wkv_b.reshape(MLA_KV_RANK, MLA_HEADS, MLA_NOPE + MLA_V)
    zk = jnp.zeros((MLA_KV_RANK, MLA_HEADS, 32), F32)
    wk = jnp.concatenate([kv3[..., :32], zk, kv3[..., 32:64], zk], axis=-1)
    wvt = kv3[..., MLA_NOPE:].reshape(MLA_KV_RANK, -1).T
    return (w_a.astype(BF16), wq.reshape(MLA_Q_RANK, -1).astype(BF16),
            wk.reshape(MLA_KV_RANK, -1).astype(BF16), wvt.astype(BF16))


def _dil_weights(w_qkv):
    d = w_qkv.shape[0]
    n_g = len(DIL_PATTERNS)
    n = DIL_HEADS * DIL_DIM
    w = w_qkv.reshape(d, 3, n_g, n)
    qk = w[:, :2].reshape(d, 2, n_g, n // LANES, 2, 2, 32)
    qk = qk * jnp.array([DIL_DIM ** -0.5, 1.0], F32).reshape(1, 2, 1, 1, 1, 1, 1)
    qk = qk.transpose(0, 1, 2, 3, 5, 4, 6).reshape(d, 2, n_g, n)
    w = jnp.concatenate([qk, w[:, 2:]], axis=1).astype(BF16)
    return [w[:, :, g].reshape(d, 3 * n) for g in range(n_g)]


def _mla_attn(h, attn_norm, wq_a, q_norm, wq_b, wkv_a, kv_norm, wkv_b, tables, *, batch, seq):
    w_a, wq, wk, wvt = _mla_weights(wq_a, wq_b, wkv_a, wkv_b)
    cq, ckv, kpe = _mla_a(h, attn_norm, w_a, q_norm, kv_norm, *tables, seq=seq)
    qk, vt = _mla_b(cq, ckv, kpe, wq, wk, wvt, *tables, seq=seq)
    return _flash(qk, vt, None, batch=batch, seq=seq, n_pairs=MLA_HEADS // 2,
                  q_blk=0, k_blk=MLA_HEADS, sep=True)


def _fox_attn(h, attn_norm, w_qkv, w_f, b_f, *, batch, seq):
    n = FOX_HEADS * FOX_DIM
    d = w_qkv.shape[0]
    wqk = jnp.concatenate([w_qkv[:, :n] * (FOX_DIM ** -0.5 * LOG2E), w_qkv[:, n:2 * n]], axis=1)
    wvt = w_qkv[:, 2 * n:].T
    wf = jnp.concatenate([w_f, jnp.zeros((d, LANES - FOX_HEADS), F32)], axis=1)
    bf = jnp.concatenate([b_f.astype(F32), jnp.zeros((LANES - FOX_HEADS,), F32)]).reshape(1, LANES)
    qk, vt, kd = _fox_proj(h, attn_norm, wqk.astype(BF16), wvt.astype(BF16), wf.astype(BF16), bf,
                           batch=batch, seq=seq)
    n_pairs = FOX_HEADS // 2
    return _flash(qk, vt, kd, batch=batch, seq=seq, n_pairs=n_pairs, q_blk=0, k_blk=n_pairs, sep=False)


def _dil_attn_layer(h, attn_norm, w_qkv, *, batch, seq, tm=512):
    groups = []
    for w_g, (_, d) in zip(_dil_weights(w_qkv), DIL_PATTERNS):
        groups.append(_dil_proj(h, attn_norm, w_g, *_dil_tables(seq, d, tm),
                                batch=batch, seq=seq, d=d, tm=tm))
    return _dil_attn(groups, batch=batch, seq=seq)


def kernel(x, l0_attn_norm, l0_mla_wq_a, l0_mla_q_norm, l0_mla_wq_b, l0_mla_wkv_a, l0_mla_kv_norm, l0_mla_wkv_b, l0_mla_wo, l0_mlp_norm, l0_w_up, l0_w_down, l1_attn_norm, l1_fox_w_qkv, l1_fox_w_f, l1_fox_b_f, l1_fox_wo, l1_mlp_norm, l1_w_up, l1_w_down, l2_attn_norm, l2_dil_w_qkv, l2_dil_wo, l2_mlp_norm, l2_w_up, l2_w_down, l3_attn_norm, l3_mla_wq_a, l3_mla_q_norm, l3_mla_wq_b, l3_mla_wkv_a, l3_mla_kv_norm, l3_mla_wkv_b, l3_mla_wo, l3_mlp_norm, l3_w_up, l3_w_down, final_norm):
    batch, seq, d = x.shape
    assert seq % DIL_TOKENS == 0 and (batch * seq) % 1024 == 0
    for window, dil in DIL_PATTERNS:
        assert window // dil == DIL_WINDOW_KEYS
    mla_t = _mla_tables(seq)
    kw = dict(batch=batch, seq=seq)
    bf = lambda w: w.astype(BF16)
    h = x.reshape(batch * seq, d)

    o = _mla_attn(h, l0_attn_norm, l0_mla_wq_a, l0_mla_q_norm, l0_mla_wq_b, l0_mla_wkv_a,
                  l0_mla_kv_norm, l0_mla_wkv_b, mla_t, **kw)
    h = _mlp(h, o, bf(l0_mla_wo), l0_mlp_norm, bf(l0_w_up), bf(l0_w_down), final_norm, final_norm=False)

    o = _fox_attn(h, l1_attn_norm, l1_fox_w_qkv, l1_fox_w_f, l1_fox_b_f, **kw)
    h = _mlp(h, o, bf(l1_fox_wo), l1_mlp_norm, bf(l1_w_up), bf(l1_w_down), final_norm, final_norm=False)

    o = _dil_attn_layer(h, l2_attn_norm, l2_dil_w_qkv, **kw)
    h = _mlp(h, o, bf(l2_dil_wo), l2_mlp_norm, bf(l2_w_up), bf(l2_w_down), final_norm, final_norm=False)

    o = _mla_attn(h, l3_attn_norm, l3_mla_wq_a, l3_mla_q_norm, l3_mla_wq_b, l3_mla_wkv_a,
                  l3_mla_kv_norm, l3_mla_wkv_b, mla_t, **kw)
    h = _mlp(h, o, bf(l3_mla_wo), l3_mlp_norm, bf(l3_w_up), bf(l3_w_down), final_norm, final_norm=True)
    return h.reshape(batch, seq, d)
```

```python
import functools
import math

import numpy as np
import jax
import jax.numpy as jnp
from jax import lax
from jax.experimental import pallas as pl
from jax.experimental.pallas import tpu as pltpu

F32 = jnp.float32
BF16 = jnp.bfloat16

LANES = 128
NORM_EPS = 1e-6
NEG_INF = -1e30
ROPE_THETA = 10000.0
LOG2E = math.log2(math.e)

MLA_HEADS = 16
MLA_Q_RANK = 384
MLA_KV_RANK = 256
MLA_NOPE = 64
MLA_ROPE = 32
MLA_V = 64
FOX_HEADS = 16
FOX_DIM = 64
DIL_PATTERNS = ((128, 1), (512, 4), (2048, 16))
DIL_HEADS = 16
DIL_DIM = 64
DIL_WINDOW_KEYS = 128
DIL_TOKENS = 2048
GATE_PARTS = 3

VMEM_LIMIT = 56 * 2**20


def _params(*sem):
    return pltpu.CompilerParams(dimension_semantics=sem, vmem_limit_bytes=VMEM_LIMIT)


def _rms(x, g):
    ms = jnp.mean(x * x, axis=-1, keepdims=True)
    return x * lax.rsqrt(ms + NORM_EPS) * g


def _dot(a, b):
    return jnp.dot(a, b, preferred_element_type=F32)


def _dot_nt(a, b):
    return lax.dot_general(a, b, (((1,), (1,)), ((), ())), preferred_element_type=F32)


def _rope(blk, cos, sin):
    return blk * cos + pltpu.roll(blk, 64, axis=1) * sin


def _colmax(st):
    n, c = st.shape
    return jnp.max(jnp.max(st.reshape(8, n // 8, c), axis=0), axis=0, keepdims=True)


def _split3(x):
    hi = x.astype(BF16)
    r1 = x - hi.astype(F32)
    mid = r1.astype(BF16)
    lo = (r1 - mid.astype(F32)).astype(BF16)
    return hi, mid, lo


def _dil_proj_kernel(x_ref, g_ref, w_ref, cos_ref, sin_ref, o_ref, xs_ref, xn_ref, *, d, tm):
    rpr = tm // d
    nc = x_ref.shape[1] // LANES
    xn = _rms(x_ref[...], g_ref[...])
    if d == 1:
        xn_ref[...] = xn.astype(BF16)
    else:
        for c in range(nc):
            xs_ref[c] = xn[:, c * LANES:(c + 1) * LANES]
        for c in range(nc):
            for r in range(d):
                xn_ref[r * rpr:(r + 1) * rpr, c * LANES:(c + 1) * LANES] = (
                    xs_ref[c, pl.ds(r, rpr, stride=d), :].astype(BF16))

    nb = o_ref.shape[0] // 3
    for j in range(3):
        acc = _dot(xn_ref[...], w_ref[:, j * nb * LANES:(j + 1) * nb * LANES])
        for c in range(nb):
            for r in range(d):
                rs = slice(r * rpr, (r + 1) * rpr)
                blk = acc[rs, c * LANES:(c + 1) * LANES]
                if j < 2:
                    blk = _rope(blk, cos_ref[rs, :], sin_ref[rs, :])
                o_ref[j * nb + c, 0, :, r * LANES:(r + 1) * LANES] = blk.astype(o_ref.dtype)


def _dil_proj(h, gain, w, cos_t, sin_t, *, batch, seq, d, tm):
    t, dm = h.shape
    ns = seq // tm
    nb = w.shape[1] // LANES
    return pl.pallas_call(
        functools.partial(_dil_proj_kernel, d=d, tm=tm),
        grid=(batch, ns),
        in_specs=[
            pl.BlockSpec((tm, dm), lambda b, i: (b * ns + i, 0)),
            pl.BlockSpec((1, dm), lambda b, i: (0, 0)),
            pl.BlockSpec(w.shape, lambda b, i: (0, 0)),
            pl.BlockSpec((tm, LANES), lambda b, i: (i, 0)),
            pl.BlockSpec((tm, LANES), lambda b, i: (i, 0)),
        ],
        out_specs=pl.BlockSpec((nb, 1, tm // d, d * LANES), lambda b, i: (0, b, i, 0)),
        out_shape=jax.ShapeDtypeStruct((nb, batch, seq // d, d * LANES), BF16),
        scratch_shapes=[pltpu.VMEM((dm // LANES, tm, LANES), F32), pltpu.VMEM((tm, dm), BF16)],
        compiler_params=_params("parallel", "parallel"),
        name="dil_proj_d%d" % d,
    )(h, gain.reshape(1, dm), w, cos_t, sin_t)


def _mla_a_kernel(x_ref, g_ref, w_ref, qn_ref, kvn_ref, cos_ref, sin_ref, cq_ref, ckv_ref, kpe_ref):
    xn = _rms(x_ref[...], g_ref[...]).astype(BF16)
    y = _dot(xn, w_ref[...])
    cq_ref[...] = _rms(y[:, :MLA_Q_RANK], qn_ref[...]).astype(BF16)
    ckv_ref[...] = _rms(y[:, MLA_Q_RANK:MLA_Q_RANK + MLA_KV_RANK], kvn_ref[...]).astype(BF16)
    kpe_ref[...] = _rope(y[:, MLA_Q_RANK + MLA_KV_RANK:], cos_ref[...], sin_ref[...]).astype(BF16)


def _mla_a(h, gain, w_a, q_norm, kv_norm, cos_t, sin_t, *, seq, tm=1024):
    t, d = h.shape
    n = w_a.shape[1]
    ns = seq // tm
    row = lambda i: (i, 0)
    fixed = lambda i: (0, 0)
    tab = lambda i: (i % ns, 0)
    return pl.pallas_call(
        _mla_a_kernel,
        grid=(t // tm,),
        in_specs=[
            pl.BlockSpec((tm, d), row),
            pl.BlockSpec((1, d), fixed),
            pl.BlockSpec((d, n), fixed),
            pl.BlockSpec((1, MLA_Q_RANK), fixed),
            pl.BlockSpec((1, MLA_KV_RANK), fixed),
            pl.BlockSpec((tm, LANES), tab),
            pl.BlockSpec((tm, LANES), tab),
        ],
        out_specs=[
            pl.BlockSpec((tm, MLA_Q_RANK), row),
            pl.BlockSpec((tm, MLA_KV_RANK), row),
            pl.BlockSpec((tm, LANES), row),
        ],
        out_shape=[
            jax.ShapeDtypeStruct((t, MLA_Q_RANK), BF16),
            jax.ShapeDtypeStruct((t, MLA_KV_RANK), BF16),
            jax.ShapeDtypeStruct((t, LANES), BF16),
        ],
        compiler_params=_params("parallel"),
        name="mla_a",
    )(h, gain.reshape(1, d), w_a, q_norm.reshape(1, -1), kv_norm.reshape(1, -1), cos_t, sin_t)


def _mla_b_kernel(cq_ref, ckv_ref, kpe_ref, wq_ref, wk_ref, wvt_ref, cos_ref, sin_ref, qk_ref, vt_ref):
    q = _dot(cq_ref[...], wq_ref[...])
    for c in range(MLA_HEADS):
        blk = q[:, c * LANES:(c + 1) * LANES]
        qk_ref[c] = _rope(blk, cos_ref[...], sin_ref[...]).astype(BF16)
    k = _dot(ckv_ref[...], wk_ref[...])
    kpe = kpe_ref[...].astype(F32)
    for c in range(MLA_HEADS):
        qk_ref[MLA_HEADS + c] = (k[:, c * LANES:(c + 1) * LANES] + kpe).astype(BF16)
    vt = _dot_nt(wvt_ref[...], ckv_ref[...])
    for c in range(MLA_HEADS // 2):
        vt_ref[c] = vt[c * LANES:(c + 1) * LANES, :].astype(BF16)


def _mla_b(cq, ckv, kpe, wq, wk, wvt, cos_t, sin_t, *, seq, tm=512):
    t = cq.shape[0]
    ns = seq // tm
    row = lambda i: (i, 0)
    fixed = lambda i: (0, 0)
    tab = lambda i: (i % ns, 0)
    return pl.pallas_call(
        _mla_b_kernel,
        grid=(t // tm,),
        in_specs=[
            pl.BlockSpec((tm, MLA_Q_RANK), row),
            pl.BlockSpec((tm, MLA_KV_RANK), row),
            pl.BlockSpec((tm, LANES), row),
            pl.BlockSpec(wq.shape, fixed),
            pl.BlockSpec(wk.shape, fixed),
            pl.BlockSpec(wvt.shape, fixed),
            pl.BlockSpec((tm, LANES), tab),
            pl.BlockSpec((tm, LANES), tab),
        ],
        out_specs=[
            pl.BlockSpec((2 * MLA_HEADS, tm, LANES), lambda i: (0, i, 0)),
            pl.BlockSpec((MLA_HEADS // 2, LANES, tm), lambda i: (0, 0, i)),
        ],
        out_shape=[
            jax.ShapeDtypeStruct((2 * MLA_HEADS, t, LANES), BF16),
            jax.ShapeDtypeStruct((MLA_HEADS // 2, LANES, t), BF16),
        ],
        compiler_params=_params("parallel"),
        name="mla_b",
    )(cq, ckv, kpe, wq, wk, wvt, cos_t, sin_t)


def _fox_proj_kernel(x_ref, g_ref, wqk_ref, wvt_ref, wf_ref, bf_ref, tri_ref, scat_ref,
                     qk_ref, vt_ref, kd_ref, carry_ref, *, tm):
    @pl.when(pl.program_id(1) == 0)
    def _():
        carry_ref[...] = jnp.zeros_like(carry_ref)

    xn = _rms(x_ref[...], g_ref[...]).astype(BF16)
    qk = _dot(xn, wqk_ref[...])
    for c in range(qk_ref.shape[0]):
        qk_ref[c] = qk[:, c * LANES:(c + 1) * LANES].astype(BF16)
    vt = _dot_nt(wvt_ref[...], xn)
    for c in range(vt_ref.shape[0]):
        vt_ref[c] = vt[c * LANES:(c + 1) * LANES, :].astype(BF16)

    z = _dot(xn, wf_ref[...]) + bf_ref[...]
    logf = jnp.minimum(z, 0.0) - jnp.log(1.0 + jnp.exp(-jnp.abs(z)))
    tri = tri_ref[...]
    cum = carry_ref[...] + sum(_dot(tri, part) for part in _split3(logf))
    carry_ref[...] = cum[tm - 1:tm, :]
    kd = sum(_dot(part, scat_ref[p]) for p, part in enumerate(_split3(cum * (-LOG2E))))
    for c in range(kd_ref.shape[0]):
        kd_ref[c] = kd[:, c * LANES:(c + 1) * LANES].astype(BF16)


def _fox_proj(h, gain, wqk, wvt, wf, bf, *, batch, seq, tm=512):
    t, d = h.shape
    ns = seq // tm
    n_pairs = FOX_HEADS // 2
    tri = jnp.tril(jnp.ones((tm, tm), BF16))
    scat = np.zeros((GATE_PARTS, LANES, n_pairs * LANES), np.float32)
    for p in range(GATE_PARTS):
        for hd in range(FOX_HEADS):
            scat[p, hd, (hd // 2) * LANES + GATE_PARTS * (hd % 2) + p] = 1.0
    fixed2 = lambda b, i: (0, 0)
    rows = lambda b, i: (0, b * ns + i, 0)
    return pl.pallas_call(
        functools.partial(_fox_proj_kernel, tm=tm),
        grid=(batch, ns),
        in_specs=[
            pl.BlockSpec((tm, d), lambda b, i: (b * ns + i, 0)),
            pl.BlockSpec((1, d), fixed2),
            pl.BlockSpec(wqk.shape, fixed2),
            pl.BlockSpec(wvt.shape, fixed2),
            pl.BlockSpec(wf.shape, fixed2),
            pl.BlockSpec((1, LANES), fixed2),
            pl.BlockSpec((tm, tm), fixed2),
            pl.BlockSpec(scat.shape, lambda b, i: (0, 0, 0)),
        ],
        out_specs=[
            pl.BlockSpec((2 * n_pairs, tm, LANES), rows),
            pl.BlockSpec((n_pairs, LANES, tm), lambda b, i: (0, 0, b * ns + i)),
            pl.BlockSpec((n_pairs, tm, LANES), rows),
        ],
        out_shape=[
            jax.ShapeDtypeStruct((2 * n_pairs, t, LANES), BF16),
            jax.ShapeDtypeStruct((n_pairs, LANES, t), BF16),
            jax.ShapeDtypeStruct((n_pairs, t, LANES), BF16),
        ],
        scratch_shapes=[pltpu.VMEM((1, LANES), F32)],
        compiler_params=_params("arbitrary", "arbitrary"),
        name="fox_proj",
    )(h, gain.reshape(1, d), wqk, wvt, wf, bf, tri, jnp.asarray(scat, BF16))


def _flash_kernel(*refs, sep, has_bias, tq, unroll):
    if has_bias:
        q_ref, k_ref, vt_ref, kd_ref, o_ref, qs_ref, vta_ref, acc_ref, s_ref, mx_ref, m_ref = refs
    else:
        q_ref, k_ref, vt_ref, o_ref, qs_ref, vta_ref, acc_ref, s_ref, mx_ref, m_ref = refs
        kd_ref = None
    half = LANES // 2
    nq = acc_ref.shape[0]
    seq = nq * tq

    vt = vt_ref[0]
    r = lax.broadcasted_iota(jnp.int32, vt.shape, 0)
    one = jnp.ones_like(vt)
    vta_ref[0] = jnp.where(r < half, vt, one)
    vta_ref[1] = jnp.where(r < half, one, vt)
    if not sep or has_bias:
        lane = lax.broadcasted_iota(jnp.int32, (seq, LANES), 1)
        for hh in range(2):
            if sep:
                q_h = q_ref[hh]
            else:
                q = q_ref[0]
                own = (lane < half) if hh == 0 else (lane >= half)
                q_h = jnp.where(own, q, jnp.zeros_like(q))
            qs_ref[hh, :, 0:LANES] = q_h
            if has_bias:
                sel = jnp.logical_and(lane >= GATE_PARTS * hh, lane < GATE_PARTS * (hh + 1))
                qs_ref[hh, :, LANES:2 * LANES] = jnp.where(sel, 1.0, 0.0).astype(BF16)
    key = lax.broadcasted_iota(jnp.int32, (tq, tq), 0)
    qry = lax.broadcasted_iota(jnp.int32, (tq, tq), 1)
    src = qs_ref if (not sep or has_bias) else q_ref

    def scores(i, kb, slot, diag):
        qstart, kstart = i * tq, kb * tq
        if not diag:
            qstart, kstart = pl.multiple_of(qstart, tq), pl.multiple_of(kstart, tq)
        for hh in range(2):
            k_blk = k_ref[hh if sep else 0, pl.ds(kstart, tq), :]
            if has_bias:
                k_blk = jnp.concatenate([k_blk, kd_ref[0, pl.ds(kstart, tq), :]], axis=1)
            st = _dot_nt(k_blk, src[hh, pl.ds(qstart, tq), :])
            if diag:
                st = jnp.where(key <= qry, st, NEG_INF)
            s_ref[slot, hh] = st
            mx_ref[slot, hh] = _colmax(st)

    def consume(i, kb, slot, first):
        kstart = kb * tq if first else pl.multiple_of(kb * tq, tq)
        for hh in range(2):
            m_new = mx_ref[slot, hh]
            if not first:
                m_old = m_ref[i, hh]
                m_new = jnp.maximum(m_old, m_new)
            m_ref[i, hh] = m_new
            pt = jnp.exp2(s_ref[slot, hh] - m_new).astype(BF16)
            pv = _dot(vta_ref[hh, :, pl.ds(kstart, tq)], pt)
            acc_ref[i, hh] = pv if first else jnp.exp2(m_old - m_new) * acc_ref[i, hh] + pv

    scores(0, 0, 0, True)
    for i in range(nq):
        if i + 1 < nq:
            scores(i + 1, i + 1, (i + 1) % 2, True)
        elif nq > 1:
            scores(1, 0, (i + 1) % 2, False)
        consume(i, i, i % 2, True)

    def advance(i, kb):
        last = kb + 1 == i
        return jnp.where(last, i + 1, i), jnp.where(last, 0, kb + 1)

    def trip(_, carry):
        i, kb = carry
        for u in range(unroll):
            ni, nkb = advance(i, kb)
            scores(jnp.minimum(ni, nq - 1), nkb, (nq + u + 1) % 2, False)
            consume(i, kb, (nq + u) % 2, False)
            i, kb = ni, nkb
        return i, kb

    n_items = nq * (nq - 1) // 2
    assert unroll % 2 == 0 and n_items % unroll == 0
    lax.fori_loop(0, n_items // unroll, trip, (jnp.int32(1), jnp.int32(0)))

    r = lax.broadcasted_iota(jnp.int32, (LANES, tq), 0)
    for i in range(nq):
        acc_a, acc_b = acc_ref[i, 0], acc_ref[i, 1]
        o_t = jnp.where(r < half, acc_a / acc_a[half:half + 1, :], acc_b / acc_b[0:1, :])
        o_ref[0, i * tq:(i + 1) * tq, :] = o_t.T.astype(o_ref.dtype)


def _flash(qk, vt, kd, *, batch, seq, n_pairs, q_blk, k_blk, sep, tq=512, unroll=2):
    t = qk.shape[1]
    nq = seq // tq
    w = 2 if sep else 1
    kc = 2 * LANES if kd is not None else LANES
    in_specs = [
        pl.BlockSpec((w, seq, LANES), lambda b, hp: (q_blk // w + hp, b, 0)),
        pl.BlockSpec((w, seq, LANES), lambda b, hp: (k_blk // w + hp, b, 0)),
        pl.BlockSpec((1, LANES, seq), lambda b, hp: (hp, 0, b)),
    ]
    args = [qk, qk, vt]
    if kd is not None:
        in_specs.append(pl.BlockSpec((1, seq, LANES), lambda b, hp: (hp, b, 0)))
        args.append(kd)
    return pl.pallas_call(
        functools.partial(_flash_kernel, sep=sep, has_bias=kd is not None, tq=tq, unroll=unroll),
        grid=(batch, n_pairs),
        in_specs=in_specs,
        out_specs=pl.BlockSpec((1, seq, LANES), lambda b, hp: (hp, b, 0)),
        out_shape=jax.ShapeDtypeStruct((n_pairs, t, LANES), BF16),
        scratch_shapes=[
            pltpu.VMEM((2, seq, kc), BF16),
            pltpu.VMEM((2, LANES, seq), BF16),
            pltpu.VMEM((nq, 2, LANES, tq), F32),
            pltpu.VMEM((2, 2, tq, tq), F32),
            pltpu.VMEM((2, 2, 1, tq), F32),
            pltpu.VMEM((nq, 2, 1, tq), F32),
        ],
        compiler_params=_params("parallel", "parallel"),
        name="flash_sep" if sep else "flash_shared",
    )(*args)


def _dil_attn_kernel(*refs):
    n_in = 5 * len(DIL_PATTERNS)
    o_ref, og_ref, lg_ref = refs[n_in:]
    i = pl.program_id(2)
    w = DIL_WINDOW_KEYS
    lane = lax.broadcasted_iota(jnp.int32, (w, LANES), 1)
    qk_first = (lane % 64) < 32
    v_first = lane < 64
    qi = lax.broadcasted_iota(jnp.int32, (w, 2 * w), 0)
    kk = lax.broadcasted_iota(jnp.int32, (w, 2 * w), 1)
    band = jnp.logical_and(kk >= qi, kk <= qi + w)
    band_first = jnp.logical_and(band, kk >= jnp.where(i > 0, 0, w))
    for g, (_, d) in enumerate(DIL_PATTERNS):
        q_ref, k_ref, kp_ref, v_ref, vp_ref = refs[5 * g:5 * g + 5]
        for r in range(d):
            ls = slice(r * LANES, (r + 1) * LANES)
            for sub in range(DIL_TOKENS // d // w):
                q_u = q_ref[0, 0, sub * w:(sub + 1) * w, ls]
                if sub == 0:
                    k_u = jnp.concatenate([kp_ref[0, 0, :, ls], k_ref[0, 0, 0:w, ls]], axis=0)
                    v_u = jnp.concatenate([vp_ref[0, 0, :, ls], v_ref[0, 0, 0:w, ls]], axis=0)
                    msk = band_first
                else:
                    k_u = k_ref[0, 0, (sub - 1) * w:(sub + 1) * w, ls]
                    v_u = v_ref[0, 0, (sub - 1) * w:(sub + 1) * w, ls]
                    msk = band
                zero = jnp.zeros_like(q_u)
                res = []
                for hh in range(2):
                    q_h = jnp.where(qk_first, q_u, zero) if hh == 0 else jnp.where(qk_first, zero, q_u)
                    s = jnp.where(msk, _dot_nt(q_h, k_u), NEG_INF)
                    m = jnp.max(s, axis=-1, keepdims=True)
                    p = jnp.exp(s - m)
                    l = jnp.sum(p, axis=-1, keepdims=True)
                    o = _dot(p.astype(BF16), v_u) / l
                    res.append((o, m + jnp.log(l)))
                tok = pl.ds(sub * w * d + r, w, stride=d) if d > 1 else pl.ds(sub * w, w)
                og_ref[g, tok, :] = jnp.where(v_first, res[0][0], res[1][0])
                lg_ref[g, tok, :] = jnp.where(
                    v_first, jnp.broadcast_to(res[0][1], (w, LANES)), jnp.broadcast_to(res[1][1], (w, LANES)))
    lse = [lg_ref[g] for g in range(len(DIL_PATTERNS))]
    top = functools.reduce(jnp.maximum, lse)
    e = [jnp.exp(x - top) for x in lse]
    num = sum(e[g] * og_ref[g] for g in range(len(DIL_PATTERNS)))
    o_ref[0] = (num / sum(e)).astype(o_ref.dtype)


def _dil_attn(groups, *, batch, seq):
    n_pairs = DIL_HEADS // 2
    n_tok_blk = seq // DIL_TOKENS
    in_specs, args = [], []
    for arr, (_, d) in zip(groups, DIL_PATTERNS):
        rows = DIL_TOKENS // d
        sub_per_blk = rows // DIL_WINDOW_KEYS

        def cur(base, rows=rows, d=d):
            return pl.BlockSpec((1, 1, rows, d * LANES), lambda b, hp, i: (base + hp, b, i, 0))

        def prev(base, spb=sub_per_blk, d=d):
            return pl.BlockSpec((1, 1, DIL_WINDOW_KEYS, d * LANES),
                                lambda b, hp, i: (base + hp, b, jnp.maximum(i * spb - 1, 0), 0))

        in_specs += [cur(0), cur(n_pairs), prev(n_pairs), cur(2 * n_pairs), prev(2 * n_pairs)]
        args += [arr] * 5
    n_g = len(DIL_PATTERNS)
    return pl.pallas_call(
        _dil_attn_kernel,
        grid=(batch, n_pairs, n_tok_blk),
        in_specs=in_specs,
        out_specs=pl.BlockSpec((1, DIL_TOKENS, LANES), lambda b, hp, i: (hp, b * n_tok_blk + i, 0)),
        out_shape=jax.ShapeDtypeStruct((n_pairs, batch * seq, LANES), BF16),
        scratch_shapes=[pltpu.VMEM((n_g, DIL_TOKENS, LANES), F32), pltpu.VMEM((n_g, DIL_TOKENS, LANES), F32)],
        compiler_params=_params("parallel", "parallel", "arbitrary"),
        name="dil_attn",
    )(*args)


def _mlp_kernel(h_ref, o_ref, wo_ref, g_ref, wu_ref, wd_ref, fg_ref, out_ref, xn_ref, acc_ref, *, final_norm):
    f = pl.program_id(1)

    @pl.when(f == 0)
    def _():
        a = jnp.concatenate([o_ref[c] for c in range(o_ref.shape[0])], axis=-1)
        h1 = h_ref[...] + _dot(a, wo_ref[...])
        xn_ref[...] = _rms(h1, g_ref[...]).astype(BF16)
        acc_ref[...] = h1

    u = jnp.maximum(_dot(xn_ref[...], wu_ref[...]), 0.0)
    acc_ref[...] += _dot((u * u).astype(BF16), wd_ref[...])

    @pl.when(f == pl.num_programs(1) - 1)
    def _():
        y = acc_ref[...]
        out_ref[...] = _rms(y, fg_ref[...]) if final_norm else y


def _mlp(h, o, wo, gain, w_up, w_down, final_gain, *, final_norm, tm=1024, tf=1024):
    t, d = h.shape
    dff = w_up.shape[1]
    return pl.pallas_call(
        functools.partial(_mlp_kernel, final_norm=final_norm),
        grid=(t // tm, dff // tf),
        in_specs=[
            pl.BlockSpec((tm, d), lambda i, f: (i, 0)),
            pl.BlockSpec((o.shape[0], tm, LANES), lambda i, f: (0, i, 0)),
            pl.BlockSpec(wo.shape, lambda i, f: (0, 0)),
            pl.BlockSpec((1, d), lambda i, f: (0, 0)),
            pl.BlockSpec((d, tf), lambda i, f: (0, f)),
            pl.BlockSpec((tf, d), lambda i, f: (f, 0)),
            pl.BlockSpec((1, d), lambda i, f: (0, 0)),
        ],
        out_specs=pl.BlockSpec((tm, d), lambda i, f: (i, 0)),
        out_shape=jax.ShapeDtypeStruct((t, d), F32),
        scratch_shapes=[pltpu.VMEM((tm, d), BF16), pltpu.VMEM((tm, d), F32)],
        compiler_params=_params("parallel", "arbitrary"),
        name="mlp_final" if final_norm else "mlp",
    )(h, o, wo, gain.reshape(1, d), w_up, w_down, final_gain.reshape(1, d))


def _rope_halves(seq, dim):
    inv = 1.0 / (ROPE_THETA ** (jnp.arange(0, dim, 2, dtype=F32) / dim))
    ang = jnp.arange(seq, dtype=F32)[:, None] * inv[None, :]
    return jnp.cos(ang), jnp.sin(ang)


def _mla_tables(seq):
    cos, sin = _rope_halves(seq, MLA_ROPE)
    one = lambda n: jnp.ones((seq, n), F32)
    zero = lambda n: jnp.zeros((seq, n), F32)
    cos_t = jnp.concatenate([one(32), cos, one(16), one(32), cos, one(16)], axis=1)
    sin_t = jnp.concatenate([zero(32), -sin, zero(16), zero(32), sin, zero(16)], axis=1)
    return cos_t, sin_t


def _dil_tables(seq, d, tm):
    cos, sin = _rope_halves(seq, DIL_DIM)
    order = lambda x: x.reshape(seq // tm, tm // d, d, LANES).transpose(0, 2, 1, 3).reshape(seq, LANES)
    return (order(jnp.concatenate([cos] * 4, axis=1)),
            order(jnp.concatenate([-sin, -sin, sin, sin], axis=1)))


def _mla_weights(wq_a, wq_b, wkv_a, wkv_b):
    d = wq_a.shape[0]
    z = lambda r, n: jnp.zeros((r, n), F32)
    kpe = wkv_a[:, MLA_KV_RANK:]
    w_a = jnp.concatenate(
        [wq_a, wkv_a[:, :MLA_KV_RANK], z(d, 32), kpe[:, :16], z(d, 48), kpe[:, 16:], z(d, 16)], axis=1)
    scale = (MLA_NOPE + MLA_ROPE) ** -0.5 * LOG2E
    q3 = wq_b.reshape(MLA_Q_RANK, MLA_HEADS, MLA_NOPE + MLA_ROPE) * scale
    zq = jnp.zeros((MLA_Q_RANK, MLA_HEADS, 16), F32)
    wq = jnp.concatenate(
        [q3[..., :32], q3[..., 64:80], zq, q3[..., 32:64], q3[..., 80:96], zq], axis=-1)
    kv3 = wkv_b.reshape(MLA_KV_RANK, MLA_HEADS, MLA_NOPE + MLA_V)
    zk = jnp.zeros((MLA_KV_RANK, MLA_HEADS, 32), F32)
    wk = jnp.concatenate([kv3[..., :32], zk, kv3[..., 32:64], zk], axis=-1)
    wvt = kv3[..., MLA_NOPE:].reshape(MLA_KV_RANK, -1).T
    return (w_a.astype(BF16), wq.reshape(MLA_Q_RANK, -1).astype(BF16),
            wk.reshape(MLA_KV_RANK, -1).astype(BF16), wvt.astype(BF16))


def _dil_weights(w_qkv):
    d = w_qkv.shape[0]
    n_g = len(DIL_PATTERNS)
    n = DIL_HEADS * DIL_DIM
    w = w_qkv.reshape(d, 3, n_g, n)
    qk = w[:, :2].reshape(d, 2, n_g, n // LANES, 2, 2, 32)
    qk = qk * jnp.array([DIL_DIM ** -0.5, 1.0], F32).reshape(1, 2, 1, 1, 1, 1, 1)
    qk = qk.transpose(0, 1, 2, 3, 5, 4, 6).reshape(d, 2, n_g, n)
    w = jnp.concatenate([qk, w[:, 2:]], axis=1).astype(BF16)
    return [w[:, :, g].reshape(d, 3 * n) for g in range(n_g)]


def _mla_attn(h, attn_norm, wq_a, q_norm, wq_b, wkv_a, kv_norm, wkv_b, tables, *, batch, seq):
    w_a, wq, wk, wvt = _mla_weights(wq_a, wq_b, wkv_a, wkv_b)
    cq, ckv, kpe = _mla_a(h, attn_norm, w_a, q_norm, kv_norm, *tables, seq=seq)
    qk, vt = _mla_b(cq, ckv, kpe, wq, wk, wvt, *tables, seq=seq)
    return _flash(qk, vt, None, batch=batch, seq=seq, n_pairs=MLA_HEADS // 2,
                  q_blk=0, k_blk=MLA_HEADS, sep=True)


def _fox_attn(h, attn_norm, w_qkv, w_f, b_f, *, batch, seq):
    n = FOX_HEADS * FOX_DIM
    d = w_qkv.shape[0]
    wqk = jnp.concatenate([w_qkv[:, :n] * (FOX_DIM ** -0.5 * LOG2E), w_qkv[:, n:2 * n]], axis=1)
    wvt = w_qkv[:, 2 * n:].T
    wf = jnp.concatenate([w_f, jnp.zeros((d, LANES - FOX_HEADS), F32)], axis=1)
    bf = jnp.concatenate([b_f.astype(F32), jnp.zeros((LANES - FOX_HEADS,), F32)]).reshape(1, LANES)
    qk, vt, kd = _fox_proj(h, attn_norm, wqk.astype(BF16), wvt.astype(BF16), wf.astype(BF16), bf,
                           batch=batch, seq=seq)
    n_pairs = FOX_HEADS // 2
    return _flash(qk, vt, kd, batch=batch, seq=seq, n_pairs=n_pairs, q_blk=0, k_blk=n_pairs, sep=False)


def _dil_attn_layer(h, attn_norm, w_qkv, *, batch, seq, tm=512):
    groups = []
    for w_g, (_, d) in zip(_dil_weights(w_qkv), DIL_PATTERNS):
        groups.append(_dil_proj(h, attn_norm, w_g, *_dil_tables(seq, d, tm),
                                batch=batch, seq=seq, d=d, tm=tm))
    return _dil_attn(groups, batch=batch, seq=seq)


def kernel(x, l0_attn_norm, l0_mla_wq_a, l0_mla_q_norm, l0_mla_wq_b, l0_mla_wkv_a, l0_mla_kv_norm, l0_mla_wkv_b, l0_mla_wo, l0_mlp_norm, l0_w_up, l0_w_down, l1_attn_norm, l1_fox_w_qkv, l1_fox_w_f, l1_fox_b_f, l1_fox_wo, l1_mlp_norm, l1_w_up, l1_w_down, l2_attn_norm, l2_dil_w_qkv, l2_dil_wo, l2_mlp_norm, l2_w_up, l2_w_down, l3_attn_norm, l3_mla_wq_a, l3_mla_q_norm, l3_mla_wq_b, l3_mla_wkv_a, l3_mla_kv_norm, l3_mla_wkv_b, l3_mla_wo, l3_mlp_norm, l3_w_up, l3_w_down, final_norm):
    batch, seq, d = x.shape
    assert seq % DIL_TOKENS == 0 and (batch * seq) % 1024 == 0
    for window, dil in DIL_PATTERNS:
        assert window // dil == DIL_WINDOW_KEYS
    mla_t = _mla_tables(seq)
    kw = dict(batch=batch, seq=seq)
    bf = lambda w: w.astype(BF16)
    h = x.reshape(batch * seq, d)

    o = _mla_attn(h, l0_attn_norm, l0_mla_wq_a, l0_mla_q_norm, l0_mla_wq_b, l0_mla_wkv_a,
                  l0_mla_kv_norm, l0_mla_wkv_b, mla_t, **kw)
    h = _mlp(h, o, bf(l0_mla_wo), l0_mlp_norm, bf(l0_w_up), bf(l0_w_down), final_norm, final_norm=False)

    o = _fox_attn(h, l1_attn_norm, l1_fox_w_qkv, l1_fox_w_f, l1_fox_b_f, **kw)
    h = _mlp(h, o, bf(l1_fox_wo), l1_mlp_norm, bf(l1_w_up), bf(l1_w_down), final_norm, final_norm=False)

    o = _dil_attn_layer(h, l2_attn_norm, l2_dil_w_qkv, **kw)
    h = _mlp(h, o, bf(l2_dil_wo), l2_mlp_norm, bf(l2_w_up), bf(l2_w_down), final_norm, final_norm=False)

    o = _mla_attn(h, l3_attn_norm, l3_mla_wq_a, l3_mla_q_norm, l3_mla_wq_b, l3_mla_wkv_a,
                  l3_mla_kv_norm, l3_mla_wkv_b, mla_t, **kw)
    h = _mlp(h, o, bf(l3_mla_wo), l3_mlp_norm, bf(l3_w_up), bf(l3_w_down), final_norm, final_norm=True)
    return h.reshape(batch, seq, d)
```

```python
import functools
import math

import numpy as np
import jax
import jax.numpy as jnp
from jax import lax
from jax.experimental import pallas as pl
from jax.experimental.pallas import tpu as pltpu

F32 = jnp.float32
BF16 = jnp.bfloat16

LANES = 128
NORM_EPS = 1e-6
NEG_INF = -1e30
ROPE_THETA = 10000.0
LOG2E = math.log2(math.e)

MLA_HEADS = 16
MLA_Q_RANK = 384
MLA_KV_RANK = 256
MLA_NOPE = 64
MLA_ROPE = 32
MLA_V = 64
FOX_HEADS = 16
FOX_DIM = 64
DIL_PATTERNS = ((128, 1), (512, 4), (2048, 16))
DIL_HEADS = 16
DIL_DIM = 64
DIL_WINDOW_KEYS = 128
DIL_TOKENS = 2048
GATE_PARTS = 3

VMEM_LIMIT = 56 * 2**20


def _params(*sem):
    return pltpu.CompilerParams(dimension_semantics=sem, vmem_limit_bytes=VMEM_LIMIT)


def _rms(x, g):
    ms = jnp.mean(x * x, axis=-1, keepdims=True)
    return x * lax.rsqrt(ms + NORM_EPS) * g


def _dot(a, b):
    return jnp.dot(a, b, preferred_element_type=F32)


def _dot_nt(a, b):
    return lax.dot_general(a, b, (((1,), (1,)), ((), ())), preferred_element_type=F32)


def _rope(blk, cos, sin):
    return blk * cos + pltpu.roll(blk, 64, axis=1) * sin


def _colmax(st):
    n, c = st.shape
    return jnp.max(jnp.max(st.reshape(8, n // 8, c), axis=0), axis=0, keepdims=True)


def _split3(x):
    hi = x.astype(BF16)
    r1 = x - hi.astype(F32)
    mid = r1.astype(BF16)
    lo = (r1 - mid.astype(F32)).astype(BF16)
    return hi, mid, lo


def _dil_proj_kernel(x_ref, g_ref, w_ref, cos_ref, sin_ref, o_ref, xs_ref, xn_ref, *, d, tm):
    rpr = tm // d
    nc = x_ref.shape[1] // LANES
    xn = _rms(x_ref[...], g_ref[...])
    if d == 1:
        xn_ref[...] = xn.astype(BF16)
    else:
        for c in range(nc):
            xs_ref[c] = xn[:, c * LANES:(c + 1) * LANES]
        for c in range(nc):
            for r in range(d):
                xn_ref[r * rpr:(r + 1) * rpr, c * LANES:(c + 1) * LANES] = (
                    xs_ref[c, pl.ds(r, rpr, stride=d), :].astype(BF16))

    nb = o_ref.shape[0] // 4
    lane = lax.broadcasted_iota(jnp.int32, (1, LANES), 1)
    first = jnp.where((lane % 64) < 32, 1.0, 0.0)
    for j in range(3):
        acc = _dot(xn_ref[...], w_ref[:, j * nb * LANES:(j + 1) * nb * LANES])
        for c in range(nb):
            for r in range(d):
                rs = slice(r * rpr, (r + 1) * rpr)
                ls = slice(r * LANES, (r + 1) * LANES)
                blk = acc[rs, c * LANES:(c + 1) * LANES]
                if j < 2:
                    blk = _rope(blk, cos_ref[rs, :], sin_ref[rs, :])
                if j == 0:
                    o_ref[2 * c, 0, :, ls] = (blk * first).astype(o_ref.dtype)
                    o_ref[2 * c + 1, 0, :, ls] = (blk * (1.0 - first)).astype(o_ref.dtype)
                else:
                    o_ref[(j + 1) * nb + c, 0, :, ls] = blk.astype(o_ref.dtype)


def _dil_proj(h, gain, w, cos_t, sin_t, *, batch, seq, d, tm):
    t, dm = h.shape
    ns = seq // tm
    nb = 4 * w.shape[1] // (3 * LANES)
    return pl.pallas_call(
        functools.partial(_dil_proj_kernel, d=d, tm=tm),
        grid=(batch, ns),
        in_specs=[
            pl.BlockSpec((tm, dm), lambda b, i: (b * ns + i, 0)),
            pl.BlockSpec((1, dm), lambda b, i: (0, 0)),
            pl.BlockSpec(w.shape, lambda b, i: (0, 0)),
            pl.BlockSpec((tm, LANES), lambda b, i: (i, 0)),
            pl.BlockSpec((tm, LANES), lambda b, i: (i, 0)),
        ],
        out_specs=pl.BlockSpec((nb, 1, tm // d, d * LANES), lambda b, i: (0, b, i, 0)),
        out_shape=jax.ShapeDtypeStruct((nb, batch, seq // d, d * LANES), BF16),
        scratch_shapes=[pltpu.VMEM((dm // LANES, tm, LANES), F32), pltpu.VMEM((tm, dm), BF16)],
        compiler_params=_params("parallel", "parallel"),
        name="dil_proj_d%d" % d,
    )(h, gain.reshape(1, dm), w, cos_t, sin_t)


def _mla_a_kernel(x_ref, g_ref, w_ref, qn_ref, kvn_ref, cos_ref, sin_ref, cq_ref, ckv_ref, kpe_ref):
    xn = _rms(x_ref[...], g_ref[...]).astype(BF16)
    y = _dot(xn, w_ref[...])
    cq_ref[...] = _rms(y[:, :MLA_Q_RANK], qn_ref[...]).astype(BF16)
    ckv_ref[...] = _rms(y[:, MLA_Q_RANK:MLA_Q_RANK + MLA_KV_RANK], kvn_ref[...]).astype(BF16)
    kpe_ref[...] = _rope(y[:, MLA_Q_RANK + MLA_KV_RANK:], cos_ref[...], sin_ref[...]).astype(BF16)


def _mla_a(h, gain, w_a, q_norm, kv_norm, cos_t, sin_t, *, seq, tm=1024):
    t, d = h.shape
    n = w_a.shape[1]
    ns = seq // tm
    row = lambda i: (i, 0)
    fixed = lambda i: (0, 0)
    tab = lambda i: (i % ns, 0)
    return pl.pallas_call(
        _mla_a_kernel,
        grid=(t // tm,),
        in_specs=[
            pl.BlockSpec((tm, d), row),
            pl.BlockSpec((1, d), fixed),
            pl.BlockSpec((d, n), fixed),
            pl.BlockSpec((1, MLA_Q_RANK), fixed),
            pl.BlockSpec((1, MLA_KV_RANK), fixed),
            pl.BlockSpec((tm, LANES), tab),
            pl.BlockSpec((tm, LANES), tab),
        ],
        out_specs=[
            pl.BlockSpec((tm, MLA_Q_RANK), row),
            pl.BlockSpec((tm, MLA_KV_RANK), row),
            pl.BlockSpec((tm, LANES), row),
        ],
        out_shape=[
            jax.ShapeDtypeStruct((t, MLA_Q_RANK), BF16),
            jax.ShapeDtypeStruct((t, MLA_KV_RANK), BF16),
            jax.ShapeDtypeStruct((t, LANES), BF16),
        ],
        compiler_params=_params("parallel"),
        name="mla_a",
    )(h, gain.reshape(1, d), w_a, q_norm.reshape(1, -1), kv_norm.reshape(1, -1), cos_t, sin_t)


def _mla_b_kernel(cq_ref, ckv_ref, kpe_ref, wq_ref, wk_ref, wvt_ref, cos_ref, sin_ref, qk_ref, vt_ref):
    q = _dot(cq_ref[...], wq_ref[...])
    for c in range(MLA_HEADS):
        blk = q[:, c * LANES:(c + 1) * LANES]
        qk_ref[c] = _rope(blk, cos_ref[...], sin_ref[...]).astype(BF16)
    k = _dot(ckv_ref[...], wk_ref[...])
    kpe = kpe_ref[...].astype(F32)
    for c in range(MLA_HEADS):
        qk_ref[MLA_HEADS + c] = (k[:, c * LANES:(c + 1) * LANES] + kpe).astype(BF16)
    vt = _dot_nt(wvt_ref[...], ckv_ref[...])
    for c in range(MLA_HEADS // 2):
        vt_ref[c] = vt[c * LANES:(c + 1) * LANES, :].astype(BF16)


def _mla_b(cq, ckv, kpe, wq, wk, wvt, cos_t, sin_t, *, seq, tm=512):
    t = cq.shape[0]
    ns = seq // tm
    row = lambda i: (i, 0)
    fixed = lambda i: (0, 0)
    tab = lambda i: (i % ns, 0)
    return pl.pallas_call(
        _mla_b_kernel,
        grid=(t // tm,),
        in_specs=[
            pl.BlockSpec((tm, MLA_Q_RANK), row),
            pl.BlockSpec((tm, MLA_KV_RANK), row),
            pl.BlockSpec((tm, LANES), row),
            pl.BlockSpec(wq.shape, fixed),
            pl.BlockSpec(wk.shape, fixed),
            pl.BlockSpec(wvt.shape, fixed),
            pl.BlockSpec((tm, LANES), tab),
            pl.BlockSpec((tm, LANES), tab),
        ],
        out_specs=[
            pl.BlockSpec((2 * MLA_HEADS, tm, LANES), lambda i: (0, i, 0)),
            pl.BlockSpec((MLA_HEADS // 2, LANES, tm), lambda i: (0, 0, i)),
        ],
        out_shape=[
            jax.ShapeDtypeStruct((2 * MLA_HEADS, t, LANES), BF16),
            jax.ShapeDtypeStruct((MLA_HEADS // 2, LANES, t), BF16),
        ],
        compiler_params=_params("parallel"),
        name="mla_b",
    )(cq, ckv, kpe, wq, wk, wvt, cos_t, sin_t)


def _fox_proj_kernel(x_ref, g_ref, wqk_ref, wvt_ref, wf_ref, bf_ref, tri_ref, scat_ref,
                     qk_ref, vt_ref, kd_ref, carry_ref, *, tm):
    @pl.when(pl.program_id(1) == 0)
    def _():
        carry_ref[...] = jnp.zeros_like(carry_ref)

    xn = _rms(x_ref[...], g_ref[...]).astype(BF16)
    qk = _dot(xn, wqk_ref[...])
    for c in range(qk_ref.shape[0]):
        qk_ref[c] = qk[:, c * LANES:(c + 1) * LANES].astype(BF16)
    vt = _dot_nt(wvt_ref[...], xn)
    for c in range(vt_ref.shape[0]):
        vt_ref[c] = vt[c * LANES:(c + 1) * LANES, :].astype(BF16)

    z = _dot(xn, wf_ref[...]) + bf_ref[...]
    logf = jnp.minimum(z, 0.0) - jnp.log(1.0 + jnp.exp(-jnp.abs(z)))
    tri = tri_ref[...]
    cum = carry_ref[...] + sum(_dot(tri, part) for part in _split3(logf))
    carry_ref[...] = cum[tm - 1:tm, :]
    kd = sum(_dot(part, scat_ref[p]) for p, part in enumerate(_split3(cum * (-LOG2E))))
    for c in range(kd_ref.shape[0]):
        kd_ref[c] = kd[:, c * LANES:(c + 1) * LANES].astype(BF16)


def _fox_proj(h, gain, wqk, wvt, wf, bf, *, batch, seq, tm=512):
    t, d = h.shape
    ns = seq // tm
    n_pairs = FOX_HEADS // 2
    tri = jnp.tril(jnp.ones((tm, tm), BF16))
    scat = np.zeros((GATE_PARTS, LANES, n_pairs * LANES), np.float32)
    for p in range(GATE_PARTS):
        for hd in range(FOX_HEADS):
            scat[p, hd, (hd // 2) * LANES + GATE_PARTS * (hd % 2) + p] = 1.0
    fixed2 = lambda b, i: (0, 0)
    rows = lambda b, i: (0, b * ns + i, 0)
    return pl.pallas_call(
        functools.partial(_fox_proj_kernel, tm=tm),
        grid=(batch, ns),
        in_specs=[
            pl.BlockSpec((tm, d), lambda b, i: (b * ns + i, 0)),
            pl.BlockSpec((1, d), fixed2),
            pl.BlockSpec(wqk.shape, fixed2),
            pl.BlockSpec(wvt.shape, fixed2),
            pl.BlockSpec(wf.shape, fixed2),
            pl.BlockSpec((1, LANES), fixed2),
            pl.BlockSpec((tm, tm), fixed2),
            pl.BlockSpec(scat.shape, lambda b, i: (0, 0, 0)),
        ],
        out_specs=[
            pl.BlockSpec((2 * n_pairs, tm, LANES), rows),
            pl.BlockSpec((n_pairs, LANES, tm), lambda b, i: (0, 0, b * ns + i)),
            pl.BlockSpec((n_pairs, tm, LANES), rows),
        ],
        out_shape=[
            jax.ShapeDtypeStruct((2 * n_pairs, t, LANES), BF16),
            jax.ShapeDtypeStruct((n_pairs, LANES, t), BF16),
            jax.ShapeDtypeStruct((n_pairs, t, LANES), BF16),
        ],
        scratch_shapes=[pltpu.VMEM((1, LANES), F32)],
        compiler_params=_params("arbitrary", "arbitrary"),
        name="fox_proj",
    )(h, gain.reshape(1, d), wqk, wvt, wf, bf, tri, jnp.asarray(scat, BF16))


def _flash_kernel(*refs, sep, has_bias, tq, unroll):
    if has_bias:
        q_ref, k_ref, vt_ref, kd_ref, o_ref, qs_ref, vta_ref, acc_ref, s_ref, mx_ref, m_ref = refs
    else:
        q_ref, k_ref, vt_ref, o_ref, qs_ref, vta_ref, acc_ref, s_ref, mx_ref, m_ref = refs
        kd_ref = None
    half = LANES // 2
    nq = acc_ref.shape[0]
    seq = nq * tq

    vt = vt_ref[0]
    r = lax.broadcasted_iota(jnp.int32, vt.shape, 0)
    one = jnp.ones_like(vt)
    vta_ref[0] = jnp.where(r < half, vt, one)
    vta_ref[1] = jnp.where(r < half, one, vt)
    if not sep or has_bias:
        lane = lax.broadcasted_iota(jnp.int32, (seq, LANES), 1)
        for hh in range(2):
            if sep:
                q_h = q_ref[hh]
            else:
                q = q_ref[0]
                own = (lane < half) if hh == 0 else (lane >= half)
                q_h = jnp.where(own, q, jnp.zeros_like(q))
            qs_ref[hh, :, 0:LANES] = q_h
            if has_bias:
                sel = jnp.logical_and(lane >= GATE_PARTS * hh, lane < GATE_PARTS * (hh + 1))
                qs_ref[hh, :, LANES:2 * LANES] = jnp.where(sel, 1.0, 0.0).astype(BF16)
    key = lax.broadcasted_iota(jnp.int32, (tq, tq), 0)
    qry = lax.broadcasted_iota(jnp.int32, (tq, tq), 1)
    src = qs_ref if (not sep or has_bias) else q_ref

    def scores(i, kb, slot, diag):
        qstart, kstart = i * tq, kb * tq
        if not diag:
            qstart, kstart = pl.multiple_of(qstart, tq), pl.multiple_of(kstart, tq)
        for hh in range(2):
            k_blk = k_ref[hh if sep else 0, pl.ds(kstart, tq), :]
            if has_bias:
                k_blk = jnp.concatenate([k_blk, kd_ref[0, pl.ds(kstart, tq), :]], axis=1)
            st = _dot_nt(k_blk, src[hh, pl.ds(qstart, tq), :])
            if diag:
                st = jnp.where(key <= qry, st, NEG_INF)
            s_ref[slot, hh] = st
            mx_ref[slot, hh] = _colmax(st)

    def consume(i, kb, slot, first):
        kstart = kb * tq if first else pl.multiple_of(kb * tq, tq)
        for hh in range(2):
            m_new = mx_ref[slot, hh]
            if not first:
                m_old = m_ref[i, hh]
                m_new = jnp.maximum(m_old, m_new)
            m_ref[i, hh] = m_new
            pt = jnp.exp2(s_ref[slot, hh] - m_new).astype(BF16)
            pv = _dot(vta_ref[hh, :, pl.ds(kstart, tq)], pt)
            acc_ref[i, hh] = pv if first else jnp.exp2(m_old - m_new) * acc_ref[i, hh] + pv

    scores(0, 0, 0, True)
    for i in range(nq):
        if i + 1 < nq:
            scores(i + 1, i + 1, (i + 1) % 2, True)
        elif nq > 1:
            scores(1, 0, (i + 1) % 2, False)
        consume(i, i, i % 2, True)

    def advance(i, kb):
        last = kb + 1 == i
        return jnp.where(last, i + 1, i), jnp.where(last, 0, kb + 1)

    def trip(_, carry):
        i, kb = carry
        for u in range(unroll):
            ni, nkb = advance(i, kb)
            scores(jnp.minimum(ni, nq - 1), nkb, (nq + u + 1) % 2, False)
            consume(i, kb, (nq + u) % 2, False)
            i, kb = ni, nkb
        return i, kb

    n_items = nq * (nq - 1) // 2
    assert unroll % 2 == 0 and n_items % unroll == 0
    lax.fori_loop(0, n_items // unroll, trip, (jnp.int32(1), jnp.int32(0)))

    r = lax.broadcasted_iota(jnp.int32, (LANES, tq), 0)
    for i in range(nq):
        acc_a, acc_b = acc_ref[i, 0], acc_ref[i, 1]
        o_t = jnp.where(r < half, acc_a / acc_a[half:half + 1, :], acc_b / acc_b[0:1, :])
        o_ref[0, i * tq:(i + 1) * tq, :] = o_t.T.astype(o_ref.dtype)


def _flash(qk, vt, kd, *, batch, seq, n_pairs, q_blk, k_blk, sep, tq=512, unroll=2):
    t = qk.shape[1]
    nq = seq // tq
    w = 2 if sep else 1
    kc = 2 * LANES if kd is not None else LANES
    in_specs = [
        pl.BlockSpec((w, seq, LANES), lambda b, hp: (q_blk // w + hp, b, 0)),
        pl.BlockSpec((w, seq, LANES), lambda b, hp: (k_blk // w + hp, b, 0)),
        pl.BlockSpec((1, LANES, seq), lambda b, hp: (hp, 0, b)),
    ]
    args = [qk, qk, vt]
    if kd is not None:
        in_specs.append(pl.BlockSpec((1, seq, LANES), lambda b, hp: (hp, b, 0)))
        args.append(kd)
    return pl.pallas_call(
        functools.partial(_flash_kernel, sep=sep, has_bias=kd is not None, tq=tq, unroll=unroll),
        grid=(batch, n_pairs),
        in_specs=in_specs,
        out_specs=pl.BlockSpec((1, seq, LANES), lambda b, hp: (hp, b, 0)),
        out_shape=jax.ShapeDtypeStruct((n_pairs, t, LANES), BF16),
        scratch_shapes=[
            pltpu.VMEM((2, seq, kc), BF16),
            pltpu.VMEM((2, LANES, seq), BF16),
            pltpu.VMEM((nq, 2, LANES, tq), F32),
            pltpu.VMEM((2, 2, tq, tq), F32),
            pltpu.VMEM((2, 2, 1, tq), F32),
            pltpu.VMEM((nq, 2, 1, tq), F32),
        ],
        compiler_params=_params("parallel", "parallel"),
        name="flash_sep" if sep else "flash_shared",
    )(*args)


def _dil_attn_kernel(*refs):
    n_in = 5 * len(DIL_PATTERNS)
    o_ref, og_ref, lg_ref, band_ref = refs[n_in:]
    i = pl.program_id(2)
    w = DIL_WINDOW_KEYS
    v_first = lax.broadcasted_iota(jnp.int32, (w, LANES), 1) < 64
    qi = lax.broadcasted_iota(jnp.int32, (2 * w, 2 * w), 0) % w
    kk = lax.broadcasted_iota(jnp.int32, (2 * w, 2 * w), 1)
    band = jnp.logical_and(kk >= qi, kk <= qi + w)
    band_ref[0] = jnp.where(band, 0.0, NEG_INF)
    band_ref[1] = jnp.where(jnp.logical_and(band, kk >= jnp.where(i > 0, 0, w)), 0.0, NEG_INF)
    ones = jnp.ones((2 * w, LANES), BF16)
    for g, (_, d) in enumerate(DIL_PATTERNS):
        q_ref, k_ref, kp_ref, v_ref, vp_ref = refs[5 * g:5 * g + 5]
        for r in range(d):
            ls = slice(r * LANES, (r + 1) * LANES)
            for sub in range(DIL_TOKENS // d // w):
                rows = slice(sub * w, (sub + 1) * w)
                q_u = jnp.concatenate([q_ref[0, 0, rows, ls], q_ref[1, 0, rows, ls]], axis=0)
                if sub == 0:
                    k_u = jnp.concatenate([kp_ref[0, 0, :, ls], k_ref[0, 0, 0:w, ls]], axis=0)
                    v_u = jnp.concatenate([vp_ref[0, 0, :, ls], v_ref[0, 0, 0:w, ls]], axis=0)
                else:
                    k_u = k_ref[0, 0, (sub - 1) * w:(sub + 1) * w, ls]
                    v_u = v_ref[0, 0, (sub - 1) * w:(sub + 1) * w, ls]
                s = _dot_nt(q_u, k_u) + band_ref[1 if sub == 0 else 0]
                m = jnp.max(s, axis=-1, keepdims=True)
                p = jnp.exp2(s - m).astype(BF16)
                ov = _dot(p, jnp.concatenate([v_u, ones], axis=1))
                num = jnp.where(v_first, ov[0:w, 0:LANES], ov[w:2 * w, 0:LANES])
                den = jnp.where(v_first, ov[0:w, LANES:2 * LANES], ov[w:2 * w, LANES:2 * LANES])
                top = jnp.where(v_first, jnp.broadcast_to(m[0:w], (w, LANES)),
                                jnp.broadcast_to(m[w:2 * w], (w, LANES)))
                tok = pl.ds(sub * w * d + r, w, stride=d) if d > 1 else pl.ds(sub * w, w)
                og_ref[g, tok, :] = num / den
                lg_ref[g, tok, :] = top + jnp.log(den) * LOG2E
    lse = [lg_ref[g] for g in range(len(DIL_PATTERNS))]
    top = functools.reduce(jnp.maximum, lse)
    e = [jnp.exp2(x - top) for x in lse]
    num = sum(e[g] * og_ref[g] for g in range(len(DIL_PATTERNS)))
    o_ref[0] = (num / sum(e)).astype(o_ref.dtype)


def _dil_attn(groups, *, batch, seq):
    n_pairs = DIL_HEADS // 2
    n_tok_blk = seq // DIL_TOKENS
    in_specs, args = [], []
    for arr, (_, d) in zip(groups, DIL_PATTERNS):
        rows = DIL_TOKENS // d
        sub_per_blk = rows // DIL_WINDOW_KEYS

        def cur(base, rows=rows, d=d):
            return pl.BlockSpec((1, 1, rows, d * LANES), lambda b, hp, i: (base + hp, b, i, 0))

        def prev(base, spb=sub_per_blk, d=d):
            return pl.BlockSpec((1, 1, DIL_WINDOW_KEYS, d * LANES),
                                lambda b, hp, i: (base + hp, b, jnp.maximum(i * spb - 1, 0), 0))

        q_spec = pl.BlockSpec((2, 1, rows, d * LANES), lambda b, hp, i: (hp, b, i, 0))
        in_specs += [q_spec, cur(2 * n_pairs), prev(2 * n_pairs), cur(3 * n_pairs), prev(3 * n_pairs)]
        args += [arr] * 5
    n_g = len(DIL_PATTERNS)
    w2 = 2 * DIL_WINDOW_KEYS
    return pl.pallas_call(
        _dil_attn_kernel,
        grid=(batch, n_pairs, n_tok_blk),
        in_specs=in_specs,
        out_specs=pl.BlockSpec((1, DIL_TOKENS, LANES), lambda b, hp, i: (hp, b * n_tok_blk + i, 0)),
        out_shape=jax.ShapeDtypeStruct((n_pairs, batch * seq, LANES), BF16),
        scratch_shapes=[pltpu.VMEM((n_g, DIL_TOKENS, LANES), F32), pltpu.VMEM((n_g, DIL_TOKENS, LANES), F32),
                        pltpu.VMEM((2, w2, w2), F32)],
        compiler_params=_params("parallel", "parallel", "arbitrary"),
        name="dil_attn",
    )(*args)


def _mlp_kernel(h_ref, o_ref, wo_ref, g_ref, wu_ref, wd_ref, fg_ref, out_ref, xn_ref, acc_ref, *, final_norm):
    f = pl.program_id(1)

    @pl.when(f == 0)
    def _():
        a = jnp.concatenate([o_ref[c] for c in range(o_ref.shape[0])], axis=-1)
        h1 = h_ref[...] + _dot(a, wo_ref[...])
        xn_ref[...] = _rms(h1, g_ref[...]).astype(BF16)
        acc_ref[...] = h1

    u = jnp.maximum(_dot(xn_ref[...], wu_ref[...]), 0.0)
    acc_ref[...] += _dot((u * u).astype(BF16), wd_ref[...])

    @pl.when(f == pl.num_programs(1) - 1)
    def _():
        y = acc_ref[...]
        out_ref[...] = _rms(y, fg_ref[...]) if final_norm else y


def _mlp(h, o, wo, gain, w_up, w_down, final_gain, *, final_norm, tm=1024, tf=1024):
    t, d = h.shape
    dff = w_up.shape[1]
    return pl.pallas_call(
        functools.partial(_mlp_kernel, final_norm=final_norm),
        grid=(t // tm, dff // tf),
        in_specs=[
            pl.BlockSpec((tm, d), lambda i, f: (i, 0)),
            pl.BlockSpec((o.shape[0], tm, LANES), lambda i, f: (0, i, 0)),
            pl.BlockSpec(wo.shape, lambda i, f: (0, 0)),
            pl.BlockSpec((1, d), lambda i, f: (0, 0)),
            pl.BlockSpec((d, tf), lambda i, f: (0, f)),
            pl.BlockSpec((tf, d), lambda i, f: (f, 0)),
            pl.BlockSpec((1, d), lambda i, f: (0, 0)),
        ],
        out_specs=pl.BlockSpec((tm, d), lambda i, f: (i, 0)),
        out_shape=jax.ShapeDtypeStruct((t, d), F32),
        scratch_shapes=[pltpu.VMEM((tm, d), BF16), pltpu.VMEM((tm, d), F32)],
        compiler_params=_params("parallel", "arbitrary"),
        name="mlp_final" if final_norm else "mlp",
    )(h, o, wo, gain.reshape(1, d), w_up, w_down, final_gain.reshape(1, d))


def _rope_halves(seq, dim):
    inv = 1.0 / (ROPE_THETA ** (jnp.arange(0, dim, 2, dtype=F32) / dim))
    ang = jnp.arange(seq, dtype=F32)[:, None] * inv[None, :]
    return jnp.cos(ang), jnp.sin(ang)


def _mla_tables(seq):
    cos, sin = _rope_halves(seq, MLA_ROPE)
    one = lambda n: jnp.ones((seq, n), F32)
    zero = lambda n: jnp.zeros((seq, n), F32)
    cos_t = jnp.concatenate([one(32), cos, one(16), one(32), cos, one(16)], axis=1)
    sin_t = jnp.concatenate([zero(32), -sin, zero(16), zero(32), sin, zero(16)], axis=1)
    return cos_t, sin_t


def _dil_tables(seq, d, tm):
    cos, sin = _rope_halves(seq, DIL_DIM)
    order = lambda x: x.reshape(seq // tm, tm // d, d, LANES).transpose(0, 2, 1, 3).reshape(seq, LANES)
    return (order(jnp.concatenate([cos] * 4, axis=1)),
            order(jnp.concatenate([-sin, -sin, sin, sin], axis=1)))


def _mla_weights(wq_a, wq_b, wkv_a, wkv_b):
    d = wq_a.shape[0]
    z = lambda r, n: jnp.zeros((r, n), F32)
    kpe = wkv_a[:, MLA_KV_RANK:]
    w_a = jnp.concatenate(
        [wq_a, wkv_a[:, :MLA_KV_RANK], z(d, 32), kpe[:, :16], z(d, 48), kpe[:, 16:], z(d, 16)], axis=1)
    scale = (MLA_NOPE + MLA_ROPE) ** -0.5 * LOG2E
    q3 = wq_b.reshape(MLA_Q_RANK, MLA_HEADS, MLA_NOPE + MLA_ROPE) * scale
    zq = jnp.zeros((MLA_Q_RANK, MLA_HEADS, 16), F32)
    wq = jnp.concatenate(
        [q3[..., :32], q3[..., 64:80], zq, q3[..., 32:64], q3[..., 80:96], zq], axis=-1)
    kv3 = wkv_b.reshape(MLA_KV_RANK, MLA_HEADS, MLA_NOPE + MLA_V)
    zk = jnp.zeros((MLA_KV_RANK, MLA_HEADS, 32), F32)
    wk = jnp.concatenate([kv3[..., :32], zk, kv3[..., 32:64], zk], axis=-1)
    wvt = kv3[..., MLA_NOPE:].reshape(MLA_KV_RANK, -1).T
    return (w_a.astype(BF16), wq.reshape(MLA_Q_RANK, -1).astype(BF16),
            wk.reshape(MLA_KV_RANK, -1).astype(BF16), wvt.astype(BF16))


def _dil_weights(w_qkv):
    d = w_qkv.shape[0]
    n_g = len(DIL_PATTERNS)
    n = DIL_HEADS * DIL_DIM
    w = w_qkv.reshape(d, 3, n_g, n)
    qk = w[:, :2].reshape(d, 2, n_g, n // LANES, 2, 2, 32)
    qk = qk * jnp.array([DIL_DIM ** -0.5 * LOG2E, 1.0], F32).reshape(1, 2, 1, 1, 1, 1, 1)
    qk = qk.transpose(0, 1, 2, 3, 5, 4, 6).reshape(d, 2, n_g, n)
    w = jnp.concatenate([qk, w[:, 2:]], axis=1).astype(BF16)
    return [w[:, :, g].reshape(d, 3 * n) for g in range(n_g)]


def _mla_attn(h, attn_norm, wq_a, q_norm, wq_b, wkv_a, kv_norm, wkv_b, tables, *, batch, seq):
    w_a, wq, wk, wvt = _mla_weights(wq_a, wq_b, wkv_a, wkv_b)
    cq, ckv, kpe = _mla_a(h, attn_norm, w_a, q_norm, kv_norm, *tables, seq=seq)
    qk, vt = _mla_b(cq, ckv, kpe, wq, wk, wvt, *tables, seq=seq)
    return _flash(qk, vt, None, batch=batch, seq=seq, n_pairs=MLA_HEADS // 2,
                  q_blk=0, k_blk=MLA_HEADS, sep=True)


def _fox_attn(h, attn_norm, w_qkv, w_f, b_f, *, batch, seq):
    n = FOX_HEADS * FOX_DIM
    d = w_qkv.shape[0]
    wqk = jnp.concatenate([w_qkv[:, :n] * (FOX_DIM ** -0.5 * LOG2E), w_qkv[:, n:2 * n]], axis=1)
    wvt = w_qkv[:, 2 * n:].T
    wf = jnp.concatenate([w_f, jnp.zeros((d, LANES - FOX_HEADS), F32)], axis=1)
    bf = jnp.concatenate([b_f.astype(F32), jnp.zeros((LANES - FOX_HEADS,), F32)]).reshape(1, LANES)
    qk, vt, kd = _fox_proj(h, attn_norm, wqk.astype(BF16), wvt.astype(BF16), wf.astype(BF16), bf,
                           batch=batch, seq=seq)
    n_pairs = FOX_HEADS // 2
    return _flash(qk, vt, kd, batch=batch, seq=seq, n_pairs=n_pairs, q_blk=0, k_blk=n_pairs, sep=False)


def _dil_attn_layer(h, attn_norm, w_qkv, *, batch, seq, tm=512):
    groups = []
    for w_g, (_, d) in zip(_dil_weights(w_qkv), DIL_PATTERNS):
        groups.append(_dil_proj(h, attn_norm, w_g, *_dil_tables(seq, d, tm),
                                batch=batch, seq=seq, d=d, tm=tm))
    return _dil_attn(groups, batch=batch, seq=seq)


def kernel(x, l0_attn_norm, l0_mla_wq_a, l0_mla_q_norm, l0_mla_wq_b, l0_mla_wkv_a, l0_mla_kv_norm, l0_mla_wkv_b, l0_mla_wo, l0_mlp_norm, l0_w_up, l0_w_down, l1_attn_norm, l1_fox_w_qkv, l1_fox_w_f, l1_fox_b_f, l1_fox_wo, l1_mlp_norm, l1_w_up, l1_w_down, l2_attn_norm, l2_dil_w_qkv, l2_dil_wo, l2_mlp_norm, l2_w_up, l2_w_down, l3_attn_norm, l3_mla_wq_a, l3_mla_q_norm, l3_mla_wq_b, l3_mla_wkv_a, l3_mla_kv_norm, l3_mla_wkv_b, l3_mla_wo, l3_mlp_norm, l3_w_up, l3_w_down, final_norm):
    batch, seq, d = x.shape
    assert seq % DIL_TOKENS == 0 and (batch * seq) % 1024 == 0
    for window, dil in DIL_PATTERNS:
        assert window // dil == DIL_WINDOW_KEYS
    mla_t = _mla_tables(seq)
    kw = dict(batch=batch, seq=seq)
    bf = lambda w: w.astype(BF16)
    h = x.reshape(batch * seq, d)

    o = _mla_attn(h, l0_attn_norm, l0_mla_wq_a, l0_mla_q_norm, l0_mla_wq_b, l0_mla_wkv_a,
                  l0_mla_kv_norm, l0_mla_wkv_b, mla_t, **kw)
    h = _mlp(h, o, bf(l0_mla_wo), l0_mlp_norm, bf(l0_w_up), bf(l0_w_down), final_norm, final_norm=False)

    o = _fox_attn(h, l1_attn_norm, l1_fox_w_qkv, l1_fox_w_f, l1_fox_b_f, **kw)
    h = _mlp(h, o, bf(l1_fox_wo), l1_mlp_norm, bf(l1_w_up), bf(l1_w_down), final_norm, final_norm=False)

    o = _dil_attn_layer(h, l2_attn_norm, l2_dil_w_qkv, **kw)
    h = _mlp(h, o, bf(l2_dil_wo), l2_mlp_norm, bf(l2_w_up), bf(l2_w_down), final_norm, final_norm=False)

    o = _mla_attn(h, l3_attn_norm, l3_mla_wq_a, l3_mla_q_norm, l3_mla_wq_b, l3_mla_wkv_a,
                  l3_mla_kv_norm, l3_mla_wkv_b, mla_t, **kw)
    h = _mlp(h, o, bf(l3_mla_wo), l3_mlp_norm, bf(l3_w_up), bf(l3_w_down), final_norm, final_norm=True)
    return h.reshape(batch, seq, d)
```

```python
import functools
import math

import numpy as np
import jax
import jax.numpy as jnp
from jax import lax
from jax.experimental import pallas as pl
from jax.experimental.pallas import tpu as pltpu

F32 = jnp.float32
BF16 = jnp.bfloat16

LANES = 128
NORM_EPS = 1e-6
NEG_INF = -1e30
ROPE_THETA = 10000.0
LOG2E = math.log2(math.e)

MLA_HEADS = 16
MLA_Q_RANK = 384
MLA_KV_RANK = 256
MLA_NOPE = 64
MLA_ROPE = 32
MLA_V = 64
FOX_HEADS = 16
FOX_DIM = 64
DIL_PATTERNS = ((128, 1), (512, 4), (2048, 16))
DIL_HEADS = 16
DIL_DIM = 64
DIL_WINDOW_KEYS = 128
DIL_TOKENS = 2048
GATE_PARTS = 3

VMEM_LIMIT = 56 * 2**20


def _params(*sem):
    return pltpu.CompilerParams(dimension_semantics=sem, vmem_limit_bytes=VMEM_LIMIT)


def _rms(x, g):
    ms = jnp.mean(x * x, axis=-1, keepdims=True)
    return x * lax.rsqrt(ms + NORM_EPS) * g


def _dot(a, b):
    return jnp.dot(a, b, preferred_element_type=F32)


def _dot_nt(a, b):
    return lax.dot_general(a, b, (((1,), (1,)), ((), ())), preferred_element_type=F32)


def _rope(blk, cos, sin):
    return blk * cos + pltpu.roll(blk, 64, axis=1) * sin


def _colmax(st):
    n, c = st.shape
    return jnp.max(jnp.max(st.reshape(8, n // 8, c), axis=0), axis=0, keepdims=True)


def _split3(x):
    hi = x.astype(BF16)
    r1 = x - hi.astype(F32)
    mid = r1.astype(BF16)
    lo = (r1 - mid.astype(F32)).astype(BF16)
    return hi, mid, lo


def _dil_proj_kernel(x_ref, g_ref, w_ref, cos_ref, sin_ref, o_ref, xs_ref, xn_ref, *, d, tm):
    rpr = tm // d
    nc = x_ref.shape[1] // LANES
    xn = _rms(x_ref[...], g_ref[...])
    if d == 1:
        xn_ref[...] = xn.astype(BF16)
    else:
        for c in range(nc):
            xs_ref[c] = xn[:, c * LANES:(c + 1) * LANES]
        for c in range(nc):
            for r in range(d):
                xn_ref[r * rpr:(r + 1) * rpr, c * LANES:(c + 1) * LANES] = (
                    xs_ref[c, pl.ds(r, rpr, stride=d), :].astype(BF16))

    nb = o_ref.shape[0] // 4
    lane = lax.broadcasted_iota(jnp.int32, (1, LANES), 1)
    first = jnp.where((lane % 64) < 32, 1.0, 0.0)
    for j in range(3):
        acc = _dot(xn_ref[...], w_ref[:, j * nb * LANES:(j + 1) * nb * LANES])
        for c in range(nb):
            for r in range(d):
                rs = slice(r * rpr, (r + 1) * rpr)
                ls = slice(r * LANES, (r + 1) * LANES)
                blk = acc[rs, c * LANES:(c + 1) * LANES]
                if j < 2:
                    blk = _rope(blk, cos_ref[rs, :], sin_ref[rs, :])
                if j == 0:
                    o_ref[2 * c, 0, :, ls] = (blk * first).astype(o_ref.dtype)
                    o_ref[2 * c + 1, 0, :, ls] = (blk * (1.0 - first)).astype(o_ref.dtype)
                else:
                    o_ref[(j + 1) * nb + c, 0, :, ls] = blk.astype(o_ref.dtype)


def _dil_proj(h, gain, w, cos_t, sin_t, *, batch, seq, d, tm):
    t, dm = h.shape
    ns = seq // tm
    nb = 4 * w.shape[1] // (3 * LANES)
    return pl.pallas_call(
        functools.partial(_dil_proj_kernel, d=d, tm=tm),
        grid=(batch, ns),
        in_specs=[
            pl.BlockSpec((tm, dm), lambda b, i: (b * ns + i, 0)),
            pl.BlockSpec((1, dm), lambda b, i: (0, 0)),
            pl.BlockSpec(w.shape, lambda b, i: (0, 0)),
            pl.BlockSpec((tm, LANES), lambda b, i: (i, 0)),
            pl.BlockSpec((tm, LANES), lambda b, i: (i, 0)),
        ],
        out_specs=pl.BlockSpec((nb, 1, tm // d, d * LANES), lambda b, i: (0, b, i, 0)),
        out_shape=jax.ShapeDtypeStruct((nb, batch, seq // d, d * LANES), BF16),
        scratch_shapes=[pltpu.VMEM((dm // LANES, tm, LANES), F32), pltpu.VMEM((tm, dm), BF16)],
        compiler_params=_params("parallel", "parallel"),
        name="dil_proj_d%d" % d,
    )(h, gain.reshape(1, dm), w, cos_t, sin_t)


def _mla_a_kernel(x_ref, g_ref, w_ref, qn_ref, kvn_ref, cos_ref, sin_ref, cq_ref, ckv_ref, kpe_ref):
    xn = _rms(x_ref[...], g_ref[...]).astype(BF16)
    y = _dot(xn, w_ref[...])
    cq_ref[...] = _rms(y[:, :MLA_Q_RANK], qn_ref[...]).astype(BF16)
    ckv_ref[...] = _rms(y[:, MLA_Q_RANK:MLA_Q_RANK + MLA_KV_RANK], kvn_ref[...]).astype(BF16)
    kpe_ref[...] = _rope(y[:, MLA_Q_RANK + MLA_KV_RANK:], cos_ref[...], sin_ref[...]).astype(BF16)


def _mla_a(h, gain, w_a, q_norm, kv_norm, cos_t, sin_t, *, seq, tm=1024):
    t, d = h.shape
    n = w_a.shape[1]
    ns = seq // tm
    row = lambda i: (i, 0)
    fixed = lambda i: (0, 0)
    tab = lambda i: (i % ns, 0)
    return pl.pallas_call(
        _mla_a_kernel,
        grid=(t // tm,),
        in_specs=[
            pl.BlockSpec((tm, d), row),
            pl.BlockSpec((1, d), fixed),
            pl.BlockSpec((d, n), fixed),
            pl.BlockSpec((1, MLA_Q_RANK), fixed),
            pl.BlockSpec((1, MLA_KV_RANK), fixed),
            pl.BlockSpec((tm, LANES), tab),
            pl.BlockSpec((tm, LANES), tab),
        ],
        out_specs=[
            pl.BlockSpec((tm, MLA_Q_RANK), row),
            pl.BlockSpec((tm, MLA_KV_RANK), row),
            pl.BlockSpec((tm, LANES), row),
        ],
        out_shape=[
            jax.ShapeDtypeStruct((t, MLA_Q_RANK), BF16),
            jax.ShapeDtypeStruct((t, MLA_KV_RANK), BF16),
            jax.ShapeDtypeStruct((t, LANES), BF16),
        ],
        compiler_params=_params("parallel"),
        name="mla_a",
    )(h, gain.reshape(1, d), w_a, q_norm.reshape(1, -1), kv_norm.reshape(1, -1), cos_t, sin_t)


def _mla_b_kernel(cq_ref, ckv_ref, kpe_ref, wq_ref, wk_ref, wvt_ref, cos_ref, sin_ref, qk_ref, vt_ref):
    q = _dot(cq_ref[...], wq_ref[...])
    for c in range(MLA_HEADS):
        blk = q[:, c * LANES:(c + 1) * LANES]
        qk_ref[c] = _rope(blk, cos_ref[...], sin_ref[...]).astype(BF16)
    k = _dot(ckv_ref[...], wk_ref[...])
    kpe = kpe_ref[...].astype(F32)
    for c in range(MLA_HEADS):
        qk_ref[MLA_HEADS + c] = (k[:, c * LANES:(c + 1) * LANES] + kpe).astype(BF16)
    vt = _dot_nt(wvt_ref[...], ckv_ref[...])
    for c in range(MLA_HEADS // 2):
        vt_ref[c] = vt[c * LANES:(c + 1) * LANES, :].astype(BF16)


def _mla_b(cq, ckv, kpe, wq, wk, wvt, cos_t, sin_t, *, seq, tm=512):
    t = cq.shape[0]
    ns = seq // tm
    row = lambda i: (i, 0)
    fixed = lambda i: (0, 0)
    tab = lambda i: (i % ns, 0)
    return pl.pallas_call(
        _mla_b_kernel,
        grid=(t // tm,),
        in_specs=[
            pl.BlockSpec((tm, MLA_Q_RANK), row),
            pl.BlockSpec((tm, MLA_KV_RANK), row),
            pl.BlockSpec((tm, LANES), row),
            pl.BlockSpec(wq.shape, fixed),
            pl.BlockSpec(wk.shape, fixed),
            pl.BlockSpec(wvt.shape, fixed),
            pl.BlockSpec((tm, LANES), tab),
            pl.BlockSpec((tm, LANES), tab),
        ],
        out_specs=[
            pl.BlockSpec((2 * MLA_HEADS, tm, LANES), lambda i: (0, i, 0)),
            pl.BlockSpec((MLA_HEADS // 2, LANES, tm), lambda i: (0, 0, i)),
        ],
        out_shape=[
            jax.ShapeDtypeStruct((2 * MLA_HEADS, t, LANES), BF16),
            jax.ShapeDtypeStruct((MLA_HEADS // 2, LANES, t), BF16),
        ],
        compiler_params=_params("parallel"),
        name="mla_b",
    )(cq, ckv, kpe, wq, wk, wvt, cos_t, sin_t)


def _fox_proj_kernel(x_ref, g_ref, wqk_ref, wvt_ref, wf_ref, bf_ref, tri_ref, scat_ref,
                     qk_ref, vt_ref, kd_ref, carry_ref, *, tm):
    @pl.when(pl.program_id(1) == 0)
    def _():
        carry_ref[...] = jnp.zeros_like(carry_ref)

    xn = _rms(x_ref[...], g_ref[...]).astype(BF16)
    qk = _dot(xn, wqk_ref[...])
    for c in range(qk_ref.shape[0]):
        qk_ref[c] = qk[:, c * LANES:(c + 1) * LANES].astype(BF16)
    vt = _dot_nt(wvt_ref[...], xn)
    for c in range(vt_ref.shape[0]):
        vt_ref[c] = vt[c * LANES:(c + 1) * LANES, :].astype(BF16)

    z = _dot(xn, wf_ref[...]) + bf_ref[...]
    logf = jnp.minimum(z, 0.0) - jnp.log(1.0 + jnp.exp(-jnp.abs(z)))
    tri = tri_ref[...]
    cum = carry_ref[...] + sum(_dot(tri, part) for part in _split3(logf))
    carry_ref[...] = cum[tm - 1:tm, :]
    kd = sum(_dot(part, scat_ref[p]) for p, part in enumerate(_split3(cum * (-LOG2E))))
    for c in range(kd_ref.shape[0]):
        kd_ref[c] = kd[:, c * LANES:(c + 1) * LANES].astype(BF16)


def _fox_proj(h, gain, wqk, wvt, wf, bf, *, batch, seq, tm=512):
    t, d = h.shape
    ns = seq // tm
    n_pairs = FOX_HEADS // 2
    tri = jnp.tril(jnp.ones((tm, tm), BF16))
    scat = np.zeros((GATE_PARTS, LANES, n_pairs * LANES), np.float32)
    for p in range(GATE_PARTS):
        for hd in range(FOX_HEADS):
            scat[p, hd, (hd // 2) * LANES + GATE_PARTS * (hd % 2) + p] = 1.0
    fixed2 = lambda b, i: (0, 0)
    rows = lambda b, i: (0, b * ns + i, 0)
    return pl.pallas_call(
        functools.partial(_fox_proj_kernel, tm=tm),
        grid=(batch, ns),
        in_specs=[
            pl.BlockSpec((tm, d), lambda b, i: (b * ns + i, 0)),
            pl.BlockSpec((1, d), fixed2),
            pl.BlockSpec(wqk.shape, fixed2),
            pl.BlockSpec(wvt.shape, fixed2),
            pl.BlockSpec(wf.shape, fixed2),
            pl.BlockSpec((1, LANES), fixed2),
            pl.BlockSpec((tm, tm), fixed2),
            pl.BlockSpec(scat.shape, lambda b, i: (0, 0, 0)),
        ],
        out_specs=[
            pl.BlockSpec((2 * n_pairs, tm, LANES), rows),
            pl.BlockSpec((n_pairs, LANES, tm), lambda b, i: (0, 0, b * ns + i)),
            pl.BlockSpec((n_pairs, tm, LANES), rows),
        ],
        out_shape=[
            jax.ShapeDtypeStruct((2 * n_pairs, t, LANES), BF16),
            jax.ShapeDtypeStruct((n_pairs, LANES, t), BF16),
            jax.ShapeDtypeStruct((n_pairs, t, LANES), BF16),
        ],
        scratch_shapes=[pltpu.VMEM((1, LANES), F32)],
        compiler_params=_params("arbitrary", "arbitrary"),
        name="fox_proj",
    )(h, gain.reshape(1, d), wqk, wvt, wf, bf, tri, jnp.asarray(scat, BF16))


def _flash_kernel(*refs, sep, has_bias, tq, unroll):
    if has_bias:
        q_ref, k_ref, vt_ref, kd_ref, o_ref, qs_ref, vta_ref, acc_ref, s_ref, mx_ref, m_ref = refs
    else:
        q_ref, k_ref, vt_ref, o_ref, qs_ref, vta_ref, acc_ref, s_ref, mx_ref, m_ref = refs
        kd_ref = None
    half = LANES // 2
    nq = acc_ref.shape[0]
    seq = nq * tq

    vt = vt_ref[0]
    r = lax.broadcasted_iota(jnp.int32, vt.shape, 0)
    one = jnp.ones_like(vt)
    vta_ref[0] = jnp.where(r < half, vt, one)
    vta_ref[1] = jnp.where(r < half, one, vt)
    if not sep or has_bias:
        lane = lax.broadcasted_iota(jnp.int32, (seq, LANES), 1)
        for hh in range(2):
            if sep:
                q_h = q_ref[hh]
            else:
                q = q_ref[0]
                own = (lane < half) if hh == 0 else (lane >= half)
                q_h = jnp.where(own, q, jnp.zeros_like(q))
            qs_ref[hh, :, 0:LANES] = q_h
            if has_bias:
                sel = jnp.logical_and(lane >= GATE_PARTS * hh, lane < GATE_PARTS * (hh + 1))
                qs_ref[hh, :, LANES:2 * LANES] = jnp.where(sel, 1.0, 0.0).astype(BF16)
    key = lax.broadcasted_iota(jnp.int32, (tq, tq), 0)
    qry = lax.broadcasted_iota(jnp.int32, (tq, tq), 1)
    src = qs_ref if (not sep or has_bias) else q_ref

    def scores(i, kb, slot, diag):
        qstart, kstart = i * tq, kb * tq
        if not diag:
            qstart, kstart = pl.multiple_of(qstart, tq), pl.multiple_of(kstart, tq)
        for hh in range(2):
            k_blk = k_ref[hh if sep else 0, pl.ds(kstart, tq), :]
            if has_bias:
                k_blk = jnp.concatenate([k_blk, kd_ref[0, pl.ds(kstart, tq), :]], axis=1)
            st = _dot_nt(k_blk, src[hh, pl.ds(qstart, tq), :])
            if diag:
                st = jnp.where(key <= qry, st, NEG_INF)
            s_ref[slot, hh] = st
            mx_ref[slot, hh] = _colmax(st)

    def consume(i, kb, slot, first):
        kstart = kb * tq if first else pl.multiple_of(kb * tq, tq)
        for hh in range(2):
            m_new = mx_ref[slot, hh]
            if not first:
                m_old = m_ref[i, hh]
                m_new = jnp.maximum(m_old, m_new)
            m_ref[i, hh] = m_new
            pt = jnp.exp2(s_ref[slot, hh] - m_new).astype(BF16)
            pv = _dot(vta_ref[hh, :, pl.ds(kstart, tq)], pt)
            acc_ref[i, hh] = pv if first else jnp.exp2(m_old - m_new) * acc_ref[i, hh] + pv

    scores(0, 0, 0, True)
    for i in range(nq):
        if i + 1 < nq:
            scores(i + 1, i + 1, (i + 1) % 2, True)
        elif nq > 1:
            scores(1, 0, (i + 1) % 2, False)
        consume(i, i, i % 2, True)

    def advance(i, kb):
        last = kb + 1 == i
        return jnp.where(last, i + 1, i), jnp.where(last, 0, kb + 1)

    def trip(_, carry):
        i, kb = carry
        for u in range(unroll):
            ni, nkb = advance(i, kb)
            scores(jnp.minimum(ni, nq - 1), nkb, (nq + u + 1) % 2, False)
            consume(i, kb, (nq + u) % 2, False)
            i, kb = ni, nkb
        return i, kb

    n_items = nq * (nq - 1) // 2
    assert unroll % 2 == 0 and n_items % unroll == 0
    lax.fori_loop(0, n_items // unroll, trip, (jnp.int32(1), jnp.int32(0)))

    r = lax.broadcasted_iota(jnp.int32, (LANES, tq), 0)
    for i in range(nq):
        acc_a, acc_b = acc_ref[i, 0], acc_ref[i, 1]
        o_t = jnp.where(r < half, acc_a / acc_a[half:half + 1, :], acc_b / acc_b[0:1, :])
        o_ref[0, i * tq:(i + 1) * tq, :] = o_t.T.astype(o_ref.dtype)


def _flash(qk, vt, kd, *, batch, seq, n_pairs, q_blk, k_blk, sep, tq=512, unroll=14):
    t = qk.shape[1]
    nq = seq // tq
    w = 2 if sep else 1
    kc = 2 * LANES if kd is not None else LANES
    in_specs = [
        pl.BlockSpec((w, seq, LANES), lambda b, hp: (q_blk // w + hp, b, 0)),
        pl.BlockSpec((w, seq, LANES), lambda b, hp: (k_blk // w + hp, b, 0)),
        pl.BlockSpec((1, LANES, seq), lambda b, hp: (hp, 0, b)),
    ]
    args = [qk, qk, vt]
    if kd is not None:
        in_specs.append(pl.BlockSpec((1, seq, LANES), lambda b, hp: (hp, b, 0)))
        args.append(kd)
    return pl.pallas_call(
        functools.partial(_flash_kernel, sep=sep, has_bias=kd is not None, tq=tq, unroll=unroll),
        grid=(batch, n_pairs),
        in_specs=in_specs,
        out_specs=pl.BlockSpec((1, seq, LANES), lambda b, hp: (hp, b, 0)),
        out_shape=jax.ShapeDtypeStruct((n_pairs, t, LANES), BF16),
        scratch_shapes=[
            pltpu.VMEM((2, seq, kc), BF16),
            pltpu.VMEM((2, LANES, seq), BF16),
            pltpu.VMEM((nq, 2, LANES, tq), F32),
            pltpu.VMEM((2, 2, tq, tq), F32),
            pltpu.VMEM((2, 2, 1, tq), F32),
            pltpu.VMEM((nq, 2, 1, tq), F32),
        ],
        compiler_params=_params("parallel", "parallel"),
        name="flash_sep" if sep else "flash_shared",
    )(*args)


def _dil_attn_kernel(*refs):
    n_in = 5 * len(DIL_PATTERNS)
    o_ref, og_ref, lg_ref, band_ref = refs[n_in:]
    i = pl.program_id(2)
    w = DIL_WINDOW_KEYS
    v_first = lax.broadcasted_iota(jnp.int32, (w, LANES), 1) < 64
    qi = lax.broadcasted_iota(jnp.int32, (2 * w, 2 * w), 0) % w
    kk = lax.broadcasted_iota(jnp.int32, (2 * w, 2 * w), 1)
    band = jnp.logical_and(kk >= qi, kk <= qi + w)
    band_ref[0] = jnp.where(band, 0.0, NEG_INF)
    band_ref[1] = jnp.where(jnp.logical_and(band, kk >= jnp.where(i > 0, 0, w)), 0.0, NEG_INF)
    ones = jnp.ones((2 * w, LANES), BF16)
    for g, (_, d) in enumerate(DIL_PATTERNS):
        q_ref, k_ref, kp_ref, v_ref, vp_ref = refs[5 * g:5 * g + 5]
        for r in range(d):
            ls = slice(r * LANES, (r + 1) * LANES)
            for sub in range(DIL_TOKENS // d // w):
                rows = slice(sub * w, (sub + 1) * w)
                q_u = jnp.concatenate([q_ref[0, 0, rows, ls], q_ref[1, 0, rows, ls]], axis=0)
                if sub == 0:
                    k_u = jnp.concatenate([kp_ref[0, 0, :, ls], k_ref[0, 0, 0:w, ls]], axis=0)
                    v_u = jnp.concatenate([vp_ref[0, 0, :, ls], v_ref[0, 0, 0:w, ls]], axis=0)
                else:
                    k_u = k_ref[0, 0, (sub - 1) * w:(sub + 1) * w, ls]
                    v_u = v_ref[0, 0, (sub - 1) * w:(sub + 1) * w, ls]
                s = _dot_nt(q_u, k_u) + band_ref[1 if sub == 0 else 0]
                m = jnp.max(s, axis=-1, keepdims=True)
                p = jnp.exp2(s - m).astype(BF16)
                ov = _dot(p, jnp.concatenate([v_u, ones], axis=1))
                num = jnp.where(v_first, ov[0:w, 0:LANES], ov[w:2 * w, 0:LANES])
                den = jnp.where(v_first, ov[0:w, LANES:2 * LANES], ov[w:2 * w, LANES:2 * LANES])
                top = jnp.where(v_first, jnp.broadcast_to(m[0:w], (w, LANES)),
                                jnp.broadcast_to(m[w:2 * w], (w, LANES)))
                tok = pl.ds(sub * w * d + r, w, stride=d) if d > 1 else pl.ds(sub * w, w)
                og_ref[g, tok, :] = num / den
                lg_ref[g, tok, :] = top + jnp.log(den) * LOG2E
    lse = [lg_ref[g] for g in range(len(DIL_PATTERNS))]
    top = functools.reduce(jnp.maximum, lse)
    e = [jnp.exp2(x - top) for x in lse]
    num = sum(e[g] * og_ref[g] for g in range(len(DIL_PATTERNS)))
    o_ref[0] = (num / sum(e)).astype(o_ref.dtype)


def _dil_attn(groups, *, batch, seq):
    n_pairs = DIL_HEADS // 2
    n_tok_blk = seq // DIL_TOKENS
    in_specs, args = [], []
    for arr, (_, d) in zip(groups, DIL_PATTERNS):
        rows = DIL_TOKENS // d
        sub_per_blk = rows // DIL_WINDOW_KEYS

        def cur(base, rows=rows, d=d):
            return pl.BlockSpec((1, 1, rows, d * LANES), lambda b, hp, i: (base + hp, b, i, 0))

        def prev(base, spb=sub_per_blk, d=d):
            return pl.BlockSpec((1, 1, DIL_WINDOW_KEYS, d * LANES),
                                lambda b, hp, i: (base + hp, b, jnp.maximum(i * spb - 1, 0), 0))

        q_spec = pl.BlockSpec((2, 1, rows, d * LANES), lambda b, hp, i: (hp, b, i, 0))
        in_specs += [q_spec, cur(2 * n_pairs), prev(2 * n_pairs), cur(3 * n_pairs), prev(3 * n_pairs)]
        args += [arr] * 5
    n_g = len(DIL_PATTERNS)
    w2 = 2 * DIL_WINDOW_KEYS
    return pl.pallas_call(
        _dil_attn_kernel,
        grid=(batch, n_pairs, n_tok_blk),
        in_specs=in_specs,
        out_specs=pl.BlockSpec((1, DIL_TOKENS, LANES), lambda b, hp, i: (hp, b * n_tok_blk + i, 0)),
        out_shape=jax.ShapeDtypeStruct((n_pairs, batch * seq, LANES), BF16),
        scratch_shapes=[pltpu.VMEM((n_g, DIL_TOKENS, LANES), F32), pltpu.VMEM((n_g, DIL_TOKENS, LANES), F32),
                        pltpu.VMEM((2, w2, w2), F32)],
        compiler_params=_params("parallel", "parallel", "arbitrary"),
        name="dil_attn",
    )(*args)


def _mlp_kernel(h_ref, o_ref, wo_ref, g_ref, wu_ref, wd_ref, fg_ref, out_ref, xn_ref, acc_ref, *, final_norm):
    f = pl.program_id(1)

    @pl.when(f == 0)
    def _():
        rows = h_ref.shape[0] // 4
        for c in range(4):
            rs = slice(c * rows, (c + 1) * rows)
            a = jnp.concatenate([o_ref[k, rs, :] for k in range(o_ref.shape[0])], axis=-1)
            h1 = h_ref[rs, :] + _dot(a, wo_ref[...])
            xn_ref[rs, :] = _rms(h1, g_ref[...]).astype(BF16)
            acc_ref[rs, :] = h1

    u = jnp.maximum(_dot(xn_ref[...], wu_ref[...]), 0.0)
    acc_ref[...] += _dot((u * u).astype(BF16), wd_ref[...])

    @pl.when(f == pl.num_programs(1) - 1)
    def _():
        y = acc_ref[...]
        out_ref[...] = _rms(y, fg_ref[...]) if final_norm else y


def _mlp(h, o, wo, gain, w_up, w_down, final_gain, *, final_norm, tm=1024, tf=1024):
    t, d = h.shape
    dff = w_up.shape[1]
    return pl.pallas_call(
        functools.partial(_mlp_kernel, final_norm=final_norm),
        grid=(t // tm, dff // tf),
        in_specs=[
            pl.BlockSpec((tm, d), lambda i, f: (i, 0)),
            pl.BlockSpec((o.shape[0], tm, LANES), lambda i, f: (0, i, 0)),
            pl.BlockSpec(wo.shape, lambda i, f: (0, 0)),
            pl.BlockSpec((1, d), lambda i, f: (0, 0)),
            pl.BlockSpec((d, tf), lambda i, f: (0, f)),
            pl.BlockSpec((tf, d), lambda i, f: (f, 0)),
            pl.BlockSpec((1, d), lambda i, f: (0, 0)),
        ],
        out_specs=pl.BlockSpec((tm, d), lambda i, f: (i, 0)),
        out_shape=jax.ShapeDtypeStruct((t, d), F32),
        scratch_shapes=[pltpu.VMEM((tm, d), BF16), pltpu.VMEM((tm, d), F32)],
        compiler_params=_params("parallel", "arbitrary"),
        name="mlp_final" if final_norm else "mlp",
    )(h, o, wo, gain.reshape(1, d), w_up, w_down, final_gain.reshape(1, d))


def _rope_halves(seq, dim):
    inv = 1.0 / (ROPE_THETA ** (jnp.arange(0, dim, 2, dtype=F32) / dim))
    ang = jnp.arange(seq, dtype=F32)[:, None] * inv[None, :]
    return jnp.cos(ang), jnp.sin(ang)


def _mla_tables(seq):
    cos, sin = _rope_halves(seq, MLA_ROPE)
    one = lambda n: jnp.ones((seq, n), F32)
    zero = lambda n: jnp.zeros((seq, n), F32)
    cos_t = jnp.concatenate([one(32), cos, one(16), one(32), cos, one(16)], axis=1)
    sin_t = jnp.concatenate([zero(32), -sin, zero(16), zero(32), sin, zero(16)], axis=1)
    return cos_t, sin_t


def _dil_tables(seq, d, tm):
    cos, sin = _rope_halves(seq, DIL_DIM)
    order = lambda x: x.reshape(seq // tm, tm // d, d, LANES).transpose(0, 2, 1, 3).reshape(seq, LANES)
    return (order(jnp.concatenate([cos] * 4, axis=1)),
            order(jnp.concatenate([-sin, -sin, sin, sin], axis=1)))


def _mla_weights(wq_a, wq_b, wkv_a, wkv_b):
    d = wq_a.shape[0]
    z = lambda r, n: jnp.zeros((r, n), F32)
    kpe = wkv_a[:, MLA_KV_RANK:]
    w_a = jnp.concatenate(
        [wq_a, wkv_a[:, :MLA_KV_RANK], z(d, 32), kpe[:, :16], z(d, 48), kpe[:, 16:], z(d, 16)], axis=1)
    scale = (MLA_NOPE + MLA_ROPE) ** -0.5 * LOG2E
    q3 = wq_b.reshape(MLA_Q_RANK, MLA_HEADS, MLA_NOPE + MLA_ROPE) * scale
    zq = jnp.zeros((MLA_Q_RANK, MLA_HEADS, 16), F32)
    wq = jnp.concatenate(
        [q3[..., :32], q3[..., 64:80], zq, q3[..., 32:64], q3[..., 80:96], zq], axis=-1)
    kv3 = wkv_b.reshape(MLA_KV_RANK, MLA_HEADS, MLA_NOPE + MLA_V)
    zk = jnp.zeros((MLA_KV_RANK, MLA_HEADS, 32), F32)
    wk = jnp.concatenate([kv3[..., :32], zk, kv3[..., 32:64], zk], axis=-1)
    wvt = kv3[..., MLA_NOPE:].reshape(MLA_KV_RANK, -1).T
    return (w_a.astype(BF16), wq.reshape(MLA_Q_RANK, -1).astype(BF16),
            wk.reshape(MLA_KV_RANK, -1).astype(BF16), wvt.astype(BF16))


def _dil_weights(w_qkv):
    d = w_qkv.shape[0]
    n_g = len(DIL_PATTERNS)
    n = DIL_HEADS * DIL_DIM
    w = w_qkv.reshape(d, 3, n_g, n)
    qk = w[:, :2].reshape(d, 2, n_g, n // LANES, 2, 2, 32)
    qk = qk * jnp.array([DIL_DIM ** -0.5 * LOG2E, 1.0], F32).reshape(1, 2, 1, 1, 1, 1, 1)
    qk = qk.transpose(0, 1, 2, 3, 5, 4, 6).reshape(d, 2, n_g, n)
    w = jnp.concatenate([qk, w[:, 2:]], axis=1).astype(BF16)
    return [w[:, :, g].reshape(d, 3 * n) for g in range(n_g)]


def _mla_attn(h, attn_norm, wq_a, q_norm, wq_b, wkv_a, kv_norm, wkv_b, tables, *, batch, seq):
    w_a, wq, wk, wvt = _mla_weights(wq_a, wq_b, wkv_a, wkv_b)
    cq, ckv, kpe = _mla_a(h, attn_norm, w_a, q_norm, kv_norm, *tables, seq=seq)
    qk, vt = _mla_b(cq, ckv, kpe, wq, wk, wvt, *tables, seq=seq)
    return _flash(qk, vt, None, batch=batch, seq=seq, n_pairs=MLA_HEADS // 2,
                  q_blk=0, k_blk=MLA_HEADS, sep=True)


def _fox_attn(h, attn_norm, w_qkv, w_f, b_f, *, batch, seq):
    n = FOX_HEADS * FOX_DIM
    d = w_qkv.shape[0]
    wqk = jnp.concatenate([w_qkv[:, :n] * (FOX_DIM ** -0.5 * LOG2E), w_qkv[:, n:2 * n]], axis=1)
    wvt = w_qkv[:, 2 * n:].T
    wf = jnp.concatenate([w_f, jnp.zeros((d, LANES - FOX_HEADS), F32)], axis=1)
    bf = jnp.concatenate([b_f.astype(F32), jnp.zeros((LANES - FOX_HEADS,), F32)]).reshape(1, LANES)
    qk, vt, kd = _fox_proj(h, attn_norm, wqk.astype(BF16), wvt.astype(BF16), wf.astype(BF16), bf,
                           batch=batch, seq=seq)
    n_pairs = FOX_HEADS // 2
    return _flash(qk, vt, kd, batch=batch, seq=seq, n_pairs=n_pairs, q_blk=0, k_blk=n_pairs, sep=False)


def _dil_attn_layer(h, attn_norm, w_qkv, *, batch, seq, tm=512):
    groups = []
    for w_g, (_, d) in zip(_dil_weights(w_qkv), DIL_PATTERNS):
        groups.append(_dil_proj(h, attn_norm, w_g, *_dil_tables(seq, d, tm),
                                batch=batch, seq=seq, d=d, tm=tm))
    return _dil_attn(groups, batch=batch, seq=seq)


def kernel(x, l0_attn_norm, l0_mla_wq_a, l0_mla_q_norm, l0_mla_wq_b, l0_mla_wkv_a, l0_mla_kv_norm, l0_mla_wkv_b, l0_mla_wo, l0_mlp_norm, l0_w_up, l0_w_down, l1_attn_norm, l1_fox_w_qkv, l1_fox_w_f, l1_fox_b_f, l1_fox_wo, l1_mlp_norm, l1_w_up, l1_w_down, l2_attn_norm, l2_dil_w_qkv, l2_dil_wo, l2_mlp_norm, l2_w_up, l2_w_down, l3_attn_norm, l3_mla_wq_a, l3_mla_q_norm, l3_mla_wq_b, l3_mla_wkv_a, l3_mla_kv_norm, l3_mla_wkv_b, l3_mla_wo, l3_mlp_norm, l3_w_up, l3_w_down, final_norm):
    batch, seq, d = x.shape
    assert seq % DIL_TOKENS == 0 and (batch * seq) % 1024 == 0
    for window, dil in DIL_PATTERNS:
        assert window // dil == DIL_WINDOW_KEYS
    mla_t = _mla_tables(seq)
    kw = dict(batch=batch, seq=seq)
    bf = lambda w: w.astype(BF16)
    h = x.reshape(batch * seq, d)

    o = _mla_attn(h, l0_attn_norm, l0_mla_wq_a, l0_mla_q_norm, l0_mla_wq_b, l0_mla_wkv_a,
                  l0_mla_kv_norm, l0_mla_wkv_b, mla_t, **kw)
    h = _mlp(h, o, bf(l0_mla_wo), l0_mlp_norm, bf(l0_w_up), bf(l0_w_down), final_norm, final_norm=False)

    o = _fox_attn(h, l1_attn_norm, l1_fox_w_qkv, l1_fox_w_f, l1_fox_b_f, **kw)
    h = _mlp(h, o, bf(l1_fox_wo), l1_mlp_norm, bf(l1_w_up), bf(l1_w_down), final_norm, final_norm=False)

    o = _dil_attn_layer(h, l2_attn_norm, l2_dil_w_qkv, **kw)
    h = _mlp(h, o, bf(l2_dil_wo), l2_mlp_norm, bf(l2_w_up), bf(l2_w_down), final_norm, final_norm=False)

    o = _mla_attn(h, l3_attn_norm, l3_mla_wq_a, l3_mla_q_norm, l3_mla_wq_b, l3_mla_wkv_a,
                  l3_mla_kv_norm, l3_mla_wkv_b, mla_t, **kw)
    h = _mlp(h, o, bf(l3_mla_wo), l3_mlp_norm, bf(l3_w_up), bf(l3_w_down), final_norm, final_norm=True)
    return h.reshape(batch, seq, d)
```

```python
import functools
import math

import numpy as np
import jax
import jax.numpy as jnp
from jax import lax
from jax.experimental import pallas as pl
from jax.experimental.pallas import tpu as pltpu

F32 = jnp.float32
BF16 = jnp.bfloat16

LANES = 128
NORM_EPS = 1e-6
NEG_INF = -1e30
ROPE_THETA = 10000.0
LOG2E = math.log2(math.e)

MLA_HEADS = 16
MLA_Q_RANK = 384
MLA_KV_RANK = 256
MLA_NOPE = 64
MLA_ROPE = 32
MLA_V = 64
FOX_HEADS = 16
FOX_DIM = 64
DIL_PATTERNS = ((128, 1), (512, 4), (2048, 16))
DIL_HEADS = 16
DIL_DIM = 64
DIL_WINDOW_KEYS = 128
DIL_TOKENS = 2048
GATE_PARTS = 3

VMEM_LIMIT = 56 * 2**20


def _params(*sem):
    return pltpu.CompilerParams(dimension_semantics=sem, vmem_limit_bytes=VMEM_LIMIT)


def _rms(x, g):
    ms = jnp.mean(x * x, axis=-1, keepdims=True)
    return x * lax.rsqrt(ms + NORM_EPS) * g


def _dot(a, b):
    return jnp.dot(a, b, preferred_element_type=F32)


def _dot_nt(a, b):
    return lax.dot_general(a, b, (((1,), (1,)), ((), ())), preferred_element_type=F32)


def _rope(blk, cos, sin):
    return blk * cos + pltpu.roll(blk, 64, axis=1) * sin


def _colmax(st):
    n, c = st.shape
    return jnp.max(jnp.max(st.reshape(8, n // 8, c), axis=0), axis=0, keepdims=True)


def _split3(x):
    hi = x.astype(BF16)
    r1 = x - hi.astype(F32)
    mid = r1.astype(BF16)
    lo = (r1 - mid.astype(F32)).astype(BF16)
    return hi, mid, lo


def _dil_proj_kernel(x_ref, g_ref, w_ref, cos_ref, sin_ref, o_ref, xs_ref, xn_ref, *, d, tm):
    rpr = tm // d
    nc = x_ref.shape[1] // LANES
    xn = _rms(x_ref[...], g_ref[...])
    if d == 1:
        xn_ref[...] = xn.astype(BF16)
    else:
        for c in range(nc):
            xs_ref[c] = xn[:, c * LANES:(c + 1) * LANES]
        for c in range(nc):
            for r in range(d):
                xn_ref[r * rpr:(r + 1) * rpr, c * LANES:(c + 1) * LANES] = (
                    xs_ref[c, pl.ds(r, rpr, stride=d), :].astype(BF16))

    nb = o_ref.shape[0] // 4
    lane = lax.broadcasted_iota(jnp.int32, (1, LANES), 1)
    first = jnp.where((lane % 64) < 32, 1.0, 0.0)
    for j in range(3):
        acc = _dot(xn_ref[...], w_ref[:, j * nb * LANES:(j + 1) * nb * LANES])
        for c in range(nb):
            for r in range(d):
                rs = slice(r * rpr, (r + 1) * rpr)
                ls = slice(r * LANES, (r + 1) * LANES)
                blk = acc[rs, c * LANES:(c + 1) * LANES]
                if j < 2:
                    blk = _rope(blk, cos_ref[rs, :], sin_ref[rs, :])
                if j == 0:
                    o_ref[2 * c, 0, :, ls] = (blk * first).astype(o_ref.dtype)
                    o_ref[2 * c + 1, 0, :, ls] = (blk * (1.0 - first)).astype(o_ref.dtype)
                else:
                    o_ref[(j + 1) * nb + c, 0, :, ls] = blk.astype(o_ref.dtype)


def _dil_proj(h, gain, w, cos_t, sin_t, *, batch, seq, d, tm):
    t, dm = h.shape
    ns = seq // tm
    nb = 4 * w.shape[1] // (3 * LANES)
    return pl.pallas_call(
        functools.partial(_dil_proj_kernel, d=d, tm=tm),
        grid=(batch, ns),
        in_specs=[
            pl.BlockSpec((tm, dm), lambda b, i: (b * ns + i, 0)),
            pl.BlockSpec((1, dm), lambda b, i: (0, 0)),
            pl.BlockSpec(w.shape, lambda b, i: (0, 0)),
            pl.BlockSpec((tm, LANES), lambda b, i: (i, 0)),
            pl.BlockSpec((tm, LANES), lambda b, i: (i, 0)),
        ],
        out_specs=pl.BlockSpec((nb, 1, tm // d, d * LANES), lambda b, i: (0, b, i, 0)),
        out_shape=jax.ShapeDtypeStruct((nb, batch, seq // d, d * LANES), BF16),
        scratch_shapes=[pltpu.VMEM((dm // LANES, tm, LANES), F32), pltpu.VMEM((tm, dm), BF16)],
        compiler_params=_params("parallel", "parallel"),
        name="dil_proj_d%d" % d,
    )(h, gain.reshape(1, dm), w, cos_t, sin_t)


def _mla_a_kernel(x_ref, g_ref, w_ref, qn_ref, kvn_ref, cos_ref, sin_ref, cq_ref, ckv_ref, kpe_ref):
    xn = _rms(x_ref[...], g_ref[...]).astype(BF16)
    y = _dot(xn, w_ref[...])
    cq_ref[...] = _rms(y[:, :MLA_Q_RANK], qn_ref[...]).astype(BF16)
    ckv_ref[...] = _rms(y[:, MLA_Q_RANK:MLA_Q_RANK + MLA_KV_RANK], kvn_ref[...]).astype(BF16)
    kpe_ref[...] = _rope(y[:, MLA_Q_RANK + MLA_KV_RANK:], cos_ref[...], sin_ref[...]).astype(BF16)


def _mla_a(h, gain, w_a, q_norm, kv_norm, cos_t, sin_t, *, seq, tm=1024):
    t, d = h.shape
    n = w_a.shape[1]
    ns = seq // tm
    row = lambda i: (i, 0)
    fixed = lambda i: (0, 0)
    tab = lambda i: (i % ns, 0)
    return pl.pallas_call(
        _mla_a_kernel,
        grid=(t // tm,),
        in_specs=[
            pl.BlockSpec((tm, d), row),
            pl.BlockSpec((1, d), fixed),
            pl.BlockSpec((d, n), fixed),
            pl.BlockSpec((1, MLA_Q_RANK), fixed),
            pl.BlockSpec((1, MLA_KV_RANK), fixed),
            pl.BlockSpec((tm, LANES), tab),
            pl.BlockSpec((tm, LANES), tab),
        ],
        out_specs=[
            pl.BlockSpec((tm, MLA_Q_RANK), row),
            pl.BlockSpec((tm, MLA_KV_RANK), row),
            pl.BlockSpec((tm, LANES), row),
        ],
        out_shape=[
            jax.ShapeDtypeStruct((t, MLA_Q_RANK), BF16),
            jax.ShapeDtypeStruct((t, MLA_KV_RANK), BF16),
            jax.ShapeDtypeStruct((t, LANES), BF16),
        ],
        compiler_params=_params("parallel"),
        name="mla_a",
    )(h, gain.reshape(1, d), w_a, q_norm.reshape(1, -1), kv_norm.reshape(1, -1), cos_t, sin_t)


def _mla_b_kernel(cq_ref, ckv_ref, kpe_ref, wq_ref, wk_ref, wvt_ref, cos_ref, sin_ref, qk_ref, vt_ref):
    q = _dot(cq_ref[...], wq_ref[...])
    for c in range(MLA_HEADS):
        blk = q[:, c * LANES:(c + 1) * LANES]
        qk_ref[c] = _rope(blk, cos_ref[...], sin_ref[...]).astype(BF16)
    k = _dot(ckv_ref[...], wk_ref[...])
    kpe = kpe_ref[...].astype(F32)
    for c in range(MLA_HEADS):
        qk_ref[MLA_HEADS + c] = (k[:, c * LANES:(c + 1) * LANES] + kpe).astype(BF16)
    vt = _dot_nt(wvt_ref[...], ckv_ref[...])
    for c in range(MLA_HEADS // 2):
        vt_ref[c] = vt[c * LANES:(c + 1) * LANES, :].astype(BF16)


def _mla_b(cq, ckv, kpe, wq, wk, wvt, cos_t, sin_t, *, seq, tm=512):
    t = cq.shape[0]
    ns = seq // tm
    row = lambda i: (i, 0)
    fixed = lambda i: (0, 0)
    tab = lambda i: (i % ns, 0)
    return pl.pallas_call(
        _mla_b_kernel,
        grid=(t // tm,),
        in_specs=[
            pl.BlockSpec((tm, MLA_Q_RANK), row),
            pl.BlockSpec((tm, MLA_KV_RANK), row),
            pl.BlockSpec((tm, LANES), row),
            pl.BlockSpec(wq.shape, fixed),
            pl.BlockSpec(wk.shape, fixed),
            pl.BlockSpec(wvt.shape, fixed),
            pl.BlockSpec((tm, LANES), tab),
            pl.BlockSpec((tm, LANES), tab),
        ],
        out_specs=[
            pl.BlockSpec((2 * MLA_HEADS, tm, LANES), lambda i: (0, i, 0)),
            pl.BlockSpec((MLA_HEADS // 2, LANES, tm), lambda i: (0, 0, i)),
        ],
        out_shape=[
            jax.ShapeDtypeStruct((2 * MLA_HEADS, t, LANES), BF16),
            jax.ShapeDtypeStruct((MLA_HEADS // 2, LANES, t), BF16),
        ],
        compiler_params=_params("parallel"),
        name="mla_b",
    )(cq, ckv, kpe, wq, wk, wvt, cos_t, sin_t)


def _fox_proj_kernel(x_ref, g_ref, wqk_ref, wvt_ref, wf_ref, bf_ref, tri_ref, scat_ref,
                     qk_ref, vt_ref, kd_ref, carry_ref, *, tm):
    @pl.when(pl.program_id(1) == 0)
    def _():
        carry_ref[...] = jnp.zeros_like(carry_ref)

    xn = _rms(x_ref[...], g_ref[...]).astype(BF16)
    qk = _dot(xn, wqk_ref[...])
    for c in range(qk_ref.shape[0]):
        qk_ref[c] = qk[:, c * LANES:(c + 1) * LANES].astype(BF16)
    vt = _dot_nt(wvt_ref[...], xn)
    for c in range(vt_ref.shape[0]):
        vt_ref[c] = vt[c * LANES:(c + 1) * LANES, :].astype(BF16)

    z = _dot(xn, wf_ref[...]) + bf_ref[...]
    logf = jnp.minimum(z, 0.0) - jnp.log(1.0 + jnp.exp(-jnp.abs(z)))
    tri = tri_ref[...]
    cum = carry_ref[...] + sum(_dot(tri, part) for part in _split3(logf))
    carry_ref[...] = cum[tm - 1:tm, :]
    kd = sum(_dot(part, scat_ref[p]) for p, part in enumerate(_split3(cum * (-LOG2E))))
    for c in range(kd_ref.shape[0]):
        kd_ref[c] = kd[:, c * LANES:(c + 1) * LANES].astype(BF16)


def _fox_proj(h, gain, wqk, wvt, wf, bf, *, batch, seq, tm=512):
    t, d = h.shape
    ns = seq // tm
    n_pairs = FOX_HEADS // 2
    tri = jnp.tril(jnp.ones((tm, tm), BF16))
    scat = np.zeros((GATE_PARTS, LANES, n_pairs * LANES), np.float32)
    for p in range(GATE_PARTS):
        for hd in range(FOX_HEADS):
            scat[p, hd, (hd // 2) * LANES + GATE_PARTS * (hd % 2) + p] = 1.0
    fixed2 = lambda b, i: (0, 0)
    rows = lambda b, i: (0, b * ns + i, 0)
    return pl.pallas_call(
        functools.partial(_fox_proj_kernel, tm=tm),
        grid=(batch, ns),
        in_specs=[
            pl.BlockSpec((tm, d), lambda b, i: (b * ns + i, 0)),
            pl.BlockSpec((1, d), fixed2),
            pl.BlockSpec(wqk.shape, fixed2),
            pl.BlockSpec(wvt.shape, fixed2),
            pl.BlockSpec(wf.shape, fixed2),
            pl.BlockSpec((1, LANES), fixed2),
            pl.BlockSpec((tm, tm), fixed2),
            pl.BlockSpec(scat.shape, lambda b, i: (0, 0, 0)),
        ],
        out_specs=[
            pl.BlockSpec((2 * n_pairs, tm, LANES), rows),
            pl.BlockSpec((n_pairs, LANES, tm), lambda b, i: (0, 0, b * ns + i)),
            pl.BlockSpec((n_pairs, tm, LANES), rows),
        ],
        out_shape=[
            jax.ShapeDtypeStruct((2 * n_pairs, t, LANES), BF16),
            jax.ShapeDtypeStruct((n_pairs, LANES, t), BF16),
            jax.ShapeDtypeStruct((n_pairs, t, LANES), BF16),
        ],
        scratch_shapes=[pltpu.VMEM((1, LANES), F32)],
        compiler_params=_params("arbitrary", "arbitrary"),
        name="fox_proj",
    )(h, gain.reshape(1, d), wqk, wvt, wf, bf, tri, jnp.asarray(scat, BF16))


def _flash_kernel(*refs, sep, has_bias, tq, unroll):
    if has_bias:
        q_ref, k_ref, vt_ref, kd_ref, o_ref, qs_ref, vta_ref, acc_ref, s_ref, mx_ref, m_ref = refs
    else:
        q_ref, k_ref, vt_ref, o_ref, qs_ref, vta_ref, acc_ref, s_ref, mx_ref, m_ref = refs
        kd_ref = None
    half = LANES // 2
    nq = acc_ref.shape[0]
    seq = nq * tq

    vt = vt_ref[0]
    r = lax.broadcasted_iota(jnp.int32, vt.shape, 0)
    one = jnp.ones_like(vt)
    vta_ref[0] = jnp.where(r < half, vt, one)
    vta_ref[1] = jnp.where(r < half, one, vt)
    if not sep or has_bias:
        lane = lax.broadcasted_iota(jnp.int32, (seq, LANES), 1)
        for hh in range(2):
            if sep:
                q_h = q_ref[hh]
            else:
                q = q_ref[0]
                own = (lane < half) if hh == 0 else (lane >= half)
                q_h = jnp.where(own, q, jnp.zeros_like(q))
            qs_ref[hh, :, 0:LANES] = q_h
            if has_bias:
                sel = jnp.logical_and(lane >= GATE_PARTS * hh, lane < GATE_PARTS * (hh + 1))
                qs_ref[hh, :, LANES:2 * LANES] = jnp.where(sel, 1.0, 0.0).astype(BF16)
    key = lax.broadcasted_iota(jnp.int32, (tq, tq), 0)
    qry = lax.broadcasted_iota(jnp.int32, (tq, tq), 1)
    src = qs_ref if (not sep or has_bias) else q_ref

    def scores(i, kb, slot, diag):
        qstart, kstart = i * tq, kb * tq
        if not diag:
            qstart, kstart = pl.multiple_of(qstart, tq), pl.multiple_of(kstart, tq)
        for hh in range(2):
            k_blk = k_ref[hh if sep else 0, pl.ds(kstart, tq), :]
            if has_bias:
                k_blk = jnp.concatenate([k_blk, kd_ref[0, pl.ds(kstart, tq), :]], axis=1)
            st = _dot_nt(k_blk, src[hh, pl.ds(qstart, tq), :])
            if diag:
                st = jnp.where(key <= qry, st, NEG_INF)
            s_ref[slot, hh] = st
            mx_ref[slot, hh] = _colmax(st)

    def consume(i, kb, slot, first):
        kstart = kb * tq if first else pl.multiple_of(kb * tq, tq)
        for hh in range(2):
            m_new = mx_ref[slot, hh]
            if not first:
                m_old = m_ref[i, hh]
                m_new = jnp.maximum(m_old, m_new)
            m_ref[i, hh] = m_new
            pt = jnp.exp2(s_ref[slot, hh] - m_new).astype(BF16)
            pv = _dot(vta_ref[hh, :, pl.ds(kstart, tq)], pt)
            acc_ref[i, hh] = pv if first else jnp.exp2(m_old - m_new) * acc_ref[i, hh] + pv

    scores(0, 0, 0, True)
    for i in range(nq):
        if i + 1 < nq:
            scores(i + 1, i + 1, (i + 1) % 2, True)
        elif nq > 1:
            scores(1, 0, (i + 1) % 2, False)
        consume(i, i, i % 2, True)

    def advance(i, kb):
        last = kb + 1 == i
        return jnp.where(last, i + 1, i), jnp.where(last, 0, kb + 1)

    def trip(_, carry):
        i, kb = carry
        for u in range(unroll):
            ni, nkb = advance(i, kb)
            scores(jnp.minimum(ni, nq - 1), nkb, (nq + u + 1) % 2, False)
            consume(i, kb, (nq + u) % 2, False)
            i, kb = ni, nkb
        return i, kb

    n_items = nq * (nq - 1) // 2
    assert unroll % 2 == 0 and n_items % unroll == 0
    lax.fori_loop(0, n_items // unroll, trip, (jnp.int32(1), jnp.int32(0)))

    r = lax.broadcasted_iota(jnp.int32, (LANES, tq), 0)
    for i in range(nq):
        acc_a, acc_b = acc_ref[i, 0], acc_ref[i, 1]
        o_t = jnp.where(r < half, acc_a / acc_a[half:half + 1, :], acc_b / acc_b[0:1, :])
        o_ref[0, i * tq:(i + 1) * tq, :] = o_t.T.astype(o_ref.dtype)


def _flash(qk, vt, kd, *, batch, seq, n_pairs, q_blk, k_blk, sep, tq=512):
    t = qk.shape[1]
    nq = seq // tq
    w = 2 if sep else 1
    kc = 2 * LANES if kd is not None else LANES
    n_items = nq * (nq - 1) // 2
    unroll = max(u for u in (14, 8, 6, 4, 2) if n_items % u == 0)
    in_specs = [
        pl.BlockSpec((w, seq, LANES), lambda b, hp: (q_blk // w + hp, b, 0)),
        pl.BlockSpec((w, seq, LANES), lambda b, hp: (k_blk // w + hp, b, 0)),
        pl.BlockSpec((1, LANES, seq), lambda b, hp: (hp, 0, b)),
    ]
    args = [qk, qk, vt]
    if kd is not None:
        in_specs.append(pl.BlockSpec((1, seq, LANES), lambda b, hp: (hp, b, 0)))
        args.append(kd)
    return pl.pallas_call(
        functools.partial(_flash_kernel, sep=sep, has_bias=kd is not None, tq=tq, unroll=unroll),
        grid=(batch, n_pairs),
        in_specs=in_specs,
        out_specs=pl.BlockSpec((1, seq, LANES), lambda b, hp: (hp, b, 0)),
        out_shape=jax.ShapeDtypeStruct((n_pairs, t, LANES), BF16),
        scratch_shapes=[
            pltpu.VMEM((2, seq, kc), BF16),
            pltpu.VMEM((2, LANES, seq), BF16),
            pltpu.VMEM((nq, 2, LANES, tq), F32),
            pltpu.VMEM((2, 2, tq, tq), F32),
            pltpu.VMEM((2, 2, 1, tq), F32),
            pltpu.VMEM((nq, 2, 1, tq), F32),
        ],
        compiler_params=_params("parallel", "parallel"),
        name="flash_sep" if sep else "flash_shared",
    )(*args)


def _dil_attn_kernel(*refs):
    n_in = 5 * len(DIL_PATTERNS)
    o_ref, og_ref, lg_ref, band_ref = refs[n_in:]
    i = pl.program_id(2)
    w = DIL_WINDOW_KEYS
    v_first = lax.broadcasted_iota(jnp.int32, (w, LANES), 1) < 64
    qi = lax.broadcasted_iota(jnp.int32, (2 * w, 2 * w), 0) % w
    kk = lax.broadcasted_iota(jnp.int32, (2 * w, 2 * w), 1)
    band = jnp.logical_and(kk >= qi, kk <= qi + w)
    band_ref[0] = jnp.where(band, 0.0, NEG_INF)
    band_ref[1] = jnp.where(jnp.logical_and(band, kk >= jnp.where(i > 0, 0, w)), 0.0, NEG_INF)
    ones = jnp.ones((2 * w, LANES), BF16)
    for g, (_, d) in enumerate(DIL_PATTERNS):
        q_ref, k_ref, kp_ref, v_ref, vp_ref = refs[5 * g:5 * g + 5]
        for r in range(d):
            ls = slice(r * LANES, (r + 1) * LANES)
            for sub in range(DIL_TOKENS // d // w):
                rows = slice(sub * w, (sub + 1) * w)
                q_u = jnp.concatenate([q_ref[0, 0, rows, ls], q_ref[1, 0, rows, ls]], axis=0)
                if sub == 0:
                    k_u = jnp.concatenate([kp_ref[0, 0, :, ls], k_ref[0, 0, 0:w, ls]], axis=0)
                    v_u = jnp.concatenate([vp_ref[0, 0, :, ls], v_ref[0, 0, 0:w, ls]], axis=0)
                else:
                    k_u = k_ref[0, 0, (sub - 1) * w:(sub + 1) * w, ls]
                    v_u = v_ref[0, 0, (sub - 1) * w:(sub + 1) * w, ls]
                s = _dot_nt(q_u, k_u) + band_ref[1 if sub == 0 else 0]
                m = jnp.max(s, axis=-1, keepdims=True)
                p = jnp.exp2(s - m).astype(BF16)
                ov = _dot(p, jnp.concatenate([v_u, ones], axis=1))
                num = jnp.where(v_first, ov[0:w, 0:LANES], ov[w:2 * w, 0:LANES])
                den = jnp.where(v_first, ov[0:w, LANES:2 * LANES], ov[w:2 * w, LANES:2 * LANES])
                top = jnp.where(v_first, jnp.broadcast_to(m[0:w], (w, LANES)),
                                jnp.broadcast_to(m[w:2 * w], (w, LANES)))
                tok = pl.ds(sub * w * d + r, w, stride=d) if d > 1 else pl.ds(sub * w, w)
                og_ref[g, tok, :] = num / den
                lg_ref[g, tok, :] = top + jnp.log(den) * LOG2E
    lse = [lg_ref[g] for g in range(len(DIL_PATTERNS))]
    top = functools.reduce(jnp.maximum, lse)
    e = [jnp.exp2(x - top) for x in lse]
    num = sum(e[g] * og_ref[g] for g in range(len(DIL_PATTERNS)))
    o_ref[0] = (num / sum(e)).astype(o_ref.dtype)


def _dil_attn(groups, *, batch, seq):
    n_pairs = DIL_HEADS // 2
    n_tok_blk = seq // DIL_TOKENS
    in_specs, args = [], []
    for arr, (_, d) in zip(groups, DIL_PATTERNS):
        rows = DIL_TOKENS // d
        sub_per_blk = rows // DIL_WINDOW_KEYS

        def cur(base, rows=rows, d=d):
            return pl.BlockSpec((1, 1, rows, d * LANES), lambda b, hp, i: (base + hp, b, i, 0))

        def prev(base, spb=sub_per_blk, d=d):
            return pl.BlockSpec((1, 1, DIL_WINDOW_KEYS, d * LANES),
                                lambda b, hp, i: (base + hp, b, jnp.maximum(i * spb - 1, 0), 0))

        q_spec = pl.BlockSpec((2, 1, rows, d * LANES), lambda b, hp, i: (hp, b, i, 0))
        in_specs += [q_spec, cur(2 * n_pairs), prev(2 * n_pairs), cur(3 * n_pairs), prev(3 * n_pairs)]
        args += [arr] * 5
    n_g = len(DIL_PATTERNS)
    w2 = 2 * DIL_WINDOW_KEYS
    return pl.pallas_call(
        _dil_attn_kernel,
        grid=(batch, n_pairs, n_tok_blk),
        in_specs=in_specs,
        out_specs=pl.BlockSpec((1, DIL_TOKENS, LANES), lambda b, hp, i: (hp, b * n_tok_blk + i, 0)),
        out_shape=jax.ShapeDtypeStruct((n_pairs, batch * seq, LANES), BF16),
        scratch_shapes=[pltpu.VMEM((n_g, DIL_TOKENS, LANES), F32), pltpu.VMEM((n_g, DIL_TOKENS, LANES), F32),
                        pltpu.VMEM((2, w2, w2), F32)],
        compiler_params=_params("parallel", "parallel", "arbitrary"),
        name="dil_attn",
    )(*args)


def _mlp_kernel(h_ref, o_ref, wo_ref, g_ref, wu_ref, wd_ref, fg_ref, out_ref, xn_ref, *, final_norm, tf):
    a = jnp.concatenate([o_ref[k] for k in range(o_ref.shape[0])], axis=-1)
    h1 = h_ref[...] + _dot(a, wo_ref[...])
    xn_ref[...] = _rms(h1, g_ref[...]).astype(BF16)
    y = h1
    for c in range(wu_ref.shape[1] // tf):
        u = jnp.maximum(_dot(xn_ref[...], wu_ref[:, c * tf:(c + 1) * tf]), 0.0)
        y = y + _dot((u * u).astype(BF16), wd_ref[c * tf:(c + 1) * tf, :])
    out_ref[...] = _rms(y, fg_ref[...]) if final_norm else y


def _mlp(h, o, wo, gain, w_up, w_down, final_gain, *, final_norm, tm=512, tf=1024):
    t, d = h.shape
    fixed = lambda shape: pl.BlockSpec(shape, lambda i: (0, 0), pipeline_mode=pl.Buffered(1))
    return pl.pallas_call(
        functools.partial(_mlp_kernel, final_norm=final_norm, tf=tf),
        grid=(t // tm,),
        in_specs=[
            pl.BlockSpec((tm, d), lambda i: (i, 0)),
            pl.BlockSpec((o.shape[0], tm, LANES), lambda i: (0, i, 0)),
            fixed(wo.shape),
            fixed((1, d)),
            fixed(w_up.shape),
            fixed(w_down.shape),
            fixed((1, d)),
        ],
        out_specs=pl.BlockSpec((tm, d), lambda i: (i, 0)),
        out_shape=jax.ShapeDtypeStruct((t, d), F32),
        scratch_shapes=[pltpu.VMEM((tm, d), BF16)],
        compiler_params=_params("parallel"),
        name="mlp_final" if final_norm else "mlp",
    )(h, o, wo, gain.reshape(1, d), w_up, w_down, final_gain.reshape(1, d))


def _rope_halves(seq, dim):
    inv = 1.0 / (ROPE_THETA ** (jnp.arange(0, dim, 2, dtype=F32) / dim))
    ang = jnp.arange(seq, dtype=F32)[:, None] * inv[None, :]
    return jnp.cos(ang), jnp.sin(ang)


def _mla_tables(seq):
    cos, sin = _rope_halves(seq, MLA_ROPE)
    one = lambda n: jnp.ones((seq, n), F32)
    zero = lambda n: jnp.zeros((seq, n), F32)
    cos_t = jnp.concatenate([one(32), cos, one(16), one(32), cos, one(16)], axis=1)
    sin_t = jnp.concatenate([zero(32), -sin, zero(16), zero(32), sin, zero(16)], axis=1)
    return cos_t, sin_t


def _dil_tables(seq, d, tm):
    cos, sin = _rope_halves(seq, DIL_DIM)
    order = lambda x: x.reshape(seq // tm, tm // d, d, LANES).transpose(0, 2, 1, 3).reshape(seq, LANES)
    return (order(jnp.concatenate([cos] * 4, axis=1)),
            order(jnp.concatenate([-sin, -sin, sin, sin], axis=1)))


def _mla_weights(wq_a, wq_b, wkv_a, wkv_b):
    d = wq_a.shape[0]
    z = lambda r, n: jnp.zeros((r, n), F32)
    kpe = wkv_a[:, MLA_KV_RANK:]
    w_a = jnp.concatenate(
        [wq_a, wkv_a[:, :MLA_KV_RANK], z(d, 32), kpe[:, :16], z(d, 48), kpe[:, 16:], z(d, 16)], axis=1)
    scale = (MLA_NOPE + MLA_ROPE) ** -0.5 * LOG2E
    q3 = wq_b.reshape(MLA_Q_RANK, MLA_HEADS, MLA_NOPE + MLA_ROPE) * scale
    zq = jnp.zeros((MLA_Q_RANK, MLA_HEADS, 16), F32)
    wq = jnp.concatenate(
        [q3[..., :32], q3[..., 64:80], zq, q3[..., 32:64], q3[..., 80:96], zq], axis=-1)
    kv3 = wkv_b.reshape(MLA_KV_RANK, MLA_HEADS, MLA_NOPE + MLA_V)
    zk = jnp.zeros((MLA_KV_RANK, MLA_HEADS, 32), F32)
    wk = jnp.concatenate([kv3[..., :32], zk, kv3[..., 32:64], zk], axis=-1)
    wvt = kv3[..., MLA_NOPE:].reshape(MLA_KV_RANK, -1).T
    return (w_a.astype(BF16), wq.reshape(MLA_Q_RANK, -1).astype(BF16),
            wk.reshape(MLA_KV_RANK, -1).astype(BF16), wvt.astype(BF16))


def _dil_weights(w_qkv):
    d = w_qkv.shape[0]
    n_g = len(DIL_PATTERNS)
    n = DIL_HEADS * DIL_DIM
    w = w_qkv.reshape(d, 3, n_g, n)
    qk = w[:, :2].reshape(d, 2, n_g, n // LANES, 2, 2, 32)
    qk = qk * jnp.array([DIL_DIM ** -0.5 * LOG2E, 1.0], F32).reshape(1, 2, 1, 1, 1, 1, 1)
    qk = qk.transpose(0, 1, 2, 3, 5, 4, 6).reshape(d, 2, n_g, n)
    w = jnp.concatenate([qk, w[:, 2:]], axis=1).astype(BF16)
    return [w[:, :, g].reshape(d, 3 * n) for g in range(n_g)]


def _mla_attn(h, attn_norm, wq_a, q_norm, wq_b, wkv_a, kv_norm, wkv_b, tables, *, batch, seq):
    w_a, wq, wk, wvt = _mla_weights(wq_a, wq_b, wkv_a, wkv_b)
    cq, ckv, kpe = _mla_a(h, attn_norm, w_a, q_norm, kv_norm, *tables, seq=seq)
    qk, vt = _mla_b(cq, ckv, kpe, wq, wk, wvt, *tables, seq=seq)
    return _flash(qk, vt, None, batch=batch, seq=seq, n_pairs=MLA_HEADS // 2,
                  q_blk=0, k_blk=MLA_HEADS, sep=True)


def _fox_attn(h, attn_norm, w_qkv, w_f, b_f, *, batch, seq):
    n = FOX_HEADS * FOX_DIM
    d = w_qkv.shape[0]
    wqk = jnp.concatenate([w_qkv[:, :n] * (FOX_DIM ** -0.5 * LOG2E), w_qkv[:, n:2 * n]], axis=1)
    wvt = w_qkv[:, 2 * n:].T
    wf = jnp.concatenate([w_f, jnp.zeros((d, LANES - FOX_HEADS), F32)], axis=1)
    bf = jnp.concatenate([b_f.astype(F32), jnp.zeros((LANES - FOX_HEADS,), F32)]).reshape(1, LANES)
    qk, vt, kd = _fox_proj(h, attn_norm, wqk.astype(BF16), wvt.astype(BF16), wf.astype(BF16), bf,
                           batch=batch, seq=seq)
    n_pairs = FOX_HEADS // 2
    return _flash(qk, vt, kd, batch=batch, seq=seq, n_pairs=n_pairs, q_blk=0, k_blk=n_pairs, sep=False)


def _dil_attn_layer(h, attn_norm, w_qkv, *, batch, seq, tm=512):
    groups = []
    for w_g, (_, d) in zip(_dil_weights(w_qkv), DIL_PATTERNS):
        groups.append(_dil_proj(h, attn_norm, w_g, *_dil_tables(seq, d, tm),
                                batch=batch, seq=seq, d=d, tm=tm))
    return _dil_attn(groups, batch=batch, seq=seq)


def kernel(x, l0_attn_norm, l0_mla_wq_a, l0_mla_q_norm, l0_mla_wq_b, l0_mla_wkv_a, l0_mla_kv_norm, l0_mla_wkv_b, l0_mla_wo, l0_mlp_norm, l0_w_up, l0_w_down, l1_attn_norm, l1_fox_w_qkv, l1_fox_w_f, l1_fox_b_f, l1_fox_wo, l1_mlp_norm, l1_w_up, l1_w_down, l2_attn_norm, l2_dil_w_qkv, l2_dil_wo, l2_mlp_norm, l2_w_up, l2_w_down, l3_attn_norm, l3_mla_wq_a, l3_mla_q_norm, l3_mla_wq_b, l3_mla_wkv_a, l3_mla_kv_norm, l3_mla_wkv_b, l3_mla_wo, l3_mlp_norm, l3_w_up, l3_w_down, final_norm):
    batch, seq, d = x.shape
    assert seq % DIL_TOKENS == 0 and (batch * seq) % 1024 == 0
    for window, dil in DIL_PATTERNS:
        assert window // dil == DIL_WINDOW_KEYS
    mla_t = _mla_tables(seq)
    kw = dict(batch=batch, seq=seq)
    bf = lambda w: w.astype(BF16)
    h = x.reshape(batch * seq, d)

    o = _mla_attn(h, l0_attn_norm, l0_mla_wq_a, l0_mla_q_norm, l0_mla_wq_b, l0_mla_wkv_a,
                  l0_mla_kv_norm, l0_mla_wkv_b, mla_t, **kw)
    h = _mlp(h, o, bf(l0_mla_wo), l0_mlp_norm, bf(l0_w_up), bf(l0_w_down), final_norm, final_norm=False)

    o = _fox_attn(h, l1_attn_norm, l1_fox_w_qkv, l1_fox_w_f, l1_fox_b_f, **kw)
    h = _mlp(h, o, bf(l1_fox_wo), l1_mlp_norm, bf(l1_w_up), bf(l1_w_down), final_norm, final_norm=False)

    o = _dil_attn_layer(h, l2_attn_norm, l2_dil_w_qkv, **kw)
    h = _mlp(h, o, bf(l2_dil_wo), l2_mlp_norm, bf(l2_w_up), bf(l2_w_down), final_norm, final_norm=False)

    o = _mla_attn(h, l3_attn_norm, l3_mla_wq_a, l3_mla_q_norm, l3_mla_wq_b, l3_mla_wkv_a,
                  l3_mla_kv_norm, l3_mla_wkv_b, mla_t, **kw)
    h = _mlp(h, o, bf(l3_mla_wo), l3_mlp_norm, bf(l3_w_up), bf(l3_w_down), final_norm, final_norm=True)
    return h.reshape(batch, seq, d)
```

```python
import functools
import math

import numpy as np
import jax
import jax.numpy as jnp
from jax import lax
from jax.experimental import pallas as pl
from jax.experimental.pallas import tpu as pltpu

F32 = jnp.float32
BF16 = jnp.bfloat16

LANES = 128
NORM_EPS = 1e-6
NEG_INF = -1e30
ROPE_THETA = 10000.0
LOG2E = math.log2(math.e)

MLA_HEADS = 16
MLA_Q_RANK = 384
MLA_KV_RANK = 256
MLA_NOPE = 64
MLA_ROPE = 32
MLA_V = 64
FOX_HEADS = 16
FOX_DIM = 64
DIL_PATTERNS = ((128, 1), (512, 4), (2048, 16))
DIL_HEADS = 16
DIL_DIM = 64
DIL_WINDOW_KEYS = 128
DIL_TOKENS = 2048
GATE_PARTS = 3

VMEM_LIMIT = 56 * 2**20


def _params(*sem):
    return pltpu.CompilerParams(dimension_semantics=sem, vmem_limit_bytes=VMEM_LIMIT)


def _rms(x, g):
    ms = jnp.mean(x * x, axis=-1, keepdims=True)
    return x * lax.rsqrt(ms + NORM_EPS) * g


def _dot(a, b):
    return jnp.dot(a, b, preferred_element_type=F32)


def _dot_nt(a, b):
    return lax.dot_general(a, b, (((1,), (1,)), ((), ())), preferred_element_type=F32)


def _rope(blk, cos, sin):
    return blk * cos + pltpu.roll(blk, 64, axis=1) * sin


def _colmax(st):
    n, c = st.shape
    return jnp.max(jnp.max(st.reshape(8, n // 8, c), axis=0), axis=0, keepdims=True)


def _split3(x):
    hi = x.astype(BF16)
    r1 = x - hi.astype(F32)
    mid = r1.astype(BF16)
    lo = (r1 - mid.astype(F32)).astype(BF16)
    return hi, mid, lo


def _dil_proj_kernel(x_ref, g_ref, w_ref, cos_ref, sin_ref, o_ref, xs_ref, xn_ref, *, d, tm):
    rpr = tm // d
    nc = x_ref.shape[1] // LANES
    xn = _rms(x_ref[...], g_ref[...])
    if d == 1:
        xn_ref[...] = xn.astype(BF16)
    else:
        for c in range(nc):
            xs_ref[c] = xn[:, c * LANES:(c + 1) * LANES]
        for c in range(nc):
            for r in range(d):
                xn_ref[r * rpr:(r + 1) * rpr, c * LANES:(c + 1) * LANES] = (
                    xs_ref[c, pl.ds(r, rpr, stride=d), :].astype(BF16))

    nb = o_ref.shape[0] // 4
    lane = lax.broadcasted_iota(jnp.int32, (1, LANES), 1)
    first = jnp.where((lane % 64) < 32, 1.0, 0.0)
    for j in range(3):
        acc = _dot(xn_ref[...], w_ref[:, j * nb * LANES:(j + 1) * nb * LANES])
        for c in range(nb):
            for r in range(d):
                rs = slice(r * rpr, (r + 1) * rpr)
                ls = slice(r * LANES, (r + 1) * LANES)
                blk = acc[rs, c * LANES:(c + 1) * LANES]
                if j < 2:
                    blk = _rope(blk, cos_ref[rs, :], sin_ref[rs, :])
                if j == 0:
                    o_ref[2 * c, 0, :, ls] = (blk * first).astype(o_ref.dtype)
                    o_ref[2 * c + 1, 0, :, ls] = (blk * (1.0 - first)).astype(o_ref.dtype)
                else:
                    o_ref[(j + 1) * nb + c, 0, :, ls] = blk.astype(o_ref.dtype)


def _dil_proj(h, gain, w, cos_t, sin_t, *, batch, seq, d, tm):
    t, dm = h.shape
    ns = seq // tm
    nb = 4 * w.shape[1] // (3 * LANES)
    return pl.pallas_call(
        functools.partial(_dil_proj_kernel, d=d, tm=tm),
        grid=(batch, ns),
        in_specs=[
            pl.BlockSpec((tm, dm), lambda b, i: (b * ns + i, 0)),
            pl.BlockSpec((1, dm), lambda b, i: (0, 0)),
            pl.BlockSpec(w.shape, lambda b, i: (0, 0)),
            pl.BlockSpec((tm, LANES), lambda b, i: (i, 0)),
            pl.BlockSpec((tm, LANES), lambda b, i: (i, 0)),
        ],
        out_specs=pl.BlockSpec((nb, 1, tm // d, d * LANES), lambda b, i: (0, b, i, 0)),
        out_shape=jax.ShapeDtypeStruct((nb, batch, seq // d, d * LANES), BF16),
        scratch_shapes=[pltpu.VMEM((dm // LANES, tm, LANES), F32), pltpu.VMEM((tm, dm), BF16)],
        compiler_params=_params("parallel", "parallel"),
        name="dil_proj_d%d" % d,
    )(h, gain.reshape(1, dm), w, cos_t, sin_t)


def _mla_proj_kernel(x_ref, g_ref, wa_ref, qn_ref, kvn_ref, wq_ref, wk_ref, wvt_ref, cos_ref, sin_ref,
                     qk_ref, vt_ref):
    cos, sin = cos_ref[...], sin_ref[...]
    xn = _rms(x_ref[...], g_ref[...]).astype(BF16)
    y = _dot(xn, wa_ref[...])
    cq = _rms(y[:, :MLA_Q_RANK], qn_ref[...]).astype(BF16)
    ckv = _rms(y[:, MLA_Q_RANK:MLA_Q_RANK + MLA_KV_RANK], kvn_ref[...]).astype(BF16)
    kpe = _rope(y[:, MLA_Q_RANK + MLA_KV_RANK:], cos, sin)
    q = _dot(cq, wq_ref[...])
    for c in range(MLA_HEADS):
        qk_ref[c] = _rope(q[:, c * LANES:(c + 1) * LANES], cos, sin).astype(BF16)
    k = _dot(ckv, wk_ref[...])
    for c in range(MLA_HEADS):
        qk_ref[MLA_HEADS + c] = (k[:, c * LANES:(c + 1) * LANES] + kpe).astype(BF16)
    vt = _dot_nt(wvt_ref[...], ckv)
    for c in range(MLA_HEADS // 2):
        vt_ref[c] = vt[c * LANES:(c + 1) * LANES, :].astype(BF16)


def _mla_proj(h, gain, w_a, q_norm, kv_norm, wq, wk, wvt, cos_t, sin_t, *, seq, tm=512):
    t, d = h.shape
    ns = seq // tm
    fixed = lambda shape: pl.BlockSpec(shape, lambda i: (0, 0))
    tab = pl.BlockSpec((tm, LANES), lambda i: (i % ns, 0))
    return pl.pallas_call(
        _mla_proj_kernel,
        grid=(t // tm,),
        in_specs=[
            pl.BlockSpec((tm, d), lambda i: (i, 0)),
            fixed((1, d)),
            fixed(w_a.shape),
            fixed((1, MLA_Q_RANK)),
            fixed((1, MLA_KV_RANK)),
            fixed(wq.shape),
            fixed(wk.shape),
            fixed(wvt.shape),
            tab,
            tab,
        ],
        out_specs=[
            pl.BlockSpec((2 * MLA_HEADS, tm, LANES), lambda i: (0, i, 0)),
            pl.BlockSpec((MLA_HEADS // 2, LANES, tm), lambda i: (0, 0, i)),
        ],
        out_shape=[
            jax.ShapeDtypeStruct((2 * MLA_HEADS, t, LANES), BF16),
            jax.ShapeDtypeStruct((MLA_HEADS // 2, LANES, t), BF16),
        ],
        compiler_params=_params("parallel"),
        name="mla_proj",
    )(h, gain.reshape(1, d), w_a, q_norm.reshape(1, -1), kv_norm.reshape(1, -1), wq, wk, wvt, cos_t, sin_t)


def _fox_proj_kernel(x_ref, g_ref, wqk_ref, wvt_ref, wf_ref, bf_ref, tri_ref, scat_ref,
                     qk_ref, vt_ref, kd_ref, carry_ref, *, tm):
    @pl.when(pl.program_id(1) == 0)
    def _():
        carry_ref[...] = jnp.zeros_like(carry_ref)

    xn = _rms(x_ref[...], g_ref[...]).astype(BF16)
    qk = _dot(xn, wqk_ref[...])
    for c in range(qk_ref.shape[0]):
        qk_ref[c] = qk[:, c * LANES:(c + 1) * LANES].astype(BF16)
    vt = _dot_nt(wvt_ref[...], xn)
    for c in range(vt_ref.shape[0]):
        vt_ref[c] = vt[c * LANES:(c + 1) * LANES, :].astype(BF16)

    z = _dot(xn, wf_ref[...]) + bf_ref[...]
    logf = jnp.minimum(z, 0.0) - jnp.log(1.0 + jnp.exp(-jnp.abs(z)))
    tri = tri_ref[...]
    cum = carry_ref[...] + sum(_dot(tri, part) for part in _split3(logf))
    carry_ref[...] = cum[tm - 1:tm, :]
    kd = sum(_dot(part, scat_ref[p]) for p, part in enumerate(_split3(cum * (-LOG2E))))
    for c in range(kd_ref.shape[0]):
        kd_ref[c] = kd[:, c * LANES:(c + 1) * LANES].astype(BF16)


def _fox_proj(h, gain, wqk, wvt, wf, bf, *, batch, seq, tm=512):
    t, d = h.shape
    ns = seq // tm
    n_pairs = FOX_HEADS // 2
    tri = jnp.tril(jnp.ones((tm, tm), BF16))
    scat = np.zeros((GATE_PARTS, LANES, n_pairs * LANES), np.float32)
    for p in range(GATE_PARTS):
        for hd in range(FOX_HEADS):
            scat[p, hd, (hd // 2) * LANES + GATE_PARTS * (hd % 2) + p] = 1.0
    fixed2 = lambda b, i: (0, 0)
    rows = lambda b, i: (0, b * ns + i, 0)
    return pl.pallas_call(
        functools.partial(_fox_proj_kernel, tm=tm),
        grid=(batch, ns),
        in_specs=[
            pl.BlockSpec((tm, d), lambda b, i: (b * ns + i, 0)),
            pl.BlockSpec((1, d), fixed2),
            pl.BlockSpec(wqk.shape, fixed2),
            pl.BlockSpec(wvt.shape, fixed2),
            pl.BlockSpec(wf.shape, fixed2),
            pl.BlockSpec((1, LANES), fixed2),
            pl.BlockSpec((tm, tm), fixed2),
            pl.BlockSpec(scat.shape, lambda b, i: (0, 0, 0)),
        ],
        out_specs=[
            pl.BlockSpec((2 * n_pairs, tm, LANES), rows),
            pl.BlockSpec((n_pairs, LANES, tm), lambda b, i: (0, 0, b * ns + i)),
            pl.BlockSpec((n_pairs, tm, LANES), rows),
        ],
        out_shape=[
            jax.ShapeDtypeStruct((2 * n_pairs, t, LANES), BF16),
            jax.ShapeDtypeStruct((n_pairs, LANES, t), BF16),
            jax.ShapeDtypeStruct((n_pairs, t, LANES), BF16),
        ],
        scratch_shapes=[pltpu.VMEM((1, LANES), F32)],
        compiler_params=_params("arbitrary", "arbitrary"),
        name="fox_proj",
    )(h, gain.reshape(1, d), wqk, wvt, wf, bf, tri, jnp.asarray(scat, BF16))


def _flash_kernel(*refs, sep, has_bias, tq, unroll):
    if has_bias:
        q_ref, k_ref, vt_ref, kd_ref, o_ref, qs_ref, vta_ref, acc_ref, s_ref, mx_ref, m_ref = refs
    else:
        q_ref, k_ref, vt_ref, o_ref, qs_ref, vta_ref, acc_ref, s_ref, mx_ref, m_ref = refs
        kd_ref = None
    half = LANES // 2
    nq = acc_ref.shape[0]
    seq = nq * tq

    vt = vt_ref[0]
    r = lax.broadcasted_iota(jnp.int32, vt.shape, 0)
    one = jnp.ones_like(vt)
    vta_ref[0] = jnp.where(r < half, vt, one)
    vta_ref[1] = jnp.where(r < half, one, vt)
    if not sep or has_bias:
        lane = lax.broadcasted_iota(jnp.int32, (seq, LANES), 1)
        for hh in range(2):
            if sep:
                q_h = q_ref[hh]
            else:
                q = q_ref[0]
                own = (lane < half) if hh == 0 else (lane >= half)
                q_h = jnp.where(own, q, jnp.zeros_like(q))
            qs_ref[hh, :, 0:LANES] = q_h
            if has_bias:
                sel = jnp.logical_and(lane >= GATE_PARTS * hh, lane < GATE_PARTS * (hh + 1))
                qs_ref[hh, :, LANES:2 * LANES] = jnp.where(sel, 1.0, 0.0).astype(BF16)
    src = qs_ref if (not sep or has_bias) else q_ref

    def keys(hh, start, n):
        k_blk = k_ref[hh if sep else 0, pl.ds(start, n), :]
        if has_bias:
            k_blk = jnp.concatenate([k_blk, kd_ref[0, pl.ds(start, n), :]], axis=1)
        return k_blk

    def scores(i, kb, slot):
        qstart, kstart = pl.multiple_of(i * tq, tq), pl.multiple_of(kb * tq, tq)
        for hh in range(2):
            st = _dot_nt(keys(hh, kstart, tq), src[hh, pl.ds(qstart, tq), :])
            s_ref[slot, hh] = st
            mx_ref[slot, hh] = _colmax(st)

    def consume(i, kb, slot):
        kstart = pl.multiple_of(kb * tq, tq)
        for hh in range(2):
            m_old = m_ref[i, hh]
            m_new = jnp.maximum(m_old, mx_ref[slot, hh])
            m_ref[i, hh] = m_new
            pt = jnp.exp2(s_ref[slot, hh] - m_new).astype(BF16)
            pv = _dot(vta_ref[hh, :, pl.ds(kstart, tq)], pt)
            acc_ref[i, hh] = jnp.exp2(m_old - m_new) * acc_ref[i, hh] + pv

    hq = tq // 2
    halves = ((0, hq), (hq, tq))

    def scores_diag(i, slot):
        for hh in range(2):
            for q0, nk in halves:
                st = _dot_nt(keys(hh, i * tq, nk), src[hh, i * tq + q0:i * tq + q0 + hq, :])
                key = lax.broadcasted_iota(jnp.int32, (nk, hq), 0)
                qry = lax.broadcasted_iota(jnp.int32, (nk, hq), 1) + q0
                st = jnp.where(key <= qry, st, NEG_INF)
                s_ref[slot, hh, 0:nk, q0:q0 + hq] = st
                mx_ref[slot, hh, :, q0:q0 + hq] = _colmax(st)

    def consume_diag(i, slot):
        for hh in range(2):
            for q0, nk in halves:
                m_new = mx_ref[slot, hh, :, q0:q0 + hq]
                m_ref[i, hh, :, q0:q0 + hq] = m_new
                pt = jnp.exp2(s_ref[slot, hh, 0:nk, q0:q0 + hq] - m_new).astype(BF16)
                acc_ref[i, hh, :, q0:q0 + hq] = _dot(vta_ref[hh, :, i * tq:i * tq + nk], pt)

    scores_diag(0, 0)
    for i in range(nq):
        if i + 1 < nq:
            scores_diag(i + 1, (i + 1) % 2)
        elif nq > 1:
            scores(1, 0, (i + 1) % 2)
        consume_diag(i, i % 2)

    def advance(i, kb):
        last = kb + 1 == i
        return jnp.where(last, i + 1, i), jnp.where(last, 0, kb + 1)

    def trip(_, carry):
        i, kb = carry
        for u in range(unroll):
            ni, nkb = advance(i, kb)
            scores(jnp.minimum(ni, nq - 1), nkb, (nq + u + 1) % 2)
            consume(i, kb, (nq + u) % 2)
            i, kb = ni, nkb
        return i, kb

    n_items = nq * (nq - 1) // 2
    assert unroll % 2 == 0 and n_items % unroll == 0
    lax.fori_loop(0, n_items // unroll, trip, (jnp.int32(1), jnp.int32(0)))

    r = lax.broadcasted_iota(jnp.int32, (LANES, tq), 0)
    for i in range(nq):
        acc_a, acc_b = acc_ref[i, 0], acc_ref[i, 1]
        o_t = jnp.where(r < half, acc_a / acc_a[half:half + 1, :], acc_b / acc_b[0:1, :])
        o_ref[0, i * tq:(i + 1) * tq, :] = o_t.T.astype(o_ref.dtype)


def _flash(qk, vt, kd, *, batch, seq, n_pairs, q_blk, k_blk, sep, tq=512):
    t = qk.shape[1]
    nq = seq // tq
    w = 2 if sep else 1
    kc = 2 * LANES if kd is not None else LANES
    n_items = nq * (nq - 1) // 2
    unroll = max(u for u in (14, 8, 6, 4, 2) if n_items % u == 0)
    in_specs = [
        pl.BlockSpec((w, seq, LANES), lambda b, hp: (q_blk // w + hp, b, 0)),
        pl.BlockSpec((w, seq, LANES), lambda b, hp: (k_blk // w + hp, b, 0)),
        pl.BlockSpec((1, LANES, seq), lambda b, hp: (hp, 0, b)),
    ]
    args = [qk, qk, vt]
    if kd is not None:
        in_specs.append(pl.BlockSpec((1, seq, LANES), lambda b, hp: (hp, b, 0)))
        args.append(kd)
    return pl.pallas_call(
        functools.partial(_flash_kernel, sep=sep, has_bias=kd is not None, tq=tq, unroll=unroll),
        grid=(batch, n_pairs),
        in_specs=in_specs,
        out_specs=pl.BlockSpec((1, seq, LANES), lambda b, hp: (hp, b, 0)),
        out_shape=jax.ShapeDtypeStruct((n_pairs, t, LANES), BF16),
        scratch_shapes=[
            pltpu.VMEM((2, seq, kc), BF16),
            pltpu.VMEM((2, LANES, seq), BF16),
            pltpu.VMEM((nq, 2, LANES, tq), F32),
            pltpu.VMEM((2, 2, tq, tq), F32),
            pltpu.VMEM((2, 2, 1, tq), F32),
            pltpu.VMEM((nq, 2, 1, tq), F32),
        ],
        compiler_params=_params("parallel", "parallel"),
        name="flash_sep" if sep else "flash_shared",
    )(*args)


def _dil_attn_kernel(*refs):
    n_in = 5 * len(DIL_PATTERNS)
    o_ref, og_ref, lg_ref, band_ref = refs[n_in:]
    i = pl.program_id(2)
    w = DIL_WINDOW_KEYS
    v_first = lax.broadcasted_iota(jnp.int32, (w, LANES), 1) < 64
    qi = lax.broadcasted_iota(jnp.int32, (2 * w, 2 * w), 0) % w
    kk = lax.broadcasted_iota(jnp.int32, (2 * w, 2 * w), 1)
    band = jnp.logical_and(kk >= qi, kk <= qi + w)
    band_ref[0] = jnp.where(band, 0.0, NEG_INF)
    band_ref[1] = jnp.where(jnp.logical_and(band, kk >= jnp.where(i > 0, 0, w)), 0.0, NEG_INF)
    ones = jnp.ones((2 * w, LANES), BF16)
    for g, (_, d) in enumerate(DIL_PATTERNS):
        q_ref, k_ref, kp_ref, v_ref, vp_ref = refs[5 * g:5 * g + 5]
        for r in range(d):
            ls = slice(r * LANES, (r + 1) * LANES)
            for sub in range(DIL_TOKENS // d // w):
                rows = slice(sub * w, (sub + 1) * w)
                q_u = jnp.concatenate([q_ref[0, 0, rows, ls], q_ref[1, 0, rows, ls]], axis=0)
                if sub == 0:
                    k_u = jnp.concatenate([kp_ref[0, 0, :, ls], k_ref[0, 0, 0:w, ls]], axis=0)
                    v_u = jnp.concatenate([vp_ref[0, 0, :, ls], v_ref[0, 0, 0:w, ls]], axis=0)
                else:
                    k_u = k_ref[0, 0, (sub - 1) * w:(sub + 1) * w, ls]
                    v_u = v_ref[0, 0, (sub - 1) * w:(sub + 1) * w, ls]
                s = _dot_nt(q_u, k_u) + band_ref[1 if sub == 0 else 0]
                m = jnp.max(s, axis=-1, keepdims=True)
                p = jnp.exp2(s - m).astype(BF16)
                ov = _dot(p, jnp.concatenate([v_u, ones], axis=1))
                num = jnp.where(v_first, ov[0:w, 0:LANES], ov[w:2 * w, 0:LANES])
                den = jnp.where(v_first, ov[0:w, LANES:2 * LANES], ov[w:2 * w, LANES:2 * LANES])
                top = jnp.where(v_first, jnp.broadcast_to(m[0:w], (w, LANES)),
                                jnp.broadcast_to(m[w:2 * w], (w, LANES)))
                tok = pl.ds(sub * w * d + r, w, stride=d) if d > 1 else pl.ds(sub * w, w)
                og_ref[g, tok, :] = num / den
                lg_ref[g, tok, :] = top + jnp.log(den) * LOG2E
    lse = [lg_ref[g] for g in range(len(DIL_PATTERNS))]
    top = functools.reduce(jnp.maximum, lse)
    e = [jnp.exp2(x - top) for x in lse]
    num = sum(e[g] * og_ref[g] for g in range(len(DIL_PATTERNS)))
    o_ref[0] = (num / sum(e)).astype(o_ref.dtype)


def _dil_attn(groups, *, batch, seq):
    n_pairs = DIL_HEADS // 2
    n_tok_blk = seq // DIL_TOKENS
    in_specs, args = [], []
    for arr, (_, d) in zip(groups, DIL_PATTERNS):
        rows = DIL_TOKENS // d
        sub_per_blk = rows // DIL_WINDOW_KEYS

        def cur(base, rows=rows, d=d):
            return pl.BlockSpec((1, 1, rows, d * LANES), lambda b, hp, i: (base + hp, b, i, 0))

        def prev(base, spb=sub_per_blk, d=d):
            return pl.BlockSpec((1, 1, DIL_WINDOW_KEYS, d * LANES),
                                lambda b, hp, i: (base + hp, b, jnp.maximum(i * spb - 1, 0), 0))

        q_spec = pl.BlockSpec((2, 1, rows, d * LANES), lambda b, hp, i: (hp, b, i, 0))
        in_specs += [q_spec, cur(2 * n_pairs), prev(2 * n_pairs), cur(3 * n_pairs), prev(3 * n_pairs)]
        args += [arr] * 5
    n_g = len(DIL_PATTERNS)
    w2 = 2 * DIL_WINDOW_KEYS
    return pl.pallas_call(
        _dil_attn_kernel,
        grid=(batch, n_pairs, n_tok_blk),
        in_specs=in_specs,
        out_specs=pl.BlockSpec((1, DIL_TOKENS, LANES), lambda b, hp, i: (hp, b * n_tok_blk + i, 0)),
        out_shape=jax.ShapeDtypeStruct((n_pairs, batch * seq, LANES), BF16),
        scratch_shapes=[pltpu.VMEM((n_g, DIL_TOKENS, LANES), F32), pltpu.VMEM((n_g, DIL_TOKENS, LANES), F32),
                        pltpu.VMEM((2, w2, w2), F32)],
        compiler_params=_params("parallel", "parallel", "arbitrary"),
        name="dil_attn",
    )(*args)


def _mlp_kernel(h_ref, o_ref, wo_ref, g_ref, wu_ref, wd_ref, fg_ref, out_ref, xn_ref, *, final_norm, tf):
    a = jnp.concatenate([o_ref[k] for k in range(o_ref.shape[0])], axis=-1)
    h1 = h_ref[...] + _dot(a, wo_ref[...])
    xn_ref[...] = _rms(h1, g_ref[...]).astype(BF16)
    y = h1
    for c in range(wu_ref.shape[1] // tf):
        u = jnp.maximum(_dot(xn_ref[...], wu_ref[:, c * tf:(c + 1) * tf]), 0.0)
        y = y + _dot((u * u).astype(BF16), wd_ref[c * tf:(c + 1) * tf, :])
    out_ref[...] = _rms(y, fg_ref[...]) if final_norm else y


def _mlp(h, o, wo, gain, w_up, w_down, final_gain, *, final_norm, tm=512, tf=1024):
    t, d = h.shape
    fixed = lambda shape: pl.BlockSpec(shape, lambda i: (0, 0), pipeline_mode=pl.Buffered(1))
    return pl.pallas_call(
        functools.partial(_mlp_kernel, final_norm=final_norm, tf=tf),
        grid=(t // tm,),
        in_specs=[
            pl.BlockSpec((tm, d), lambda i: (i, 0)),
            pl.BlockSpec((o.shape[0], tm, LANES), lambda i: (0, i, 0)),
            fixed(wo.shape),
            fixed((1, d)),
            fixed(w_up.shape),
            fixed(w_down.shape),
            fixed((1, d)),
        ],
        out_specs=pl.BlockSpec((tm, d), lambda i: (i, 0)),
        out_shape=jax.ShapeDtypeStruct((t, d), F32),
        scratch_shapes=[pltpu.VMEM((tm, d), BF16)],
        compiler_params=_params("parallel"),
        name="mlp_final" if final_norm else "mlp",
    )(h, o, wo, gain.reshape(1, d), w_up, w_down, final_gain.reshape(1, d))


def _rope_halves(seq, dim):
    inv = 1.0 / (ROPE_THETA ** (jnp.arange(0, dim, 2, dtype=F32) / dim))
    ang = jnp.arange(seq, dtype=F32)[:, None] * inv[None, :]
    return jnp.cos(ang), jnp.sin(ang)


def _mla_tables(seq):
    cos, sin = _rope_halves(seq, MLA_ROPE)
    one = lambda n: jnp.ones((seq, n), F32)
    zero = lambda n: jnp.zeros((seq, n), F32)
    cos_t = jnp.concatenate([one(32), cos, one(16), one(32), cos, one(16)], axis=1)
    sin_t = jnp.concatenate([zero(32), -sin, zero(16), zero(32), sin, zero(16)], axis=1)
    return cos_t, sin_t


def _dil_tables(seq, d, tm):
    cos, sin = _rope_halves(seq, DIL_DIM)
    order = lambda x: x.reshape(seq // tm, tm // d, d, LANES).transpose(0, 2, 1, 3).reshape(seq, LANES)
    return (order(jnp.concatenate([cos] * 4, axis=1)),
            order(jnp.concatenate([-sin, -sin, sin, sin], axis=1)))


def _mla_weights(wq_a, wq_b, wkv_a, wkv_b):
    d = wq_a.shape[0]
    z = lambda r, n: jnp.zeros((r, n), F32)
    kpe = wkv_a[:, MLA_KV_RANK:]
    w_a = jnp.concatenate(
        [wq_a, wkv_a[:, :MLA_KV_RANK], z(d, 32), kpe[:, :16], z(d, 48), kpe[:, 16:], z(d, 16)], axis=1)
    scale = (MLA_NOPE + MLA_ROPE) ** -0.5 * LOG2E
    q3 = wq_b.reshape(MLA_Q_RANK, MLA_HEADS, MLA_NOPE + MLA_ROPE) * scale
    zq = jnp.zeros((MLA_Q_RANK, MLA_HEADS, 16), F32)
    wq = jnp.concatenate(
        [q3[..., :32], q3[..., 64:80], zq, q3[..., 32:64], q3[..., 80:96], zq], axis=-1)
    kv3 = wkv_b.reshape(MLA_KV_RANK, MLA_HEADS, MLA_NOPE + MLA_V)
    zk = jnp.zeros((MLA_KV_RANK, MLA_HEADS, 32), F32)
    wk = jnp.concatenate([kv3[..., :32], zk, kv3[..., 32:64], zk], axis=-1)
    wvt = kv3[..., MLA_NOPE:].reshape(MLA_KV_RANK, -1).T
    return (w_a.astype(BF16), wq.reshape(MLA_Q_RANK, -1).astype(BF16),
            wk.reshape(MLA_KV_RANK, -1).astype(BF16), wvt.astype(BF16))


def _dil_weights(w_qkv):
    d = w_qkv.shape[0]
    n_g = len(DIL_PATTERNS)
    n = DIL_HEADS * DIL_DIM
    w = w_qkv.reshape(d, 3, n_g, n)
    qk = w[:, :2].reshape(d, 2, n_g, n // LANES, 2, 2, 32)
    qk = qk * jnp.array([DIL_DIM ** -0.5 * LOG2E, 1.0], F32).reshape(1, 2, 1, 1, 1, 1, 1)
    qk = qk.transpose(0, 1, 2, 3, 5, 4, 6).reshape(d, 2, n_g, n)
    w = jnp.concatenate([qk, w[:, 2:]], axis=1).astype(BF16)
    return [w[:, :, g].reshape(d, 3 * n) for g in range(n_g)]


def _mla_attn(h, attn_norm, wq_a, q_norm, wq_b, wkv_a, kv_norm, wkv_b, tables, *, batch, seq):
    w_a, wq, wk, wvt = _mla_weights(wq_a, wq_b, wkv_a, wkv_b)
    qk, vt = _mla_proj(h, attn_norm, w_a, q_norm, kv_norm, wq, wk, wvt, *tables, seq=seq)
    return _flash(qk, vt, None, batch=batch, seq=seq, n_pairs=MLA_HEADS // 2,
                  q_blk=0, k_blk=MLA_HEADS, sep=True)


def _fox_attn(h, attn_norm, w_qkv, w_f, b_f, *, batch, seq):
    n = FOX_HEADS * FOX_DIM
    d = w_qkv.shape[0]
    wqk = jnp.concatenate([w_qkv[:, :n] * (FOX_DIM ** -0.5 * LOG2E), w_qkv[:, n:2 * n]], axis=1)
    wvt = w_qkv[:, 2 * n:].T
    wf = jnp.concatenate([w_f, jnp.zeros((d, LANES - FOX_HEADS), F32)], axis=1)
    bf = jnp.concatenate([b_f.astype(F32), jnp.zeros((LANES - FOX_HEADS,), F32)]).reshape(1, LANES)
    qk, vt, kd = _fox_proj(h, attn_norm, wqk.astype(BF16), wvt.astype(BF16), wf.astype(BF16), bf,
                           batch=batch, seq=seq)
    n_pairs = FOX_HEADS // 2
    return _flash(qk, vt, kd, batch=batch, seq=seq, n_pairs=n_pairs, q_blk=0, k_blk=n_pairs, sep=False)


def _dil_attn_layer(h, attn_norm, w_qkv, *, batch, seq, tm=512):
    groups = []
    for w_g, (_, d) in zip(_dil_weights(w_qkv), DIL_PATTERNS):
        groups.append(_dil_proj(h, attn_norm, w_g, *_dil_tables(seq, d, tm),
                                batch=batch, seq=seq, d=d, tm=tm))
    return _dil_attn(groups, batch=batch, seq=seq)


def kernel(x, l0_attn_norm, l0_mla_wq_a, l0_mla_q_norm, l0_mla_wq_b, l0_mla_wkv_a, l0_mla_kv_norm, l0_mla_wkv_b, l0_mla_wo, l0_mlp_norm, l0_w_up, l0_w_down, l1_attn_norm, l1_fox_w_qkv, l1_fox_w_f, l1_fox_b_f, l1_fox_wo, l1_mlp_norm, l1_w_up, l1_w_down, l2_attn_norm, l2_dil_w_qkv, l2_dil_wo, l2_mlp_norm, l2_w_up, l2_w_down, l3_attn_norm, l3_mla_wq_a, l3_mla_q_norm, l3_mla_wq_b, l3_mla_wkv_a, l3_mla_kv_norm, l3_mla_wkv_b, l3_mla_wo, l3_mlp_norm, l3_w_up, l3_w_down, final_norm):
    batch, seq, d = x.shape
    assert seq % DIL_TOKENS == 0 and (batch * seq) % 1024 == 0
    for window, dil in DIL_PATTERNS:
        assert window // dil == DIL_WINDOW_KEYS
    mla_t = _mla_tables(seq)
    kw = dict(batch=batch, seq=seq)
    bf = lambda w: w.astype(BF16)
    h = x.reshape(batch * seq, d)

    o = _mla_attn(h, l0_attn_norm, l0_mla_wq_a, l0_mla_q_norm, l0_mla_wq_b, l0_mla_wkv_a,
                  l0_mla_kv_norm, l0_mla_wkv_b, mla_t, **kw)
    h = _mlp(h, o, bf(l0_mla_wo), l0_mlp_norm, bf(l0_w_up), bf(l0_w_down), final_norm, final_norm=False)

    o = _fox_attn(h, l1_attn_norm, l1_fox_w_qkv, l1_fox_w_f, l1_fox_b_f, **kw)
    h = _mlp(h, o, bf(l1_fox_wo), l1_mlp_norm, bf(l1_w_up), bf(l1_w_down), final_norm, final_norm=False)

    o = _dil_attn_layer(h, l2_attn_norm, l2_dil_w_qkv, **kw)
    h = _mlp(h, o, bf(l2_dil_wo), l2_mlp_norm, bf(l2_w_up), bf(l2_w_down), final_norm, final_norm=False)

    o = _mla_attn(h, l3_attn_norm, l3_mla_wq_a, l3_mla_q_norm, l3_mla_wq_b, l3_mla_wkv_a,
                  l3_mla_kv_norm, l3_mla_wkv_b, mla_t, **kw)
    h = _mlp(h, o, bf(l3_mla_wo), l3_mlp_norm, bf(l3_w_up), bf(l3_w_down), final_norm, final_norm=True)
    return h.reshape(batch, seq, d)
```

```python
import functools
import math

import numpy as np
import jax
import jax.numpy as jnp
from jax import lax
from jax.experimental import pallas as pl
from jax.experimental.pallas import tpu as pltpu

F32 = jnp.float32
BF16 = jnp.bfloat16

LANES = 128
NORM_EPS = 1e-6
NEG_INF = -1e30
ROPE_THETA = 10000.0
LOG2E = math.log2(math.e)

MLA_HEADS = 16
MLA_Q_RANK = 384
MLA_KV_RANK = 256
MLA_NOPE = 64
MLA_ROPE = 32
MLA_V = 64
FOX_HEADS = 16
FOX_DIM = 64
DIL_PATTERNS = ((128, 1), (512, 4), (2048, 16))
DIL_HEADS = 16
DIL_DIM = 64
DIL_WINDOW_KEYS = 128
DIL_TOKENS = 2048
GATE_PARTS = 3

VMEM_LIMIT = 56 * 2**20


def _params(*sem):
    return pltpu.CompilerParams(dimension_semantics=sem, vmem_limit_bytes=VMEM_LIMIT)


def _rms(x, g):
    ms = jnp.mean(x * x, axis=-1, keepdims=True)
    return x * lax.rsqrt(ms + NORM_EPS) * g


def _dot(a, b):
    return jnp.dot(a, b, preferred_element_type=F32)


def _dot_nt(a, b):
    return lax.dot_general(a, b, (((1,), (1,)), ((), ())), preferred_element_type=F32)


def _rope(blk, cos, sin):
    return blk * cos + pltpu.roll(blk, 64, axis=1) * sin


def _colmax(st):
    n, c = st.shape
    return jnp.max(jnp.max(st.reshape(8, n // 8, c), axis=0), axis=0, keepdims=True)


def _split3(x):
    hi = x.astype(BF16)
    r1 = x - hi.astype(F32)
    mid = r1.astype(BF16)
    lo = (r1 - mid.astype(F32)).astype(BF16)
    return hi, mid, lo


def _dil_proj_kernel(x_ref, g_ref, w_ref, cos_ref, sin_ref, o_ref, xs_ref, xn_ref, *, d, tm):
    rpr = tm // d
    nc = x_ref.shape[1] // LANES
    xn = _rms(x_ref[...], g_ref[...])
    if d == 1:
        xn_ref[...] = xn.astype(BF16)
    else:
        for c in range(nc):
            xs_ref[c] = xn[:, c * LANES:(c + 1) * LANES]
        for c in range(nc):
            for r in range(d):
                xn_ref[r * rpr:(r + 1) * rpr, c * LANES:(c + 1) * LANES] = (
                    xs_ref[c, pl.ds(r, rpr, stride=d), :].astype(BF16))

    nb = o_ref.shape[0] // 4
    lane = lax.broadcasted_iota(jnp.int32, (1, LANES), 1)
    first = jnp.where((lane % 64) < 32, 1.0, 0.0)
    for j in range(3):
        acc = _dot(xn_ref[...], w_ref[:, j * nb * LANES:(j + 1) * nb * LANES])
        for c in range(nb):
            for r in range(d):
                rs = slice(r * rpr, (r + 1) * rpr)
                ls = slice(r * LANES, (r + 1) * LANES)
                blk = acc[rs, c * LANES:(c + 1) * LANES]
                if j < 2:
                    blk = _rope(blk, cos_ref[rs, :], sin_ref[rs, :])
                if j == 0:
                    o_ref[2 * c, 0, :, ls] = (blk * first).astype(o_ref.dtype)
                    o_ref[2 * c + 1, 0, :, ls] = (blk * (1.0 - first)).astype(o_ref.dtype)
                else:
                    o_ref[(j + 1) * nb + c, 0, :, ls] = blk.astype(o_ref.dtype)


def _dil_proj(h, gain, w, cos_t, sin_t, *, batch, seq, d, tm):
    t, dm = h.shape
    ns = seq // tm
    nb = 4 * w.shape[1] // (3 * LANES)
    return pl.pallas_call(
        functools.partial(_dil_proj_kernel, d=d, tm=tm),
        grid=(batch, ns),
        in_specs=[
            pl.BlockSpec((tm, dm), lambda b, i: (b * ns + i, 0)),
            pl.BlockSpec((1, dm), lambda b, i: (0, 0)),
            pl.BlockSpec(w.shape, lambda b, i: (0, 0)),
            pl.BlockSpec((tm, LANES), lambda b, i: (i, 0)),
            pl.BlockSpec((tm, LANES), lambda b, i: (i, 0)),
        ],
        out_specs=pl.BlockSpec((nb, 1, tm // d, d * LANES), lambda b, i: (0, b, i, 0)),
        out_shape=jax.ShapeDtypeStruct((nb, batch, seq // d, d * LANES), BF16),
        scratch_shapes=[pltpu.VMEM((dm // LANES, tm, LANES), F32), pltpu.VMEM((tm, dm), BF16)],
        compiler_params=_params("parallel", "parallel"),
        name="dil_proj_d%d" % d,
    )(h, gain.reshape(1, dm), w, cos_t, sin_t)


def _mla_proj_kernel(x_ref, g_ref, wa_ref, qn_ref, kvn_ref, wq_ref, wk_ref, wvt_ref, cos_ref, sin_ref,
                     qk_ref, vt_ref):
    cos, sin = cos_ref[...], sin_ref[...]
    xn = _rms(x_ref[...], g_ref[...]).astype(BF16)
    y = _dot(xn, wa_ref[...])
    cq = _rms(y[:, :MLA_Q_RANK], qn_ref[...]).astype(BF16)
    ckv = _rms(y[:, MLA_Q_RANK:MLA_Q_RANK + MLA_KV_RANK], kvn_ref[...]).astype(BF16)
    kpe = _rope(y[:, MLA_Q_RANK + MLA_KV_RANK:], cos, sin)
    q = _dot(cq, wq_ref[...])
    for c in range(MLA_HEADS):
        qk_ref[c] = _rope(q[:, c * LANES:(c + 1) * LANES], cos, sin).astype(BF16)
    k = _dot(ckv, wk_ref[...])
    for c in range(MLA_HEADS):
        qk_ref[MLA_HEADS + c] = (k[:, c * LANES:(c + 1) * LANES] + kpe).astype(BF16)
    vt = _dot_nt(wvt_ref[...], ckv)
    for c in range(MLA_HEADS // 2):
        vt_ref[c] = vt[c * LANES:(c + 1) * LANES, :].astype(BF16)


def _mla_proj(h, gain, w_a, q_norm, kv_norm, wq, wk, wvt, cos_t, sin_t, *, seq, tm=512):
    t, d = h.shape
    ns = seq // tm
    fixed = lambda shape: pl.BlockSpec(shape, lambda i: (0, 0))
    tab = pl.BlockSpec((tm, LANES), lambda i: (i % ns, 0))
    return pl.pallas_call(
        _mla_proj_kernel,
        grid=(t // tm,),
        in_specs=[
            pl.BlockSpec((tm, d), lambda i: (i, 0)),
            fixed((1, d)),
            fixed(w_a.shape),
            fixed((1, MLA_Q_RANK)),
            fixed((1, MLA_KV_RANK)),
            fixed(wq.shape),
            fixed(wk.shape),
            fixed(wvt.shape),
            tab,
            tab,
        ],
        out_specs=[
            pl.BlockSpec((2 * MLA_HEADS, tm, LANES), lambda i: (0, i, 0)),
            pl.BlockSpec((MLA_HEADS // 2, LANES, tm), lambda i: (0, 0, i)),
        ],
        out_shape=[
            jax.ShapeDtypeStruct((2 * MLA_HEADS, t, LANES), BF16),
            jax.ShapeDtypeStruct((MLA_HEADS // 2, LANES, t), BF16),
        ],
        compiler_params=_params("parallel"),
        name="mla_proj",
    )(h, gain.reshape(1, d), w_a, q_norm.reshape(1, -1), kv_norm.reshape(1, -1), wq, wk, wvt, cos_t, sin_t)


def _fox_proj_kernel(x_ref, g_ref, wqk_ref, wvt_ref, wf_ref, bf_ref, tri_ref, scat_ref,
                     qk_ref, vt_ref, kd_ref, carry_ref, *, tm):
    @pl.when(pl.program_id(1) == 0)
    def _():
        carry_ref[...] = jnp.zeros_like(carry_ref)

    xn = _rms(x_ref[...], g_ref[...]).astype(BF16)
    qk = _dot(xn, wqk_ref[...])
    for c in range(qk_ref.shape[0]):
        qk_ref[c] = qk[:, c * LANES:(c + 1) * LANES].astype(BF16)
    vt = _dot_nt(wvt_ref[...], xn)
    for c in range(vt_ref.shape[0]):
        vt_ref[c] = vt[c * LANES:(c + 1) * LANES, :].astype(BF16)

    z = _dot(xn, wf_ref[...]) + bf_ref[...]
    logf = jnp.minimum(z, 0.0) - jnp.log(1.0 + jnp.exp(-jnp.abs(z)))
    tri = tri_ref[...]
    cum = carry_ref[...] + sum(_dot(tri, part) for part in _split3(logf))
    carry_ref[...] = cum[tm - 1:tm, :]
    kd = sum(_dot(part, scat_ref[p]) for p, part in enumerate(_split3(cum * (-LOG2E))))
    for c in range(kd_ref.shape[0]):
        kd_ref[c] = kd[:, c * LANES:(c + 1) * LANES].astype(BF16)


def _fox_proj(h, gain, wqk, wvt, wf, bf, *, batch, seq, tm=512):
    t, d = h.shape
    ns = seq // tm
    n_pairs = FOX_HEADS // 2
    tri = jnp.tril(jnp.ones((tm, tm), BF16))
    scat = np.zeros((GATE_PARTS, LANES, n_pairs * LANES), np.float32)
    for p in range(GATE_PARTS):
        for hd in range(FOX_HEADS):
            scat[p, hd, (hd // 2) * LANES + GATE_PARTS * (hd % 2) + p] = 1.0
    fixed2 = lambda b, i: (0, 0)
    rows = lambda b, i: (0, b * ns + i, 0)
    return pl.pallas_call(
        functools.partial(_fox_proj_kernel, tm=tm),
        grid=(batch, ns),
        in_specs=[
            pl.BlockSpec((tm, d), lambda b, i: (b * ns + i, 0)),
            pl.BlockSpec((1, d), fixed2),
            pl.BlockSpec(wqk.shape, fixed2),
            pl.BlockSpec(wvt.shape, fixed2),
            pl.BlockSpec(wf.shape, fixed2),
            pl.BlockSpec((1, LANES), fixed2),
            pl.BlockSpec((tm, tm), fixed2),
            pl.BlockSpec(scat.shape, lambda b, i: (0, 0, 0)),
        ],
        out_specs=[
            pl.BlockSpec((2 * n_pairs, tm, LANES), rows),
            pl.BlockSpec((n_pairs, LANES, tm), lambda b, i: (0, 0, b * ns + i)),
            pl.BlockSpec((n_pairs, tm, LANES), rows),
        ],
        out_shape=[
            jax.ShapeDtypeStruct((2 * n_pairs, t, LANES), BF16),
            jax.ShapeDtypeStruct((n_pairs, LANES, t), BF16),
            jax.ShapeDtypeStruct((n_pairs, t, LANES), BF16),
        ],
        scratch_shapes=[pltpu.VMEM((1, LANES), F32)],
        compiler_params=_params("arbitrary", "arbitrary"),
        name="fox_proj",
    )(h, gain.reshape(1, d), wqk, wvt, wf, bf, tri, jnp.asarray(scat, BF16))


def _flash_kernel(*refs, sep, has_bias, tq, unroll):
    if has_bias:
        q_ref, k_ref, vt_ref, kd_ref, o_ref, qs_ref, vta_ref, acc_ref, s_ref, mx_ref, m_ref = refs
    else:
        q_ref, k_ref, vt_ref, o_ref, qs_ref, vta_ref, acc_ref, s_ref, mx_ref, m_ref = refs
        kd_ref = None
    half = LANES // 2
    nq = acc_ref.shape[0]
    seq = nq * tq

    vt = vt_ref[0]
    r = lax.broadcasted_iota(jnp.int32, vt.shape, 0)
    one = jnp.ones_like(vt)
    vta_ref[0] = jnp.where(r < half, vt, one)
    vta_ref[1] = jnp.where(r < half, one, vt)
    row = lax.broadcasted_iota(jnp.int32, (LANES, tq), 0)
    for hh in range(2):
        for t in range(nq):
            cols = slice(t * tq, (t + 1) * tq)
            qt = q_ref[hh if sep else 0, cols, :].astype(F32).T
            if not sep:
                qt = jnp.where((row < half) if hh == 0 else (row >= half), qt, 0.0)
            qs_ref[hh, 0:LANES, cols] = qt.astype(BF16)
        if has_bias:
            r2 = lax.broadcasted_iota(jnp.int32, (LANES, seq), 0)
            sel = jnp.logical_and(r2 >= GATE_PARTS * hh, r2 < GATE_PARTS * (hh + 1))
            qs_ref[hh, LANES:2 * LANES, :] = jnp.where(sel, 1.0, 0.0).astype(BF16)

    def queries(hh, start, n):
        return qs_ref[hh, :, pl.ds(start, n)]

    def keys(hh, start, n):
        k_blk = k_ref[hh if sep else 0, pl.ds(start, n), :]
        if has_bias:
            k_blk = jnp.concatenate([k_blk, kd_ref[0, pl.ds(start, n), :]], axis=1)
        return k_blk

    def scores(i, kb, slot):
        qstart, kstart = pl.multiple_of(i * tq, tq), pl.multiple_of(kb * tq, tq)
        for hh in range(2):
            st = _dot(keys(hh, kstart, tq), queries(hh, qstart, tq))
            s_ref[slot, hh] = st
            mx_ref[slot, hh] = _colmax(st)

    def consume(i, kb, slot):
        kstart = pl.multiple_of(kb * tq, tq)
        for hh in range(2):
            m_old = m_ref[i, hh]
            m_new = jnp.maximum(m_old, mx_ref[slot, hh])
            m_ref[i, hh] = m_new
            pt = jnp.exp2(s_ref[slot, hh] - m_new).astype(BF16)
            pv = _dot(vta_ref[hh, :, pl.ds(kstart, tq)], pt)
            acc_ref[i, hh] = jnp.exp2(m_old - m_new) * acc_ref[i, hh] + pv

    hq = tq // 2
    halves = ((0, hq), (hq, tq))

    def scores_diag(i, slot):
        for hh in range(2):
            for q0, nk in halves:
                st = _dot(keys(hh, i * tq, nk), queries(hh, i * tq + q0, hq))
                key = lax.broadcasted_iota(jnp.int32, (nk, hq), 0)
                qry = lax.broadcasted_iota(jnp.int32, (nk, hq), 1) + q0
                st = jnp.where(key <= qry, st, NEG_INF)
                s_ref[slot, hh, 0:nk, q0:q0 + hq] = st
                mx_ref[slot, hh, :, q0:q0 + hq] = _colmax(st)

    def consume_diag(i, slot):
        for hh in range(2):
            for q0, nk in halves:
                m_new = mx_ref[slot, hh, :, q0:q0 + hq]
                m_ref[i, hh, :, q0:q0 + hq] = m_new
                pt = jnp.exp2(s_ref[slot, hh, 0:nk, q0:q0 + hq] - m_new).astype(BF16)
                acc_ref[i, hh, :, q0:q0 + hq] = _dot(vta_ref[hh, :, i * tq:i * tq + nk], pt)

    scores_diag(0, 0)
    for i in range(nq):
        if i + 1 < nq:
            scores_diag(i + 1, (i + 1) % 2)
        elif nq > 1:
            scores(1, 0, (i + 1) % 2)
        consume_diag(i, i % 2)

    def advance(i, kb):
        last = kb + 1 == i
        return jnp.where(last, i + 1, i), jnp.where(last, 0, kb + 1)

    def trip(_, carry):
        i, kb = carry
        for u in range(unroll):
            ni, nkb = advance(i, kb)
            scores(jnp.minimum(ni, nq - 1), nkb, (nq + u + 1) % 2)
            consume(i, kb, (nq + u) % 2)
            i, kb = ni, nkb
        return i, kb

    n_items = nq * (nq - 1) // 2
    assert unroll % 2 == 0 and n_items % unroll == 0
    lax.fori_loop(0, n_items // unroll, trip, (jnp.int32(1), jnp.int32(0)))

    r = lax.broadcasted_iota(jnp.int32, (LANES, tq), 0)
    for i in range(nq):
        acc_a, acc_b = acc_ref[i, 0], acc_ref[i, 1]
        o_t = jnp.where(r < half, acc_a / acc_a[half:half + 1, :], acc_b / acc_b[0:1, :])
        o_ref[0, i * tq:(i + 1) * tq, :] = o_t.T.astype(o_ref.dtype)


def _flash(qk, vt, kd, *, batch, seq, n_pairs, q_blk, k_blk, sep, tq=512):
    t = qk.shape[1]
    nq = seq // tq
    w = 2 if sep else 1
    kc = 2 * LANES if kd is not None else LANES
    n_items = nq * (nq - 1) // 2
    unroll = max(u for u in (14, 8, 6, 4, 2) if n_items % u == 0)
    in_specs = [
        pl.BlockSpec((w, seq, LANES), lambda b, hp: (q_blk // w + hp, b, 0)),
        pl.BlockSpec((w, seq, LANES), lambda b, hp: (k_blk // w + hp, b, 0)),
        pl.BlockSpec((1, LANES, seq), lambda b, hp: (hp, 0, b)),
    ]
    args = [qk, qk, vt]
    if kd is not None:
        in_specs.append(pl.BlockSpec((1, seq, LANES), lambda b, hp: (hp, b, 0)))
        args.append(kd)
    return pl.pallas_call(
        functools.partial(_flash_kernel, sep=sep, has_bias=kd is not None, tq=tq, unroll=unroll),
        grid=(batch, n_pairs),
        in_specs=in_specs,
        out_specs=pl.BlockSpec((1, seq, LANES), lambda b, hp: (hp, b, 0)),
        out_shape=jax.ShapeDtypeStruct((n_pairs, t, LANES), BF16),
        scratch_shapes=[
            pltpu.VMEM((2, kc, seq), BF16),
            pltpu.VMEM((2, LANES, seq), BF16),
            pltpu.VMEM((nq, 2, LANES, tq), F32),
            pltpu.VMEM((2, 2, tq, tq), F32),
            pltpu.VMEM((2, 2, 1, tq), F32),
            pltpu.VMEM((nq, 2, 1, tq), F32),
        ],
        compiler_params=_params("parallel", "parallel"),
        name="flash_sep" if sep else "flash_shared",
    )(*args)


def _dil_attn_kernel(*refs):
    n_in = 5 * len(DIL_PATTERNS)
    o_ref, og_ref, lg_ref, band_ref = refs[n_in:]
    i = pl.program_id(2)
    w = DIL_WINDOW_KEYS
    v_first = lax.broadcasted_iota(jnp.int32, (w, LANES), 1) < 64
    qi = lax.broadcasted_iota(jnp.int32, (2 * w, 2 * w), 0) % w
    kk = lax.broadcasted_iota(jnp.int32, (2 * w, 2 * w), 1)
    band = jnp.logical_and(kk >= qi, kk <= qi + w)
    band_ref[0] = jnp.where(band, 0.0, NEG_INF)
    band_ref[1] = jnp.where(jnp.logical_and(band, kk >= jnp.where(i > 0, 0, w)), 0.0, NEG_INF)
    ones = jnp.ones((2 * w, LANES), BF16)
    for g, (_, d) in enumerate(DIL_PATTERNS):
        q_ref, k_ref, kp_ref, v_ref, vp_ref = refs[5 * g:5 * g + 5]
        for r in range(d):
            ls = slice(r * LANES, (r + 1) * LANES)
            for sub in range(DIL_TOKENS // d // w):
                rows = slice(sub * w, (sub + 1) * w)
                q_u = jnp.concatenate([q_ref[0, 0, rows, ls], q_ref[1, 0, rows, ls]], axis=0)
                if sub == 0:
                    k_u = jnp.concatenate([kp_ref[0, 0, :, ls], k_ref[0, 0, 0:w, ls]], axis=0)
                    v_u = jnp.concatenate([vp_ref[0, 0, :, ls], v_ref[0, 0, 0:w, ls]], axis=0)
                else:
                    k_u = k_ref[0, 0, (sub - 1) * w:(sub + 1) * w, ls]
                    v_u = v_ref[0, 0, (sub - 1) * w:(sub + 1) * w, ls]
                s = _dot_nt(q_u, k_u) + band_ref[1 if sub == 0 else 0]
                m = jnp.max(s, axis=-1, keepdims=True)
                p = jnp.exp2(s - m).astype(BF16)
                ov = _dot(p, jnp.concatenate([v_u, ones], axis=1))
                num = jnp.where(v_first, ov[0:w, 0:LANES], ov[w:2 * w, 0:LANES])
                den = jnp.where(v_first, ov[0:w, LANES:2 * LANES], ov[w:2 * w, LANES:2 * LANES])
                top = jnp.where(v_first, jnp.broadcast_to(m[0:w], (w, LANES)),
                                jnp.broadcast_to(m[w:2 * w], (w, LANES)))
                tok = pl.ds(sub * w * d + r, w, stride=d) if d > 1 else pl.ds(sub * w, w)
                og_ref[g, tok, :] = num / den
                lg_ref[g, tok, :] = top + jnp.log(den) * LOG2E
    lse = [lg_ref[g] for g in range(len(DIL_PATTERNS))]
    top = functools.reduce(jnp.maximum, lse)
    e = [jnp.exp2(x - top) for x in lse]
    num = sum(e[g] * og_ref[g] for g in range(len(DIL_PATTERNS)))
    o_ref[0] = (num / sum(e)).astype(o_ref.dtype)


def _dil_attn(groups, *, batch, seq):
    n_pairs = DIL_HEADS // 2
    n_tok_blk = seq // DIL_TOKENS
    in_specs, args = [], []
    for arr, (_, d) in zip(groups, DIL_PATTERNS):
        rows = DIL_TOKENS // d
        sub_per_blk = rows // DIL_WINDOW_KEYS

        def cur(base, rows=rows, d=d):
            return pl.BlockSpec((1, 1, rows, d * LANES), lambda b, hp, i: (base + hp, b, i, 0))

        def prev(base, spb=sub_per_blk, d=d):
            return pl.BlockSpec((1, 1, DIL_WINDOW_KEYS, d * LANES),
                                lambda b, hp, i: (base + hp, b, jnp.maximum(i * spb - 1, 0), 0))

        q_spec = pl.BlockSpec((2, 1, rows, d * LANES), lambda b, hp, i: (hp, b, i, 0))
        in_specs += [q_spec, cur(2 * n_pairs), prev(2 * n_pairs), cur(3 * n_pairs), prev(3 * n_pairs)]
        args += [arr] * 5
    n_g = len(DIL_PATTERNS)
    w2 = 2 * DIL_WINDOW_KEYS
    return pl.pallas_call(
        _dil_attn_kernel,
        grid=(batch, n_pairs, n_tok_blk),
        in_specs=in_specs,
        out_specs=pl.BlockSpec((1, DIL_TOKENS, LANES), lambda b, hp, i: (hp, b * n_tok_blk + i, 0)),
        out_shape=jax.ShapeDtypeStruct((n_pairs, batch * seq, LANES), BF16),
        scratch_shapes=[pltpu.VMEM((n_g, DIL_TOKENS, LANES), F32), pltpu.VMEM((n_g, DIL_TOKENS, LANES), F32),
                        pltpu.VMEM((2, w2, w2), F32)],
        compiler_params=_params("parallel", "parallel", "arbitrary"),
        name="dil_attn",
    )(*args)


def _mlp_kernel(h_ref, o_ref, wo_ref, g_ref, wu_ref, wd_ref, fg_ref, out_ref, xn_ref, *, final_norm, tf):
    a = jnp.concatenate([o_ref[k] for k in range(o_ref.shape[0])], axis=-1)
    h1 = h_ref[...] + _dot(a, wo_ref[...])
    xn_ref[...] = _rms(h1, g_ref[...]).astype(BF16)
    y = h1
    for c in range(wu_ref.shape[1] // tf):
        u = jnp.maximum(_dot(xn_ref[...], wu_ref[:, c * tf:(c + 1) * tf]), 0.0)
        y = y + _dot((u * u).astype(BF16), wd_ref[c * tf:(c + 1) * tf, :])
    out_ref[...] = _rms(y, fg_ref[...]) if final_norm else y


def _mlp(h, o, wo, gain, w_up, w_down, final_gain, *, final_norm, tm=512, tf=1024):
    t, d = h.shape
    fixed = lambda shape: pl.BlockSpec(shape, lambda i: (0, 0), pipeline_mode=pl.Buffered(1))
    return pl.pallas_call(
        functools.partial(_mlp_kernel, final_norm=final_norm, tf=tf),
        grid=(t // tm,),
        in_specs=[
            pl.BlockSpec((tm, d), lambda i: (i, 0)),
            pl.BlockSpec((o.shape[0], tm, LANES), lambda i: (0, i, 0)),
            fixed(wo.shape),
            fixed((1, d)),
            fixed(w_up.shape),
            fixed(w_down.shape),
            fixed((1, d)),
        ],
        out_specs=pl.BlockSpec((tm, d), lambda i: (i, 0)),
        out_shape=jax.ShapeDtypeStruct((t, d), F32),
        scratch_shapes=[pltpu.VMEM((tm, d), BF16)],
        compiler_params=_params("parallel"),
        name="mlp_final" if final_norm else "mlp",
    )(h, o, wo, gain.reshape(1, d), w_up, w_down, final_gain.reshape(1, d))


def _rope_halves(seq, dim):
    inv = 1.0 / (ROPE_THETA ** (jnp.arange(0, dim, 2, dtype=F32) / dim))
    ang = jnp.arange(seq, dtype=F32)[:, None] * inv[None, :]
    return jnp.cos(ang), jnp.sin(ang)


def _mla_tables(seq):
    cos, sin = _rope_halves(seq, MLA_ROPE)
    one = lambda n: jnp.ones((seq, n), F32)
    zero = lambda n: jnp.zeros((seq, n), F32)
    cos_t = jnp.concatenate([one(32), cos, one(16), one(32), cos, one(16)], axis=1)
    sin_t = jnp.concatenate([zero(32), -sin, zero(16), zero(32), sin, zero(16)], axis=1)
    return cos_t, sin_t


def _dil_tables(seq, d, tm):
    cos, sin = _rope_halves(seq, DIL_DIM)
    order = lambda x: x.reshape(seq // tm, tm // d, d, LANES).transpose(0, 2, 1, 3).reshape(seq, LANES)
    return (order(jnp.concatenate([cos] * 4, axis=1)),
            order(jnp.concatenate([-sin, -sin, sin, sin], axis=1)))


def _mla_weights(wq_a, wq_b, wkv_a, wkv_b):
    d = wq_a.shape[0]
    z = lambda r, n: jnp.zeros((r, n), F32)
    kpe = wkv_a[:, MLA_KV_RANK:]
    w_a = jnp.concatenate(
        [wq_a, wkv_a[:, :MLA_KV_RANK], z(d, 32), kpe[:, :16], z(d, 48), kpe[:, 16:], z(d, 16)], axis=1)
    scale = (MLA_NOPE + MLA_ROPE) ** -0.5 * LOG2E
    q3 = wq_b.reshape(MLA_Q_RANK, MLA_HEADS, MLA_NOPE + MLA_ROPE) * scale
    zq = jnp.zeros((MLA_Q_RANK, MLA_HEADS, 16), F32)
    wq = jnp.concatenate(
        [q3[..., :32], q3[..., 64:80], zq, q3[..., 32:64], q3[..., 80:96], zq], axis=-1)
    kv3 = wkv_b.reshape(MLA_KV_RANK, MLA_HEADS, MLA_NOPE + MLA_V)
    zk = jnp.zeros((MLA_KV_RANK, MLA_HEADS, 32), F32)
    wk = jnp.concatenate([kv3[..., :32], zk, kv3[..., 32:64], zk], axis=-1)
    wvt = kv3[..., MLA_NOPE:].reshape(MLA_KV_RANK, -1).T
    return (w_a.astype(BF16), wq.reshape(MLA_Q_RANK, -1).astype(BF16),
            wk.reshape(MLA_KV_RANK, -1).astype(BF16), wvt.astype(BF16))


def _dil_weights(w_qkv):
    d = w_qkv.shape[0]
    n_g = len(DIL_PATTERNS)
    n = DIL_HEADS * DIL_DIM
    w = w_qkv.reshape(d, 3, n_g, n)
    qk = w[:, :2].reshape(d, 2, n_g, n // LANES, 2, 2, 32)
    qk = qk * jnp.array([DIL_DIM ** -0.5 * LOG2E, 1.0], F32).reshape(1, 2, 1, 1, 1, 1, 1)
    qk = qk.transpose(0, 1, 2, 3, 5, 4, 6).reshape(d, 2, n_g, n)
    w = jnp.concatenate([qk, w[:, 2:]], axis=1).astype(BF16)
    return [w[:, :, g].reshape(d, 3 * n) for g in range(n_g)]


def _mla_attn(h, attn_norm, wq_a, q_norm, wq_b, wkv_a, kv_norm, wkv_b, tables, *, batch, seq):
    w_a, wq, wk, wvt = _mla_weights(wq_a, wq_b, wkv_a, wkv_b)
    qk, vt = _mla_proj(h, attn_norm, w_a, q_norm, kv_norm, wq, wk, wvt, *tables, seq=seq)
    return _flash(qk, vt, None, batch=batch, seq=seq, n_pairs=MLA_HEADS // 2,
                  q_blk=0, k_blk=MLA_HEADS, sep=True)


def _fox_attn(h, attn_norm, w_qkv, w_f, b_f, *, batch, seq):
    n = FOX_HEADS * FOX_DIM
    d = w_qkv.shape[0]
    wqk = jnp.concatenate([w_qkv[:, :n] * (FOX_DIM ** -0.5 * LOG2E), w_qkv[:, n:2 * n]], axis=1)
    wvt = w_qkv[:, 2 * n:].T
    wf = jnp.concatenate([w_f, jnp.zeros((d, LANES - FOX_HEADS), F32)], axis=1)
    bf = jnp.concatenate([b_f.astype(F32), jnp.zeros((LANES - FOX_HEADS,), F32)]).reshape(1, LANES)
    qk, vt, kd = _fox_proj(h, attn_norm, wqk.astype(BF16), wvt.astype(BF16), wf.astype(BF16), bf,
                           batch=batch, seq=seq)
    n_pairs = FOX_HEADS // 2
    return _flash(qk, vt, kd, batch=batch, seq=seq, n_pairs=n_pairs, q_blk=0, k_blk=n_pairs, sep=False)


def _dil_attn_layer(h, attn_norm, w_qkv, *, batch, seq, tm=512):
    groups = []
    for w_g, (_, d) in zip(_dil_weights(w_qkv), DIL_PATTERNS):
        groups.append(_dil_proj(h, attn_norm, w_g, *_dil_tables(seq, d, tm),
                                batch=batch, seq=seq, d=d, tm=tm))
    return _dil_attn(groups, batch=batch, seq=seq)


def kernel(x, l0_attn_norm, l0_mla_wq_a, l0_mla_q_norm, l0_mla_wq_b, l0_mla_wkv_a, l0_mla_kv_norm, l0_mla_wkv_b, l0_mla_wo, l0_mlp_norm, l0_w_up, l0_w_down, l1_attn_norm, l1_fox_w_qkv, l1_fox_w_f, l1_fox_b_f, l1_fox_wo, l1_mlp_norm, l1_w_up, l1_w_down, l2_attn_norm, l2_dil_w_qkv, l2_dil_wo, l2_mlp_norm, l2_w_up, l2_w_down, l3_attn_norm, l3_mla_wq_a, l3_mla_q_norm, l3_mla_wq_b, l3_mla_wkv_a, l3_mla_kv_norm, l3_mla_wkv_b, l3_mla_wo, l3_mlp_norm, l3_w_up, l3_w_down, final_norm):
    batch, seq, d = x.shape
    assert seq % DIL_TOKENS == 0 and (batch * seq) % 1024 == 0
    for window, dil in DIL_PATTERNS:
        assert window // dil == DIL_WINDOW_KEYS
    mla_t = _mla_tables(seq)
    kw = dict(batch=batch, seq=seq)
    bf = lambda w: w.astype(BF16)
    h = x.reshape(batch * seq, d)

    o = _mla_attn(h, l0_attn_norm, l0_mla_wq_a, l0_mla_q_norm, l0_mla_wq_b, l0_mla_wkv_a,
                  l0_mla_kv_norm, l0_mla_wkv_b, mla_t, **kw)
    h = _mlp(h, o, bf(l0_mla_wo), l0_mlp_norm, bf(l0_w_up), bf(l0_w_down), final_norm, final_norm=False)

    o = _fox_attn(h, l1_attn_norm, l1_fox_w_qkv, l1_fox_w_f, l1_fox_b_f, **kw)
    h = _mlp(h, o, bf(l1_fox_wo), l1_mlp_norm, bf(l1_w_up), bf(l1_w_down), final_norm, final_norm=False)

    o = _dil_attn_layer(h, l2_attn_norm, l2_dil_w_qkv, **kw)
    h = _mlp(h, o, bf(l2_dil_wo), l2_mlp_norm, bf(l2_w_up), bf(l2_w_down), final_norm, final_norm=False)

    o = _mla_attn(h, l3_attn_norm, l3_mla_wq_a, l3_mla_q_norm, l3_mla_wq_b, l3_mla_wkv_a,
                  l3_mla_kv_norm, l3_mla_wkv_b, mla_t, **kw)
    h = _mlp(h, o, bf(l3_mla_wo), l3_mlp_norm, bf(l3_w_up), bf(l3_w_down), final_norm, final_norm=True)
    return h.reshape(batch, seq, d)
```

```python
import functools
import math

import numpy as np
import jax
import jax.numpy as jnp
from jax import lax
from jax.experimental import pallas as pl
from jax.experimental.pallas import tpu as pltpu

F32 = jnp.float32
BF16 = jnp.bfloat16

LANES = 128
NORM_EPS = 1e-6
NEG_INF = -1e30
ROPE_THETA = 10000.0
LOG2E = math.log2(math.e)

MLA_HEADS = 16
MLA_Q_RANK = 384
MLA_KV_RANK = 256
MLA_NOPE = 64
MLA_ROPE = 32
MLA_V = 64
FOX_HEADS = 16
FOX_DIM = 64
DIL_PATTERNS = ((128, 1), (512, 4), (2048, 16))
DIL_HEADS = 16
DIL_DIM = 64
DIL_WINDOW_KEYS = 128
DIL_TOKENS = 2048
GATE_PARTS = 3

VMEM_LIMIT = 56 * 2**20


def _params(*sem):
    return pltpu.CompilerParams(dimension_semantics=sem, vmem_limit_bytes=VMEM_LIMIT)


def _rms(x, g):
    ms = jnp.mean(x * x, axis=-1, keepdims=True)
    return x * lax.rsqrt(ms + NORM_EPS) * g


def _dot(a, b):
    return jnp.dot(a, b, preferred_element_type=F32)


def _dot_nt(a, b):
    return lax.dot_general(a, b, (((1,), (1,)), ((), ())), preferred_element_type=F32)


def _rope(blk, cos, sin):
    return blk * cos + pltpu.roll(blk, 64, axis=1) * sin


def _colmax(st):
    n, c = st.shape
    return jnp.max(jnp.max(st.reshape(8, n // 8, c), axis=0), axis=0, keepdims=True)


def _split3(x):
    hi = x.astype(BF16)
    r1 = x - hi.astype(F32)
    mid = r1.astype(BF16)
    lo = (r1 - mid.astype(F32)).astype(BF16)
    return hi, mid, lo


def _dil_proj_kernel(x_ref, g_ref, wq_ref, wk_ref, wv_ref, cos_ref, sin_ref, o_ref, xs_ref, xn_ref, *, d, tm):
    rpr = tm // d
    nc = x_ref.shape[1] // LANES
    xn = _rms(x_ref[...], g_ref[...])
    if d == 1:
        xn_ref[...] = xn.astype(BF16)
    else:
        for c in range(nc):
            xs_ref[c] = xn[:, c * LANES:(c + 1) * LANES]
        for c in range(nc):
            for r in range(d):
                xn_ref[r * rpr:(r + 1) * rpr, c * LANES:(c + 1) * LANES] = (
                    xs_ref[c, pl.ds(r, rpr, stride=d), :].astype(BF16))

    nb = o_ref.shape[0] // 4
    lane = lax.broadcasted_iota(jnp.int32, (1, LANES), 1)
    first = jnp.where((lane % 64) < 32, 1.0, 0.0)
    for j in range(3):
        acc = _dot(xn_ref[...], (wq_ref, wk_ref, wv_ref)[j][...])
        for c in range(nb):
            for r in range(d):
                rs = slice(r * rpr, (r + 1) * rpr)
                ls = slice(r * LANES, (r + 1) * LANES)
                blk = acc[rs, c * LANES:(c + 1) * LANES]
                if j < 2:
                    blk = _rope(blk, cos_ref[rs, :], sin_ref[rs, :])
                if j == 0:
                    o_ref[2 * c, 0, :, ls] = (blk * first).astype(o_ref.dtype)
                    o_ref[2 * c + 1, 0, :, ls] = (blk * (1.0 - first)).astype(o_ref.dtype)
                else:
                    o_ref[(j + 1) * nb + c, 0, :, ls] = blk.astype(o_ref.dtype)


def _dil_proj(h, gain, wqk, wv, cos_t, sin_t, *, batch, seq, group, d, tm):
    t, dm = h.shape
    ns = seq // tm
    n_g = len(DIL_PATTERNS)
    n = wv.shape[1] // n_g
    nb = 4 * n // LANES
    return pl.pallas_call(
        functools.partial(_dil_proj_kernel, d=d, tm=tm),
        grid=(batch, ns),
        in_specs=[
            pl.BlockSpec((tm, dm), lambda b, i: (b * ns + i, 0)),
            pl.BlockSpec((1, dm), lambda b, i: (0, 0)),
            pl.BlockSpec((dm, n), lambda b, i: (0, group)),
            pl.BlockSpec((dm, n), lambda b, i: (0, n_g + group)),
            pl.BlockSpec((dm, n), lambda b, i: (0, group)),
            pl.BlockSpec((tm, LANES), lambda b, i: (i, 0)),
            pl.BlockSpec((tm, LANES), lambda b, i: (i, 0)),
        ],
        out_specs=pl.BlockSpec((nb, 1, tm // d, d * LANES), lambda b, i: (0, b, i, 0)),
        out_shape=jax.ShapeDtypeStruct((nb, batch, seq // d, d * LANES), BF16),
        scratch_shapes=[pltpu.VMEM((dm // LANES, tm, LANES), F32), pltpu.VMEM((tm, dm), BF16)],
        compiler_params=_params("parallel", "parallel"),
        name="dil_proj_d%d" % d,
    )(h, gain.reshape(1, dm), wqk, wqk, wv, cos_t, sin_t)


def _mla_proj_kernel(x_ref, g_ref, wa_ref, qn_ref, kvn_ref, wq_ref, wk_ref, wvt_ref, cos_ref, sin_ref,
                     qk_ref, vt_ref):
    cos, sin = cos_ref[...], sin_ref[...]
    xn = _rms(x_ref[...], g_ref[...]).astype(BF16)
    y = _dot(xn, wa_ref[...])
    cq = _rms(y[:, :MLA_Q_RANK], qn_ref[...]).astype(BF16)
    ckv = _rms(y[:, MLA_Q_RANK:MLA_Q_RANK + MLA_KV_RANK], kvn_ref[...]).astype(BF16)
    kpe = _rope(y[:, MLA_Q_RANK + MLA_KV_RANK:], cos, sin)
    q = _dot(cq, wq_ref[...])
    for c in range(MLA_HEADS):
        qk_ref[c] = _rope(q[:, c * LANES:(c + 1) * LANES], cos, sin).astype(BF16)
    k = _dot(ckv, wk_ref[...])
    for c in range(MLA_HEADS):
        qk_ref[MLA_HEADS + c] = (k[:, c * LANES:(c + 1) * LANES] + kpe).astype(BF16)
    vt = _dot_nt(wvt_ref[...], ckv)
    for c in range(MLA_HEADS // 2):
        vt_ref[c] = vt[c * LANES:(c + 1) * LANES, :].astype(BF16)


def _mla_proj(h, gain, w_a, q_norm, kv_norm, wq, wk, wvt, cos_t, sin_t, *, seq, tm=512):
    t, d = h.shape
    ns = seq // tm
    fixed = lambda shape: pl.BlockSpec(shape, lambda i: (0, 0))
    tab = pl.BlockSpec((tm, LANES), lambda i: (i % ns, 0))
    return pl.pallas_call(
        _mla_proj_kernel,
        grid=(t // tm,),
        in_specs=[
            pl.BlockSpec((tm, d), lambda i: (i, 0)),
            fixed((1, d)),
            fixed(w_a.shape),
            fixed((1, MLA_Q_RANK)),
            fixed((1, MLA_KV_RANK)),
            fixed(wq.shape),
            fixed(wk.shape),
            fixed(wvt.shape),
            tab,
            tab,
        ],
        out_specs=[
            pl.BlockSpec((2 * MLA_HEADS, tm, LANES), lambda i: (0, i, 0)),
            pl.BlockSpec((MLA_HEADS // 2, LANES, tm), lambda i: (0, 0, i)),
        ],
        out_shape=[
            jax.ShapeDtypeStruct((2 * MLA_HEADS, t, LANES), BF16),
            jax.ShapeDtypeStruct((MLA_HEADS // 2, LANES, t), BF16),
        ],
        compiler_params=_params("parallel"),
        name="mla_proj",
    )(h, gain.reshape(1, d), w_a, q_norm.reshape(1, -1), kv_norm.reshape(1, -1), wq, wk, wvt, cos_t, sin_t)


def _fox_proj_kernel(x_ref, g_ref, wqk_ref, wvt_ref, wf_ref, bf_ref, tri_ref, scat_ref,
                     qk_ref, vt_ref, kd_ref, carry_ref, *, tm):
    @pl.when(pl.program_id(1) == 0)
    def _():
        carry_ref[...] = jnp.zeros_like(carry_ref)

    xn = _rms(x_ref[...], g_ref[...]).astype(BF16)
    qk = _dot(xn, wqk_ref[...])
    for c in range(qk_ref.shape[0]):
        qk_ref[c] = qk[:, c * LANES:(c + 1) * LANES].astype(BF16)
    vt = _dot_nt(wvt_ref[...], xn)
    for c in range(vt_ref.shape[0]):
        vt_ref[c] = vt[c * LANES:(c + 1) * LANES, :].astype(BF16)

    heads = jnp.where(lax.broadcasted_iota(jnp.int32, (1, LANES), 1) < FOX_HEADS, 1.0, 0.0)

    def pack3(x):
        hi, mid, lo = (p.astype(F32) for p in _split3(x * heads))
        return (hi + pltpu.roll(mid, FOX_HEADS, axis=1) + pltpu.roll(lo, 2 * FOX_HEADS, axis=1)).astype(BF16)

    z = _dot(xn, wf_ref[...]) + bf_ref[...]
    logf = jnp.minimum(z, 0.0) - jnp.log(1.0 + jnp.exp(-jnp.abs(z)))
    c3 = _dot(tri_ref[...], pack3(logf))
    cum = (c3 + pltpu.roll(c3, LANES - FOX_HEADS, axis=1)
           + pltpu.roll(c3, LANES - 2 * FOX_HEADS, axis=1)) * heads + carry_ref[...]
    carry_ref[...] = cum[tm - 1:tm, :]
    kd = _dot(pack3(cum * (-LOG2E)), scat_ref[...])
    for c in range(kd_ref.shape[0]):
        kd_ref[c] = kd[:, c * LANES:(c + 1) * LANES].astype(BF16)


def _fox_proj(h, gain, wqk, wvt, wf, bf, *, batch, seq, tm=512):
    t, d = h.shape
    ns = seq // tm
    n_pairs = FOX_HEADS // 2
    tri = jnp.tril(jnp.ones((tm, tm), BF16))
    scat = np.zeros((LANES, n_pairs * LANES), np.float32)
    for p in range(GATE_PARTS):
        for hd in range(FOX_HEADS):
            scat[p * FOX_HEADS + hd, (hd // 2) * LANES + GATE_PARTS * (hd % 2) + p] = 1.0
    fixed2 = lambda b, i: (0, 0)
    rows = lambda b, i: (0, b * ns + i, 0)
    return pl.pallas_call(
        functools.partial(_fox_proj_kernel, tm=tm),
        grid=(batch, ns),
        in_specs=[
            pl.BlockSpec((tm, d), lambda b, i: (b * ns + i, 0)),
            pl.BlockSpec((1, d), fixed2),
            pl.BlockSpec(wqk.shape, fixed2),
            pl.BlockSpec(wvt.shape, fixed2),
            pl.BlockSpec(wf.shape, fixed2),
            pl.BlockSpec((1, LANES), fixed2),
            pl.BlockSpec((tm, tm), fixed2),
            pl.BlockSpec(scat.shape, fixed2),
        ],
        out_specs=[
            pl.BlockSpec((2 * n_pairs, tm, LANES), rows),
            pl.BlockSpec((n_pairs, LANES, tm), lambda b, i: (0, 0, b * ns + i)),
            pl.BlockSpec((n_pairs, tm, LANES), rows),
        ],
        out_shape=[
            jax.ShapeDtypeStruct((2 * n_pairs, t, LANES), BF16),
            jax.ShapeDtypeStruct((n_pairs, LANES, t), BF16),
            jax.ShapeDtypeStruct((n_pairs, t, LANES), BF16),
        ],
        scratch_shapes=[pltpu.VMEM((1, LANES), F32)],
        compiler_params=_params("arbitrary", "arbitrary"),
        name="fox_proj",
    )(h, gain.reshape(1, d), wqk, wvt, wf, bf, tri, jnp.asarray(scat, BF16))


def _flash_kernel(*refs, sep, has_bias, tq, unroll):
    n_in = 4 if has_bias else 3
    q_ref, k_ref, vt_ref = refs[:3]
    kd_ref = refs[3] if has_bias else None
    o_ref, qs_ref, vta_ref, acc_ref, s_ref, mx_ref, m_ref = refs[n_in:]
    half = LANES // 2
    nq = acc_ref.shape[0]
    seq = nq * tq

    vt = vt_ref[0]
    r = lax.broadcasted_iota(jnp.int32, vt.shape, 0)
    one = jnp.ones_like(vt)
    vta_ref[0] = jnp.where(r < half, vt, one)
    vta_ref[1] = jnp.where(r < half, one, vt)
    row = lax.broadcasted_iota(jnp.int32, (LANES, tq), 0)
    for hh in range(2):
        for t in range(nq):
            cols = slice(t * tq, (t + 1) * tq)
            qt = q_ref[hh if sep else 0, cols, :].astype(F32).T
            if not sep:
                qt = jnp.where((row < half) if hh == 0 else (row >= half), qt, 0.0)
            qs_ref[hh, 0:LANES, cols] = qt.astype(BF16)
        if has_bias:
            r2 = lax.broadcasted_iota(jnp.int32, (LANES, seq), 0)
            sel = jnp.logical_and(r2 >= GATE_PARTS * hh, r2 < GATE_PARTS * (hh + 1))
            qs_ref[hh, LANES:2 * LANES, :] = jnp.where(sel, 1.0, 0.0).astype(BF16)

    def queries(hh, start, n):
        return qs_ref[hh, :, pl.ds(start, n)]

    def keys(hh, start, n):
        k_blk = k_ref[hh if sep else 0, pl.ds(start, n), :]
        if has_bias:
            k_blk = jnp.concatenate([k_blk, kd_ref[0, pl.ds(start, n), :]], axis=1)
        return k_blk

    def scores(i, kb, slot):
        qstart, kstart = pl.multiple_of(i * tq, tq), pl.multiple_of(kb * tq, tq)
        for hh in range(2):
            st = _dot(keys(hh, kstart, tq), queries(hh, qstart, tq))
            s_ref[slot, hh] = st
            mx_ref[slot, hh] = _colmax(st)

    def consume(i, kb, slot):
        kstart = pl.multiple_of(kb * tq, tq)
        for hh in range(2):
            m_old = m_ref[i, hh]
            m_new = jnp.maximum(m_old, mx_ref[slot, hh])
            m_ref[i, hh] = m_new
            pt = jnp.exp2(s_ref[slot, hh] - m_new).astype(BF16)
            pv = _dot(vta_ref[hh, :, pl.ds(kstart, tq)], pt)
            acc_ref[i, hh] = jnp.exp2(m_old - m_new) * acc_ref[i, hh] + pv

    hq = tq // 2
    halves = ((0, hq), (hq, tq))

    def scores_diag(i, slot):
        for hh in range(2):
            for q0, nk in halves:
                st = _dot(keys(hh, i * tq, nk), queries(hh, i * tq + q0, hq))
                key = lax.broadcasted_iota(jnp.int32, (nk, hq), 0)
                qry = lax.broadcasted_iota(jnp.int32, (nk, hq), 1) + q0
                st = jnp.where(key <= qry, st, NEG_INF)
                s_ref[slot, hh, 0:nk, q0:q0 + hq] = st
                mx_ref[slot, hh, :, q0:q0 + hq] = _colmax(st)

    def consume_diag(i, slot):
        for hh in range(2):
            for q0, nk in halves:
                m_new = mx_ref[slot, hh, :, q0:q0 + hq]
                m_ref[i, hh, :, q0:q0 + hq] = m_new
                pt = jnp.exp2(s_ref[slot, hh, 0:nk, q0:q0 + hq] - m_new).astype(BF16)
                acc_ref[i, hh, :, q0:q0 + hq] = _dot(vta_ref[hh, :, i * tq:i * tq + nk], pt)

    scores_diag(0, 0)
    for i in range(nq):
        if i + 1 < nq:
            scores_diag(i + 1, (i + 1) % 2)
        elif nq > 1:
            scores(1, 0, (i + 1) % 2)
        consume_diag(i, i % 2)

    def advance(i, kb):
        last = kb + 1 == i
        return jnp.where(last, i + 1, i), jnp.where(last, 0, kb + 1)

    def trip(_, carry):
        i, kb = carry
        for u in range(unroll):
            ni, nkb = advance(i, kb)
            scores(jnp.minimum(ni, nq - 1), nkb, (nq + u + 1) % 2)
            consume(i, kb, (nq + u) % 2)
            i, kb = ni, nkb
        return i, kb

    n_items = nq * (nq - 1) // 2
    assert unroll % 2 == 0 and n_items % unroll == 0
    lax.fori_loop(0, n_items // unroll, trip, (jnp.int32(1), jnp.int32(0)))

    r = lax.broadcasted_iota(jnp.int32, (LANES, tq), 0)
    for i in range(nq):
        acc_a, acc_b = acc_ref[i, 0], acc_ref[i, 1]
        o_t = jnp.where(r < half, acc_a / acc_a[half:half + 1, :], acc_b / acc_b[0:1, :])
        o_ref[0, i * tq:(i + 1) * tq, :] = o_t.T.astype(o_ref.dtype)


def _flash(qk, vt, kd, *, batch, seq, n_pairs, q_blk, k_blk, sep, tq=512):
    t = qk.shape[1]
    nq = seq // tq
    w = 2 if sep else 1
    kc = 2 * LANES if kd is not None else LANES
    n_items = nq * (nq - 1) // 2
    unroll = max(u for u in (14, 8, 6, 4, 2) if n_items % u == 0)
    in_specs = [
        pl.BlockSpec((w, seq, LANES), lambda b, hp: (q_blk // w + hp, b, 0)),
        pl.BlockSpec((w, seq, LANES), lambda b, hp: (k_blk // w + hp, b, 0)),
        pl.BlockSpec((1, LANES, seq), lambda b, hp: (hp, 0, b)),
    ]
    args = [qk, qk, vt]
    if kd is not None:
        in_specs.append(pl.BlockSpec((1, seq, LANES), lambda b, hp: (hp, b, 0)))
        args.append(kd)
    return pl.pallas_call(
        functools.partial(_flash_kernel, sep=sep, has_bias=kd is not None, tq=tq, unroll=unroll),
        grid=(batch, n_pairs),
        in_specs=in_specs,
        out_specs=pl.BlockSpec((1, seq, LANES), lambda b, hp: (hp, b, 0)),
        out_shape=jax.ShapeDtypeStruct((n_pairs, t, LANES), BF16),
        scratch_shapes=[
            pltpu.VMEM((2, kc, seq), BF16),
            pltpu.VMEM((2, LANES, seq), BF16),
            pltpu.VMEM((nq, 2, LANES, tq), F32),
            pltpu.VMEM((2, 2, tq, tq), F32),
            pltpu.VMEM((2, 2, 1, tq), F32),
            pltpu.VMEM((nq, 2, 1, tq), F32),
        ],
        compiler_params=_params("parallel", "parallel"),
        name="flash_sep" if sep else "flash_shared",
    )(*args)


def _dil_attn_kernel(*refs):
    n_in = 5 * len(DIL_PATTERNS)
    o_ref, og_ref, lg_ref, band_ref = refs[n_in:]
    i = pl.program_id(2)
    w = DIL_WINDOW_KEYS
    v_first = lax.broadcasted_iota(jnp.int32, (w, LANES), 1) < 64
    qi = lax.broadcasted_iota(jnp.int32, (2 * w, 2 * w), 0) % w
    kk = lax.broadcasted_iota(jnp.int32, (2 * w, 2 * w), 1)
    band = jnp.logical_and(kk >= qi, kk <= qi + w)
    band_ref[0] = jnp.where(band, 0.0, NEG_INF)
    band_ref[1] = jnp.where(jnp.logical_and(band, kk >= jnp.where(i > 0, 0, w)), 0.0, NEG_INF)
    ones = jnp.ones((2 * w, LANES), BF16)
    for g, (_, d) in enumerate(DIL_PATTERNS):
        q_ref, k_ref, kp_ref, v_ref, vp_ref = refs[5 * g:5 * g + 5]
        for r in range(d):
            ls = slice(r * LANES, (r + 1) * LANES)
            for sub in range(DIL_TOKENS // d // w):
                rows = slice(sub * w, (sub + 1) * w)
                q_u = jnp.concatenate([q_ref[0, 0, rows, ls], q_ref[1, 0, rows, ls]], axis=0)
                if sub == 0:
                    k_u = jnp.concatenate([kp_ref[0, 0, :, ls], k_ref[0, 0, 0:w, ls]], axis=0)
                    v_u = jnp.concatenate([vp_ref[0, 0, :, ls], v_ref[0, 0, 0:w, ls]], axis=0)
                else:
                    k_u = k_ref[0, 0, (sub - 1) * w:(sub + 1) * w, ls]
                    v_u = v_ref[0, 0, (sub - 1) * w:(sub + 1) * w, ls]
                s = _dot_nt(q_u, k_u) + band_ref[1 if sub == 0 else 0]
                m = jnp.max(s, axis=-1, keepdims=True)
                p = jnp.exp2(s - m).astype(BF16)
                ov = _dot(p, jnp.concatenate([v_u, ones], axis=1))
                num = jnp.where(v_first, ov[0:w, 0:LANES], ov[w:2 * w, 0:LANES])
                den = jnp.where(v_first, ov[0:w, LANES:2 * LANES], ov[w:2 * w, LANES:2 * LANES])
                top = jnp.where(v_first, jnp.broadcast_to(m[0:w], (w, LANES)),
                                jnp.broadcast_to(m[w:2 * w], (w, LANES)))
                tok = pl.ds(sub * w * d + r, w, stride=d) if d > 1 else pl.ds(sub * w, w)
                og_ref[g, tok, :] = num / den
                lg_ref[g, tok, :] = top + jnp.log(den) * LOG2E
    lse = [lg_ref[g] for g in range(len(DIL_PATTERNS))]
    top = functools.reduce(jnp.maximum, lse)
    e = [jnp.exp2(x - top) for x in lse]
    num = sum(e[g] * og_ref[g] for g in range(len(DIL_PATTERNS)))
    o_ref[0] = (num / sum(e)).astype(o_ref.dtype)


def _dil_attn(groups, *, batch, seq):
    n_pairs = DIL_HEADS // 2
    n_tok_blk = seq // DIL_TOKENS
    in_specs, args = [], []
    for arr, (_, d) in zip(groups, DIL_PATTERNS):
        rows = DIL_TOKENS // d
        sub_per_blk = rows // DIL_WINDOW_KEYS

        def cur(base, rows=rows, d=d):
            return pl.BlockSpec((1, 1, rows, d * LANES), lambda b, hp, i: (base + hp, b, i, 0))

        def prev(base, spb=sub_per_blk, d=d):
            return pl.BlockSpec((1, 1, DIL_WINDOW_KEYS, d * LANES),
                                lambda b, hp, i: (base + hp, b, jnp.maximum(i * spb - 1, 0), 0))

        q_spec = pl.BlockSpec((2, 1, rows, d * LANES), lambda b, hp, i: (hp, b, i, 0))
        in_specs += [q_spec, cur(2 * n_pairs), prev(2 * n_pairs), cur(3 * n_pairs), prev(3 * n_pairs)]
        args += [arr] * 5
    n_g = len(DIL_PATTERNS)
    w2 = 2 * DIL_WINDOW_KEYS
    return pl.pallas_call(
        _dil_attn_kernel,
        grid=(batch, n_pairs, n_tok_blk),
        in_specs=in_specs,
        out_specs=pl.BlockSpec((1, DIL_TOKENS, LANES), lambda b, hp, i: (hp, b * n_tok_blk + i, 0)),
        out_shape=jax.ShapeDtypeStruct((n_pairs, batch * seq, LANES), BF16),
        scratch_shapes=[pltpu.VMEM((n_g, DIL_TOKENS, LANES), F32), pltpu.VMEM((n_g, DIL_TOKENS, LANES), F32),
                        pltpu.VMEM((2, w2, w2), F32)],
        compiler_params=_params("parallel", "parallel", "arbitrary"),
        name="dil_attn",
    )(*args)


def _mlp_kernel(h_ref, o_ref, wo_ref, g_ref, wu_ref, wd_ref, fg_ref, out_ref, xn_ref, *, final_norm, tf):
    a = jnp.concatenate([o_ref[k] for k in range(o_ref.shape[0])], axis=-1)
    h1 = h_ref[...] + _dot(a, wo_ref[...])
    xn_ref[...] = _rms(h1, g_ref[...]).astype(BF16)
    y = h1
    for c in range(wu_ref.shape[1] // tf):
        u = jnp.maximum(_dot(xn_ref[...], wu_ref[:, c * tf:(c + 1) * tf]), 0.0)
        y = y + _dot((u * u).astype(BF16), wd_ref[c * tf:(c + 1) * tf, :])
    out_ref[...] = _rms(y, fg_ref[...]) if final_norm else y


def _mlp(h, o, wo, gain, w_up, w_down, final_gain, *, final_norm, tm=512, tf=1024):
    t, d = h.shape
    fixed = lambda shape: pl.BlockSpec(shape, lambda i: (0, 0), pipeline_mode=pl.Buffered(1))
    return pl.pallas_call(
        functools.partial(_mlp_kernel, final_norm=final_norm, tf=tf),
        grid=(t // tm,),
        in_specs=[
            pl.BlockSpec((tm, d), lambda i: (i, 0)),
            pl.BlockSpec((o.shape[0], tm, LANES), lambda i: (0, i, 0)),
            fixed(wo.shape),
            fixed((1, d)),
            fixed(w_up.shape),
            fixed(w_down.shape),
            fixed((1, d)),
        ],
        out_specs=pl.BlockSpec((tm, d), lambda i: (i, 0)),
        out_shape=jax.ShapeDtypeStruct((t, d), F32),
        scratch_shapes=[pltpu.VMEM((tm, d), BF16)],
        compiler_params=_params("parallel"),
        name="mlp_final" if final_norm else "mlp",
    )(h, o, wo, gain.reshape(1, d), w_up, w_down, final_gain.reshape(1, d))


def _rope_halves(seq, dim):
    inv = 1.0 / (ROPE_THETA ** (jnp.arange(0, dim, 2, dtype=F32) / dim))
    ang = jnp.arange(seq, dtype=F32)[:, None] * inv[None, :]
    return jnp.cos(ang), jnp.sin(ang)


def _mla_tables(seq):
    cos, sin = _rope_halves(seq, MLA_ROPE)
    one = lambda n: jnp.ones((seq, n), F32)
    zero = lambda n: jnp.zeros((seq, n), F32)
    cos_t = jnp.concatenate([one(32), cos, one(16), one(32), cos, one(16)], axis=1)
    sin_t = jnp.concatenate([zero(32), -sin, zero(16), zero(32), sin, zero(16)], axis=1)
    return cos_t, sin_t


def _dil_tables(seq, d, tm):
    cos, sin = _rope_halves(seq, DIL_DIM)
    order = lambda x: x.reshape(seq // tm, tm // d, d, LANES).transpose(0, 2, 1, 3).reshape(seq, LANES)
    return (order(jnp.concatenate([cos] * 4, axis=1)),
            order(jnp.concatenate([-sin, -sin, sin, sin], axis=1)))


def _relayout(w3, moves, n_dst):
    p = np.zeros((w3.shape[-1], n_dst), np.float32)
    for src, dst in moves:
        p[src, dst] = 1.0
    return jnp.einsum('...c,cl->...l', w3, jnp.asarray(p), preferred_element_type=F32).astype(BF16)


def _mla_weights(wq_a, wq_b, wkv_a, wkv_b):
    lanes = lambda c: c if c < 32 else 32 + c if c < 64 else c - 32 if c < 80 else c + 16
    kpe_moves = [(c, c) for c in range(MLA_KV_RANK)] + [
        (MLA_KV_RANK + c, MLA_KV_RANK + lanes(MLA_NOPE + c)) for c in range(MLA_ROPE)]
    w_a = jnp.concatenate([wq_a.astype(BF16), _relayout(wkv_a, kpe_moves, MLA_KV_RANK + LANES)], axis=1)
    scale = (MLA_NOPE + MLA_ROPE) ** -0.5 * LOG2E
    q3 = wq_b.reshape(MLA_Q_RANK, MLA_HEADS, MLA_NOPE + MLA_ROPE) * scale
    wq = _relayout(q3, [(c, lanes(c)) for c in range(MLA_NOPE + MLA_ROPE)], LANES)
    kv3 = wkv_b.reshape(MLA_KV_RANK, MLA_HEADS, MLA_NOPE + MLA_V)
    wk = _relayout(kv3, [(c, lanes(c)) for c in range(MLA_NOPE)], LANES)
    wvt = kv3[..., MLA_NOPE:].reshape(MLA_KV_RANK, -1).T.astype(BF16)
    return w_a, wq.reshape(MLA_Q_RANK, -1), wk.reshape(MLA_KV_RANK, -1), wvt


def _dil_weights(w_qkv):
    d = w_qkv.shape[0]
    n_qk = 2 * len(DIL_PATTERNS) * DIL_HEADS * DIL_DIM
    qk = w_qkv[:, :n_qk].reshape(d, 2, n_qk // (2 * LANES), LANES)
    qk = qk * jnp.array([DIL_DIM ** -0.5 * LOG2E, 1.0], F32).reshape(1, 2, 1, 1)
    old = np.arange(LANES).reshape(2, 2, 32).transpose(1, 0, 2).reshape(LANES)
    qk = _relayout(qk, [(int(old[j]), j) for j in range(LANES)], LANES)
    return qk.reshape(d, n_qk), w_qkv[:, n_qk:].astype(BF16)


def _mla_attn(h, attn_norm, wq_a, q_norm, wq_b, wkv_a, kv_norm, wkv_b, tables, *, batch, seq):
    w_a, wq, wk, wvt = _mla_weights(wq_a, wq_b, wkv_a, wkv_b)
    qk, vt = _mla_proj(h, attn_norm, w_a, q_norm, kv_norm, wq, wk, wvt, *tables, seq=seq)
    return _flash(qk, vt, None, batch=batch, seq=seq, n_pairs=MLA_HEADS // 2,
                  q_blk=0, k_blk=MLA_HEADS, sep=True)


def _fox_attn(h, attn_norm, w_qkv, w_f, b_f, *, batch, seq):
    n = FOX_HEADS * FOX_DIM
    d = w_qkv.shape[0]
    wqk = jnp.concatenate([w_qkv[:, :n] * (FOX_DIM ** -0.5 * LOG2E), w_qkv[:, n:2 * n]], axis=1)
    wvt = w_qkv[:, 2 * n:].T
    wf = jnp.concatenate([w_f, jnp.zeros((d, LANES - FOX_HEADS), F32)], axis=1)
    bf = jnp.concatenate([b_f.astype(F32), jnp.zeros((LANES - FOX_HEADS,), F32)]).reshape(1, LANES)
    qk, vt, kd = _fox_proj(h, attn_norm, wqk.astype(BF16), wvt.astype(BF16), wf.astype(BF16), bf,
                           batch=batch, seq=seq)
    n_pairs = FOX_HEADS // 2
    return _flash(qk, vt, kd, batch=batch, seq=seq, n_pairs=n_pairs, q_blk=0, k_blk=n_pairs, sep=False)


def _dil_attn_layer(h, attn_norm, w_qkv, *, batch, seq, tm=512):
    wqk, wv = _dil_weights(w_qkv)
    groups = [_dil_proj(h, attn_norm, wqk, wv, *_dil_tables(seq, d, tm),
                        batch=batch, seq=seq, group=g, d=d, tm=tm)
              for g, (_, d) in enumerate(DIL_PATTERNS)]
    return _dil_attn(groups, batch=batch, seq=seq)


def kernel(x, l0_attn_norm, l0_mla_wq_a, l0_mla_q_norm, l0_mla_wq_b, l0_mla_wkv_a, l0_mla_kv_norm, l0_mla_wkv_b, l0_mla_wo, l0_mlp_norm, l0_w_up, l0_w_down, l1_attn_norm, l1_fox_w_qkv, l1_fox_w_f, l1_fox_b_f, l1_fox_wo, l1_mlp_norm, l1_w_up, l1_w_down, l2_attn_norm, l2_dil_w_qkv, l2_dil_wo, l2_mlp_norm, l2_w_up, l2_w_down, l3_attn_norm, l3_mla_wq_a, l3_mla_q_norm, l3_mla_wq_b, l3_mla_wkv_a, l3_mla_kv_norm, l3_mla_wkv_b, l3_mla_wo, l3_mlp_norm, l3_w_up, l3_w_down, final_norm):
    batch, seq, d = x.shape
    assert seq % DIL_TOKENS == 0 and (batch * seq) % 1024 == 0
    for window, dil in DIL_PATTERNS:
        assert window // dil == DIL_WINDOW_KEYS
    mla_t = _mla_tables(seq)
    kw = dict(batch=batch, seq=seq)
    bf = lambda w: w.astype(BF16)
    h = x.reshape(batch * seq, d)

    o = _mla_attn(h, l0_attn_norm, l0_mla_wq_a, l0_mla_q_norm, l0_mla_wq_b, l0_mla_wkv_a,
                  l0_mla_kv_norm, l0_mla_wkv_b, mla_t, **kw)
    h = _mlp(h, o, bf(l0_mla_wo), l0_mlp_norm, bf(l0_w_up), bf(l0_w_down), final_norm, final_norm=False)

    o = _fox_attn(h, l1_attn_norm, l1_fox_w_qkv, l1_fox_w_f, l1_fox_b_f, **kw)
    h = _mlp(h, o, bf(l1_fox_wo), l1_mlp_norm, bf(l1_w_up), bf(l1_w_down), final_norm, final_norm=False)

    o = _dil_attn_layer(h, l2_attn_norm, l2_dil_w_qkv, **kw)
    h = _mlp(h, o, bf(l2_dil_wo), l2_mlp_norm, bf(l2_w_up), bf(l2_w_down), final_norm, final_norm=False)

    o = _mla_attn(h, l3_attn_norm, l3_mla_wq_a, l3_mla_q_norm, l3_mla_wq_b, l3_mla_wkv_a,
                  l3_mla_kv_norm, l3_mla_wkv_b, mla_t, **kw)
    h = _mlp(h, o, bf(l3_mla_wo), l3_mlp_norm, bf(l3_w_up), bf(l3_w_down), final_norm, final_norm=True)
    return h.reshape(batch, seq, d)
```

```python
import functools
import math

import numpy as np
import jax
import jax.numpy as jnp
from jax import lax
from jax.experimental import pallas as pl
from jax.experimental.pallas import tpu as pltpu

F32 = jnp.float32
BF16 = jnp.bfloat16

LANES = 128
NORM_EPS = 1e-6
NEG_INF = -1e30
ROPE_THETA = 10000.0
LOG2E = math.log2(math.e)

MLA_HEADS = 16
MLA_Q_RANK = 384
MLA_KV_RANK = 256
MLA_NOPE = 64
MLA_ROPE = 32
MLA_V = 64
FOX_HEADS = 16
FOX_DIM = 64
DIL_PATTERNS = ((128, 1), (512, 4), (2048, 16))
DIL_HEADS = 16
DIL_DIM = 64
DIL_WINDOW_KEYS = 128
DIL_TOKENS = 2048
GATE_PARTS = 3

VMEM_LIMIT = 56 * 2**20


def _params(*sem):
    return pltpu.CompilerParams(dimension_semantics=sem, vmem_limit_bytes=VMEM_LIMIT)


def _rms(x, g):
    ms = jnp.mean(x * x, axis=-1, keepdims=True)
    return x * lax.rsqrt(ms + NORM_EPS) * g


def _dot(a, b):
    return jnp.dot(a, b, preferred_element_type=F32)


def _dot_nt(a, b):
    return lax.dot_general(a, b, (((1,), (1,)), ((), ())), preferred_element_type=F32)


def _rope(blk, cos, sin):
    return blk * cos + pltpu.roll(blk, 64, axis=1) * sin


def _colmax(st):
    n, c = st.shape
    return jnp.max(jnp.max(st.reshape(8, n // 8, c), axis=0), axis=0, keepdims=True)


def _split3(x):
    hi = x.astype(BF16)
    r1 = x - hi.astype(F32)
    mid = r1.astype(BF16)
    lo = (r1 - mid.astype(F32)).astype(BF16)
    return hi, mid, lo


def _dil_proj_kernel(x_ref, g_ref, wq_ref, wk_ref, wv_ref, cos_ref, sin_ref, o_ref, xs_ref, xn_ref, *, d, tm):
    rpr = tm // d
    nc = x_ref.shape[1] // LANES
    xn = _rms(x_ref[...], g_ref[...])
    if d == 1:
        xn_ref[...] = xn.astype(BF16)
    else:
        for c in range(nc):
            xs_ref[c] = xn[:, c * LANES:(c + 1) * LANES]
        for c in range(nc):
            for r in range(d):
                xn_ref[r * rpr:(r + 1) * rpr, c * LANES:(c + 1) * LANES] = (
                    xs_ref[c, pl.ds(r, rpr, stride=d), :].astype(BF16))

    nb = o_ref.shape[0] // 4
    lane = lax.broadcasted_iota(jnp.int32, (1, LANES), 1)
    first = jnp.where((lane % 64) < 32, 1.0, 0.0)
    for j in range(3):
        acc = _dot(xn_ref[...], (wq_ref, wk_ref, wv_ref)[j][...])
        for c in range(nb):
            for r in range(d):
                rs = slice(r * rpr, (r + 1) * rpr)
                ls = slice(r * LANES, (r + 1) * LANES)
                blk = acc[rs, c * LANES:(c + 1) * LANES]
                if j < 2:
                    blk = _rope(blk, cos_ref[rs, :], sin_ref[rs, :])
                if j == 0:
                    o_ref[2 * c, 0, :, ls] = (blk * first).astype(o_ref.dtype)
                    o_ref[2 * c + 1, 0, :, ls] = (blk * (1.0 - first)).astype(o_ref.dtype)
                else:
                    o_ref[(j + 1) * nb + c, 0, :, ls] = blk.astype(o_ref.dtype)


def _dil_proj(h, gain, wqk, wv, cos_t, sin_t, *, batch, seq, group, d, tm):
    t, dm = h.shape
    ns = seq // tm
    n_g = len(DIL_PATTERNS)
    n = wv.shape[1] // n_g
    nb = 4 * n // LANES
    return pl.pallas_call(
        functools.partial(_dil_proj_kernel, d=d, tm=tm),
        grid=(batch, ns),
        in_specs=[
            pl.BlockSpec((tm, dm), lambda b, i: (b * ns + i, 0)),
            pl.BlockSpec((1, dm), lambda b, i: (0, 0)),
            pl.BlockSpec((dm, n), lambda b, i: (0, group)),
            pl.BlockSpec((dm, n), lambda b, i: (0, n_g + group)),
            pl.BlockSpec((dm, n), lambda b, i: (0, group)),
            pl.BlockSpec((tm, LANES), lambda b, i: (i, 0)),
            pl.BlockSpec((tm, LANES), lambda b, i: (i, 0)),
        ],
        out_specs=pl.BlockSpec((nb, 1, tm // d, d * LANES), lambda b, i: (0, b, i, 0)),
        out_shape=jax.ShapeDtypeStruct((nb, batch, seq // d, d * LANES), BF16),
        scratch_shapes=[pltpu.VMEM((dm // LANES, tm, LANES), F32), pltpu.VMEM((tm, dm), BF16)],
        compiler_params=_params("parallel", "parallel"),
        name="dil_proj_d%d" % d,
    )(h, gain.reshape(1, dm), wqk, wqk, wv, cos_t, sin_t)


def _mla_proj_kernel(x_ref, g_ref, wa_ref, qn_ref, kvn_ref, wq_ref, wk_ref, wvt_ref, cos_ref, sin_ref,
                     qk_ref, vt_ref):
    cos, sin = cos_ref[...], sin_ref[...]
    xn = _rms(x_ref[...], g_ref[...]).astype(BF16)
    y = _dot(xn, wa_ref[...])
    cq = _rms(y[:, :MLA_Q_RANK], qn_ref[...]).astype(BF16)
    ckv = _rms(y[:, MLA_Q_RANK:MLA_Q_RANK + MLA_KV_RANK], kvn_ref[...]).astype(BF16)
    kpe = _rope(y[:, MLA_Q_RANK + MLA_KV_RANK:], cos, sin)
    q = _dot(cq, wq_ref[...])
    for c in range(MLA_HEADS):
        qk_ref[c] = _rope(q[:, c * LANES:(c + 1) * LANES], cos, sin).astype(BF16)
    k = _dot(ckv, wk_ref[...])
    for c in range(MLA_HEADS):
        qk_ref[MLA_HEADS + c] = (k[:, c * LANES:(c + 1) * LANES] + kpe).astype(BF16)
    vt = _dot_nt(wvt_ref[...], ckv)
    for c in range(MLA_HEADS // 2):
        vt_ref[c] = vt[c * LANES:(c + 1) * LANES, :].astype(BF16)


def _mla_proj(h, gain, w_a, q_norm, kv_norm, wq, wk, wvt, cos_t, sin_t, *, seq, tm=512):
    t, d = h.shape
    ns = seq // tm
    fixed = lambda shape: pl.BlockSpec(shape, lambda i: (0, 0))
    tab = pl.BlockSpec((tm, LANES), lambda i: (i % ns, 0))
    return pl.pallas_call(
        _mla_proj_kernel,
        grid=(t // tm,),
        in_specs=[
            pl.BlockSpec((tm, d), lambda i: (i, 0)),
            fixed((1, d)),
            fixed(w_a.shape),
            fixed((1, MLA_Q_RANK)),
            fixed((1, MLA_KV_RANK)),
            fixed(wq.shape),
            fixed(wk.shape),
            fixed(wvt.shape),
            tab,
            tab,
        ],
        out_specs=[
            pl.BlockSpec((2 * MLA_HEADS, tm, LANES), lambda i: (0, i, 0)),
            pl.BlockSpec((MLA_HEADS // 2, LANES, tm), lambda i: (0, 0, i)),
        ],
        out_shape=[
            jax.ShapeDtypeStruct((2 * MLA_HEADS, t, LANES), BF16),
            jax.ShapeDtypeStruct((MLA_HEADS // 2, LANES, t), BF16),
        ],
        compiler_params=_params("parallel"),
        name="mla_proj",
    )(h, gain.reshape(1, d), w_a, q_norm.reshape(1, -1), kv_norm.reshape(1, -1), wq, wk, wvt, cos_t, sin_t)


def _fox_proj_kernel(x_ref, g_ref, wqk_ref, wvt_ref, wf_ref, bf_ref, tri_ref, scat_ref,
                     qk_ref, vt_ref, kd_ref, carry_ref, *, tm):
    @pl.when(pl.program_id(1) == 0)
    def _():
        carry_ref[...] = jnp.zeros_like(carry_ref)

    xn = _rms(x_ref[...], g_ref[...]).astype(BF16)
    qk = _dot(xn, wqk_ref[...])
    for c in range(qk_ref.shape[0]):
        qk_ref[c] = qk[:, c * LANES:(c + 1) * LANES].astype(BF16)
    vt = _dot_nt(wvt_ref[...], xn)
    for c in range(vt_ref.shape[0]):
        vt_ref[c] = vt[c * LANES:(c + 1) * LANES, :].astype(BF16)

    heads = jnp.where(lax.broadcasted_iota(jnp.int32, (1, LANES), 1) < FOX_HEADS, 1.0, 0.0)

    def pack3(x):
        hi, mid, lo = (p.astype(F32) for p in _split3(x * heads))
        return (hi + pltpu.roll(mid, FOX_HEADS, axis=1) + pltpu.roll(lo, 2 * FOX_HEADS, axis=1)).astype(BF16)

    z = _dot(xn, wf_ref[...]) + bf_ref[...]
    logf = jnp.minimum(z, 0.0) - jnp.log(1.0 + jnp.exp(-jnp.abs(z)))
    c3 = _dot(tri_ref[...], pack3(logf))
    cum = (c3 + pltpu.roll(c3, LANES - FOX_HEADS, axis=1)
           + pltpu.roll(c3, LANES - 2 * FOX_HEADS, axis=1)) * heads + carry_ref[...]
    carry_ref[...] = cum[tm - 1:tm, :]
    kd = _dot(pack3(cum * (-LOG2E)), scat_ref[...])
    for c in range(kd_ref.shape[0]):
        kd_ref[c] = kd[:, c * LANES:(c + 1) * LANES].astype(BF16)


def _fox_proj(h, gain, wqk, wvt, wf, bf, *, batch, seq, tm=512):
    t, d = h.shape
    ns = seq // tm
    n_pairs = FOX_HEADS // 2
    tri = jnp.asarray(np.tril(np.ones((tm, tm), np.float32)), BF16)
    scat = np.zeros((LANES, n_pairs * LANES), np.float32)
    for p in range(GATE_PARTS):
        for hd in range(FOX_HEADS):
            scat[p * FOX_HEADS + hd, (hd // 2) * LANES + GATE_PARTS * (hd % 2) + p] = 1.0
    fixed2 = lambda b, i: (0, 0)
    rows = lambda b, i: (0, b * ns + i, 0)
    return pl.pallas_call(
        functools.partial(_fox_proj_kernel, tm=tm),
        grid=(batch, ns),
        in_specs=[
            pl.BlockSpec((tm, d), lambda b, i: (b * ns + i, 0)),
            pl.BlockSpec((1, d), fixed2),
            pl.BlockSpec(wqk.shape, fixed2),
            pl.BlockSpec(wvt.shape, fixed2),
            pl.BlockSpec(wf.shape, fixed2),
            pl.BlockSpec((1, LANES), fixed2),
            pl.BlockSpec((tm, tm), fixed2),
            pl.BlockSpec(scat.shape, fixed2),
        ],
        out_specs=[
            pl.BlockSpec((2 * n_pairs, tm, LANES), rows),
            pl.BlockSpec((n_pairs, LANES, tm), lambda b, i: (0, 0, b * ns + i)),
            pl.BlockSpec((n_pairs, tm, LANES), rows),
        ],
        out_shape=[
            jax.ShapeDtypeStruct((2 * n_pairs, t, LANES), BF16),
            jax.ShapeDtypeStruct((n_pairs, LANES, t), BF16),
            jax.ShapeDtypeStruct((n_pairs, t, LANES), BF16),
        ],
        scratch_shapes=[pltpu.VMEM((1, LANES), F32)],
        compiler_params=_params("arbitrary", "arbitrary"),
        name="fox_proj",
    )(h, gain.reshape(1, d), wqk, wvt, wf, bf, tri, jnp.asarray(scat, BF16))


def _flash_kernel(*refs, sep, has_bias, tq, unroll):
    n_in = 4 if has_bias else 3
    q_ref, k_ref, vt_ref = refs[:3]
    kd_ref = refs[3] if has_bias else None
    o_ref, qs_ref, vta_ref, acc_ref, s_ref, mx_ref, m_ref = refs[n_in:]
    half = LANES // 2
    nq = acc_ref.shape[0]
    seq = nq * tq

    vt = vt_ref[0]
    r = lax.broadcasted_iota(jnp.int32, vt.shape, 0)
    one = jnp.ones_like(vt)
    vta_ref[0] = jnp.where(r < half, vt, one)
    vta_ref[1] = jnp.where(r < half, one, vt)
    row = lax.broadcasted_iota(jnp.int32, (LANES, tq), 0)
    for hh in range(2):
        for t in range(nq):
            cols = slice(t * tq, (t + 1) * tq)
            qt = q_ref[hh if sep else 0, cols, :].astype(F32).T
            if not sep:
                qt = jnp.where((row < half) if hh == 0 else (row >= half), qt, 0.0)
            qs_ref[hh, 0:LANES, cols] = qt.astype(BF16)
        if has_bias:
            r2 = lax.broadcasted_iota(jnp.int32, (LANES, seq), 0)
            sel = jnp.logical_and(r2 >= GATE_PARTS * hh, r2 < GATE_PARTS * (hh + 1))
            qs_ref[hh, LANES:2 * LANES, :] = jnp.where(sel, 1.0, 0.0).astype(BF16)

    def queries(hh, start, n):
        return qs_ref[hh, :, pl.ds(start, n)]

    def keys(hh, start, n):
        k_blk = k_ref[hh if sep else 0, pl.ds(start, n), :]
        if has_bias:
            k_blk = jnp.concatenate([k_blk, kd_ref[0, pl.ds(start, n), :]], axis=1)
        return k_blk

    def scores(i, kb, slot):
        qstart, kstart = pl.multiple_of(i * tq, tq), pl.multiple_of(kb * tq, tq)
        for hh in range(2):
            st = _dot(keys(hh, kstart, tq), queries(hh, qstart, tq))
            s_ref[slot, hh] = st
            mx_ref[slot, hh] = _colmax(st)

    def consume(i, kb, slot):
        kstart = pl.multiple_of(kb * tq, tq)
        for hh in range(2):
            m_old = m_ref[i, hh]
            m_new = jnp.maximum(m_old, mx_ref[slot, hh])
            m_ref[i, hh] = m_new
            pt = jnp.exp2(s_ref[slot, hh] - m_new).astype(BF16)
            pv = _dot(vta_ref[hh, :, pl.ds(kstart, tq)], pt)
            acc_ref[i, hh] = jnp.exp2(m_old - m_new) * acc_ref[i, hh] + pv

    hq = tq // 2
    halves = ((0, hq), (hq, tq))

    def scores_diag(i, slot):
        for hh in range(2):
            for q0, nk in halves:
                st = _dot(keys(hh, i * tq, nk), queries(hh, i * tq + q0, hq))
                key = lax.broadcasted_iota(jnp.int32, (nk, hq), 0)
                qry = lax.broadcasted_iota(jnp.int32, (nk, hq), 1) + q0
                st = jnp.where(key <= qry, st, NEG_INF)
                s_ref[slot, hh, 0:nk, q0:q0 + hq] = st
                mx_ref[slot, hh, :, q0:q0 + hq] = _colmax(st)

    def consume_diag(i, slot):
        for hh in range(2):
            for q0, nk in halves:
                m_new = mx_ref[slot, hh, :, q0:q0 + hq]
                m_ref[i, hh, :, q0:q0 + hq] = m_new
                pt = jnp.exp2(s_ref[slot, hh, 0:nk, q0:q0 + hq] - m_new).astype(BF16)
                acc_ref[i, hh, :, q0:q0 + hq] = _dot(vta_ref[hh, :, i * tq:i * tq + nk], pt)

    scores_diag(0, 0)
    for i in range(nq):
        if i + 1 < nq:
            scores_diag(i + 1, (i + 1) % 2)
        elif nq > 1:
            scores(1, 0, (i + 1) % 2)
        consume_diag(i, i % 2)

    def advance(i, kb):
        last = kb + 1 == i
        return jnp.where(last, i + 1, i), jnp.where(last, 0, kb + 1)

    def trip(_, carry):
        i, kb = carry
        for u in range(unroll):
            ni, nkb = advance(i, kb)
            scores(jnp.minimum(ni, nq - 1), nkb, (nq + u + 1) % 2)
            consume(i, kb, (nq + u) % 2)
            i, kb = ni, nkb
        return i, kb

    n_items = nq * (nq - 1) // 2
    assert unroll % 2 == 0 and n_items % unroll == 0
    lax.fori_loop(0, n_items // unroll, trip, (jnp.int32(1), jnp.int32(0)))

    r = lax.broadcasted_iota(jnp.int32, (LANES, tq), 0)
    for i in range(nq):
        acc_a, acc_b = acc_ref[i, 0], acc_ref[i, 1]
        o_t = jnp.where(r < half, acc_a / acc_a[half:half + 1, :], acc_b / acc_b[0:1, :])
        o_ref[0, i * tq:(i + 1) * tq, :] = o_t.T.astype(o_ref.dtype)


def _flash(qk, vt, kd, *, batch, seq, n_pairs, q_blk, k_blk, sep, tq=512):
    t = qk.shape[1]
    nq = seq // tq
    w = 2 if sep else 1
    kc = 2 * LANES if kd is not None else LANES
    n_items = nq * (nq - 1) // 2
    unroll = max(u for u in (14, 8, 6, 4, 2) if n_items % u == 0)
    in_specs = [
        pl.BlockSpec((w, seq, LANES), lambda b, hp: (q_blk // w + hp, b, 0)),
        pl.BlockSpec((w, seq, LANES), lambda b, hp: (k_blk // w + hp, b, 0)),
        pl.BlockSpec((1, LANES, seq), lambda b, hp: (hp, 0, b)),
    ]
    args = [qk, qk, vt]
    if kd is not None:
        in_specs.append(pl.BlockSpec((1, seq, LANES), lambda b, hp: (hp, b, 0)))
        args.append(kd)
    return pl.pallas_call(
        functools.partial(_flash_kernel, sep=sep, has_bias=kd is not None, tq=tq, unroll=unroll),
        grid=(batch, n_pairs),
        in_specs=in_specs,
        out_specs=pl.BlockSpec((1, seq, LANES), lambda b, hp: (hp, b, 0)),
        out_shape=jax.ShapeDtypeStruct((n_pairs, t, LANES), BF16),
        scratch_shapes=[
            pltpu.VMEM((2, kc, seq), BF16),
            pltpu.VMEM((2, LANES, seq), BF16),
            pltpu.VMEM((nq, 2, LANES, tq), F32),
            pltpu.VMEM((2, 2, tq, tq), F32),
            pltpu.VMEM((2, 2, 1, tq), F32),
            pltpu.VMEM((nq, 2, 1, tq), F32),
        ],
        compiler_params=_params("parallel", "parallel"),
        name="flash_sep" if sep else "flash_shared",
    )(*args)


def _dil_attn_kernel(*refs):
    n_in = 5 * len(DIL_PATTERNS)
    o_ref, og_ref, lg_ref, band_ref = refs[n_in:]
    i = pl.program_id(2)
    w = DIL_WINDOW_KEYS
    v_first = lax.broadcasted_iota(jnp.int32, (w, LANES), 1) < 64
    qi = lax.broadcasted_iota(jnp.int32, (2 * w, 2 * w), 0) % w
    kk = lax.broadcasted_iota(jnp.int32, (2 * w, 2 * w), 1)
    band = jnp.logical_and(kk >= qi, kk <= qi + w)
    band_ref[0] = jnp.where(band, 0.0, NEG_INF)
    band_ref[1] = jnp.where(jnp.logical_and(band, kk >= jnp.where(i > 0, 0, w)), 0.0, NEG_INF)
    ones = jnp.ones((2 * w, LANES), BF16)
    for g, (_, d) in enumerate(DIL_PATTERNS):
        q_ref, k_ref, kp_ref, v_ref, vp_ref = refs[5 * g:5 * g + 5]
        for r in range(d):
            ls = slice(r * LANES, (r + 1) * LANES)
            for sub in range(DIL_TOKENS // d // w):
                rows = slice(sub * w, (sub + 1) * w)
                q_u = jnp.concatenate([q_ref[0, 0, rows, ls], q_ref[1, 0, rows, ls]], axis=0)
                if sub == 0:
                    k_u = jnp.concatenate([kp_ref[0, 0, :, ls], k_ref[0, 0, 0:w, ls]], axis=0)
                    v_u = jnp.concatenate([vp_ref[0, 0, :, ls], v_ref[0, 0, 0:w, ls]], axis=0)
                else:
                    k_u = k_ref[0, 0, (sub - 1) * w:(sub + 1) * w, ls]
                    v_u = v_ref[0, 0, (sub - 1) * w:(sub + 1) * w, ls]
                s = _dot_nt(q_u, k_u) + band_ref[1 if sub == 0 else 0]
                m = jnp.max(s, axis=-1, keepdims=True)
                p = jnp.exp2(s - m).astype(BF16)
                ov = _dot(p, jnp.concatenate([v_u, ones], axis=1))
                num = jnp.where(v_first, ov[0:w, 0:LANES], ov[w:2 * w, 0:LANES])
                den = jnp.where(v_first, ov[0:w, LANES:2 * LANES], ov[w:2 * w, LANES:2 * LANES])
                top = jnp.where(v_first, jnp.broadcast_to(m[0:w], (w, LANES)),
                                jnp.broadcast_to(m[w:2 * w], (w, LANES)))
                tok = pl.ds(sub * w * d + r, w, stride=d) if d > 1 else pl.ds(sub * w, w)
                og_ref[g, tok, :] = num / den
                lg_ref[g, tok, :] = top + jnp.log(den) * LOG2E
    lse = [lg_ref[g] for g in range(len(DIL_PATTERNS))]
    top = functools.reduce(jnp.maximum, lse)
    e = [jnp.exp2(x - top) for x in lse]
    num = sum(e[g] * og_ref[g] for g in range(len(DIL_PATTERNS)))
    o_ref[0] = (num / sum(e)).astype(o_ref.dtype)


def _dil_attn(groups, *, batch, seq):
    n_pairs = DIL_HEADS // 2
    n_tok_blk = seq // DIL_TOKENS
    in_specs, args = [], []
    for arr, (_, d) in zip(groups, DIL_PATTERNS):
        rows = DIL_TOKENS // d
        sub_per_blk = rows // DIL_WINDOW_KEYS

        def cur(base, rows=rows, d=d):
            return pl.BlockSpec((1, 1, rows, d * LANES), lambda b, hp, i: (base + hp, b, i, 0))

        def prev(base, spb=sub_per_blk, d=d):
            return pl.BlockSpec((1, 1, DIL_WINDOW_KEYS, d * LANES),
                                lambda b, hp, i: (base + hp, b, jnp.maximum(i * spb - 1, 0), 0))

        q_spec = pl.BlockSpec((2, 1, rows, d * LANES), lambda b, hp, i: (hp, b, i, 0))
        in_specs += [q_spec, cur(2 * n_pairs), prev(2 * n_pairs), cur(3 * n_pairs), prev(3 * n_pairs)]
        args += [arr] * 5
    n_g = len(DIL_PATTERNS)
    w2 = 2 * DIL_WINDOW_KEYS
    return pl.pallas_call(
        _dil_attn_kernel,
        grid=(batch, n_pairs, n_tok_blk),
        in_specs=in_specs,
        out_specs=pl.BlockSpec((1, DIL_TOKENS, LANES), lambda b, hp, i: (hp, b * n_tok_blk + i, 0)),
        out_shape=jax.ShapeDtypeStruct((n_pairs, batch * seq, LANES), BF16),
        scratch_shapes=[pltpu.VMEM((n_g, DIL_TOKENS, LANES), F32), pltpu.VMEM((n_g, DIL_TOKENS, LANES), F32),
                        pltpu.VMEM((2, w2, w2), F32)],
        compiler_params=_params("parallel", "parallel", "arbitrary"),
        name="dil_attn",
    )(*args)


def _mlp_kernel(h_ref, o_ref, wo_ref, g_ref, wu_ref, wd_ref, fg_ref, out_ref, xn_ref, *, final_norm, tf):
    a = jnp.concatenate([o_ref[k] for k in range(o_ref.shape[0])], axis=-1)
    h1 = h_ref[...] + _dot(a, wo_ref[...])
    xn_ref[...] = _rms(h1, g_ref[...]).astype(BF16)
    y = h1
    for c in range(wu_ref.shape[1] // tf):
        u = jnp.maximum(_dot(xn_ref[...], wu_ref[:, c * tf:(c + 1) * tf]), 0.0)
        y = y + _dot((u * u).astype(BF16), wd_ref[c * tf:(c + 1) * tf, :])
    out_ref[...] = _rms(y, fg_ref[...]) if final_norm else y


def _mlp(h, o, wo, gain, w_up, w_down, final_gain, *, final_norm, tm=512, tf=1024):
    t, d = h.shape
    fixed = lambda shape: pl.BlockSpec(shape, lambda i: (0, 0), pipeline_mode=pl.Buffered(1))
    return pl.pallas_call(
        functools.partial(_mlp_kernel, final_norm=final_norm, tf=tf),
        grid=(t // tm,),
        in_specs=[
            pl.BlockSpec((tm, d), lambda i: (i, 0)),
            pl.BlockSpec((o.shape[0], tm, LANES), lambda i: (0, i, 0)),
            fixed(wo.shape),
            fixed((1, d)),
            fixed(w_up.shape),
            fixed(w_down.shape),
            fixed((1, d)),
        ],
        out_specs=pl.BlockSpec((tm, d), lambda i: (i, 0)),
        out_shape=jax.ShapeDtypeStruct((t, d), F32),
        scratch_shapes=[pltpu.VMEM((tm, d), BF16)],
        compiler_params=_params("parallel"),
        name="mlp_final" if final_norm else "mlp",
    )(h, o, wo, gain.reshape(1, d), w_up, w_down, final_gain.reshape(1, d))


def _rope_halves(seq, dim):
    inv = (1.0 / (np.float32(ROPE_THETA) ** (np.arange(0, dim, 2, dtype=np.float32) / np.float32(dim))))
    ang = np.arange(seq, dtype=np.float32)[:, None] * inv.astype(np.float32)[None, :]
    return np.cos(ang).astype(np.float32), np.sin(ang).astype(np.float32)


def _mla_tables(seq):
    cos, sin = _rope_halves(seq, MLA_ROPE)
    one = lambda n: np.ones((seq, n), np.float32)
    zero = lambda n: np.zeros((seq, n), np.float32)
    cos_t = np.concatenate([one(32), cos, one(16), one(32), cos, one(16)], axis=1)
    sin_t = np.concatenate([zero(32), -sin, zero(16), zero(32), sin, zero(16)], axis=1)
    return jnp.asarray(cos_t), jnp.asarray(sin_t)


def _dil_tables(seq, d, tm):
    cos, sin = _rope_halves(seq, DIL_DIM)
    order = lambda x: x.reshape(seq // tm, tm // d, d, LANES).transpose(0, 2, 1, 3).reshape(seq, LANES)
    return (jnp.asarray(order(np.concatenate([cos] * 4, axis=1))),
            jnp.asarray(order(np.concatenate([-sin, -sin, sin, sin], axis=1))))


def _relayout(w3, moves, n_dst):
    p = np.zeros((w3.shape[-1], n_dst), np.float32)
    for src, dst in moves:
        p[src, dst] = 1.0
    return jnp.einsum('...c,cl->...l', w3, jnp.asarray(p), preferred_element_type=F32).astype(BF16)


def _mla_weights(wq_a, wq_b, wkv_a, wkv_b):
    lanes = lambda c: c if c < 32 else 32 + c if c < 64 else c - 32 if c < 80 else c + 16
    kpe_moves = [(c, c) for c in range(MLA_KV_RANK)] + [
        (MLA_KV_RANK + c, MLA_KV_RANK + lanes(MLA_NOPE + c)) for c in range(MLA_ROPE)]
    w_a = jnp.concatenate([wq_a.astype(BF16), _relayout(wkv_a, kpe_moves, MLA_KV_RANK + LANES)], axis=1)
    scale = (MLA_NOPE + MLA_ROPE) ** -0.5 * LOG2E
    q3 = wq_b.reshape(MLA_Q_RANK, MLA_HEADS, MLA_NOPE + MLA_ROPE) * scale
    wq = _relayout(q3, [(c, lanes(c)) for c in range(MLA_NOPE + MLA_ROPE)], LANES)
    kv3 = wkv_b.reshape(MLA_KV_RANK, MLA_HEADS, MLA_NOPE + MLA_V)
    wk = _relayout(kv3, [(c, lanes(c)) for c in range(MLA_NOPE)], LANES)
    wvt = kv3[..., MLA_NOPE:].reshape(MLA_KV_RANK, -1).T.astype(BF16)
    return w_a, wq.reshape(MLA_Q_RANK, -1), wk.reshape(MLA_KV_RANK, -1), wvt


def _dil_weights(w_qkv):
    d = w_qkv.shape[0]
    n_qk = 2 * len(DIL_PATTERNS) * DIL_HEADS * DIL_DIM
    qk = w_qkv[:, :n_qk].reshape(d, 2, n_qk // (2 * LANES), LANES)
    qk = qk * jnp.array([DIL_DIM ** -0.5 * LOG2E, 1.0], F32).reshape(1, 2, 1, 1)
    old = np.arange(LANES).reshape(2, 2, 32).transpose(1, 0, 2).reshape(LANES)
    qk = _relayout(qk, [(int(old[j]), j) for j in range(LANES)], LANES)
    return qk.reshape(d, n_qk), w_qkv[:, n_qk:].astype(BF16)


def _mla_attn(h, attn_norm, wq_a, q_norm, wq_b, wkv_a, kv_norm, wkv_b, tables, *, batch, seq):
    w_a, wq, wk, wvt = _mla_weights(wq_a, wq_b, wkv_a, wkv_b)
    qk, vt = _mla_proj(h, attn_norm, w_a, q_norm, kv_norm, wq, wk, wvt, *tables, seq=seq)
    return _flash(qk, vt, None, batch=batch, seq=seq, n_pairs=MLA_HEADS // 2,
                  q_blk=0, k_blk=MLA_HEADS, sep=True)


def _fox_attn(h, attn_norm, w_qkv, w_f, b_f, *, batch, seq):
    n = FOX_HEADS * FOX_DIM
    d = w_qkv.shape[0]
    wqk = jnp.concatenate([w_qkv[:, :n] * (FOX_DIM ** -0.5 * LOG2E), w_qkv[:, n:2 * n]], axis=1)
    wvt = w_qkv[:, 2 * n:].T
    wf = jnp.concatenate([w_f, jnp.zeros((d, LANES - FOX_HEADS), F32)], axis=1)
    bf = jnp.concatenate([b_f.astype(F32), jnp.zeros((LANES - FOX_HEADS,), F32)]).reshape(1, LANES)
    qk, vt, kd = _fox_proj(h, attn_norm, wqk.astype(BF16), wvt.astype(BF16), wf.astype(BF16), bf,
                           batch=batch, seq=seq)
    n_pairs = FOX_HEADS // 2
    return _flash(qk, vt, kd, batch=batch, seq=seq, n_pairs=n_pairs, q_blk=0, k_blk=n_pairs, sep=False)


def _dil_attn_layer(h, attn_norm, w_qkv, *, batch, seq, tm=512):
    wqk, wv = _dil_weights(w_qkv)
    groups = [_dil_proj(h, attn_norm, wqk, wv, *_dil_tables(seq, d, tm),
                        batch=batch, seq=seq, group=g, d=d, tm=tm)
              for g, (_, d) in enumerate(DIL_PATTERNS)]
    return _dil_attn(groups, batch=batch, seq=seq)


def kernel(x, l0_attn_norm, l0_mla_wq_a, l0_mla_q_norm, l0_mla_wq_b, l0_mla_wkv_a, l0_mla_kv_norm, l0_mla_wkv_b, l0_mla_wo, l0_mlp_norm, l0_w_up, l0_w_down, l1_attn_norm, l1_fox_w_qkv, l1_fox_w_f, l1_fox_b_f, l1_fox_wo, l1_mlp_norm, l1_w_up, l1_w_down, l2_attn_norm, l2_dil_w_qkv, l2_dil_wo, l2_mlp_norm, l2_w_up, l2_w_down, l3_attn_norm, l3_mla_wq_a, l3_mla_q_norm, l3_mla_wq_b, l3_mla_wkv_a, l3_mla_kv_norm, l3_mla_wkv_b, l3_mla_wo, l3_mlp_norm, l3_w_up, l3_w_down, final_norm):
    batch, seq, d = x.shape
    assert seq % DIL_TOKENS == 0 and (batch * seq) % 1024 == 0
    for window, dil in DIL_PATTERNS:
        assert window // dil == DIL_WINDOW_KEYS
    mla_t = _mla_tables(seq)
    kw = dict(batch=batch, seq=seq)
    bf = lambda w: w.astype(BF16)
    h = x.reshape(batch * seq, d)

    o = _mla_attn(h, l0_attn_norm, l0_mla_wq_a, l0_mla_q_norm, l0_mla_wq_b, l0_mla_wkv_a,
                  l0_mla_kv_norm, l0_mla_wkv_b, mla_t, **kw)
    h = _mlp(h, o, bf(l0_mla_wo), l0_mlp_norm, bf(l0_w_up), bf(l0_w_down), final_norm, final_norm=False)

    o = _fox_attn(h, l1_attn_norm, l1_fox_w_qkv, l1_fox_w_f, l1_fox_b_f, **kw)
    h = _mlp(h, o, bf(l1_fox_wo), l1_mlp_norm, bf(l1_w_up), bf(l1_w_down), final_norm, final_norm=False)

    o = _dil_attn_layer(h, l2_attn_norm, l2_dil_w_qkv, **kw)
    h = _mlp(h, o, bf(l2_dil_wo), l2_mlp_norm, bf(l2_w_up), bf(l2_w_down), final_norm, final_norm=False)

    o = _mla_attn(h, l3_attn_norm, l3_mla_wq_a, l3_mla_q_norm, l3_mla_wq_b, l3_mla_wkv_a,
                  l3_mla_kv_norm, l3_mla_wkv_b, mla_t, **kw)
    h = _mlp(h, o, bf(l3_mla_wo), l3_mlp_norm, bf(l3_w_up), bf(l3_w_down), final_norm, final_norm=True)
    return h.reshape(batch, seq, d)
```

```python
import functools
import math

import numpy as np
import jax
import jax.numpy as jnp
from jax import lax
from jax.experimental import pallas as pl
from jax.experimental.pallas import tpu as pltpu

F32 = jnp.float32
BF16 = jnp.bfloat16

LANES = 128
NORM_EPS = 1e-6
NEG_INF = -1e30
ROPE_THETA = 10000.0
LOG2E = math.log2(math.e)

MLA_HEADS = 16
MLA_Q_RANK = 384
MLA_KV_RANK = 256
MLA_NOPE = 64
MLA_ROPE = 32
MLA_V = 64
FOX_HEADS = 16
FOX_DIM = 64
DIL_PATTERNS = ((128, 1), (512, 4), (2048, 16))
DIL_HEADS = 16
DIL_DIM = 64
DIL_WINDOW_KEYS = 128
DIL_TOKENS = 2048
GATE_PARTS = 3

VMEM_LIMIT = 56 * 2**20


def _params(*sem):
    return pltpu.CompilerParams(dimension_semantics=sem, vmem_limit_bytes=VMEM_LIMIT)


def _rms(x, g):
    ms = jnp.mean(x * x, axis=-1, keepdims=True)
    return x * lax.rsqrt(ms + NORM_EPS) * g


def _dot(a, b):
    return jnp.dot(a, b, preferred_element_type=F32)


def _dot_nt(a, b):
    return lax.dot_general(a, b, (((1,), (1,)), ((), ())), preferred_element_type=F32)


def _rope(blk, cos, sin):
    return blk * cos + pltpu.roll(blk, 64, axis=1) * sin


def _colmax(st):
    n, c = st.shape
    return jnp.max(jnp.max(st.reshape(8, n // 8, c), axis=0), axis=0, keepdims=True)


def _split3(x):
    hi = x.astype(BF16)
    r1 = x - hi.astype(F32)
    mid = r1.astype(BF16)
    lo = (r1 - mid.astype(F32)).astype(BF16)
    return hi, mid, lo


def _dil_proj_kernel(x_ref, g_ref, wq_ref, wk_ref, wv_ref, cos_ref, sin_ref, o_ref, xs_ref, xn_ref, *, d, tm):
    rpr = tm // d
    nc = x_ref.shape[1] // LANES
    xn = _rms(x_ref[...], g_ref[...])
    if d == 1:
        xn_ref[...] = xn.astype(BF16)
    else:
        for c in range(nc):
            xs_ref[c] = xn[:, c * LANES:(c + 1) * LANES]
        for c in range(nc):
            for r in range(d):
                xn_ref[r * rpr:(r + 1) * rpr, c * LANES:(c + 1) * LANES] = (
                    xs_ref[c, pl.ds(r, rpr, stride=d), :].astype(BF16))

    nb = o_ref.shape[0] // 4
    lane = lax.broadcasted_iota(jnp.int32, (1, LANES), 1)
    first = jnp.where((lane % 64) < 32, 1.0, 0.0)
    for j in range(3):
        acc = _dot(xn_ref[...], (wq_ref, wk_ref, wv_ref)[j][...])
        for c in range(nb):
            for r in range(d):
                rs = slice(r * rpr, (r + 1) * rpr)
                ls = slice(r * LANES, (r + 1) * LANES)
                blk = acc[rs, c * LANES:(c + 1) * LANES]
                if j < 2:
                    blk = _rope(blk, cos_ref[rs, :], sin_ref[rs, :])
                if j == 0:
                    o_ref[2 * c, 0, :, ls] = (blk * first).astype(o_ref.dtype)
                    o_ref[2 * c + 1, 0, :, ls] = (blk * (1.0 - first)).astype(o_ref.dtype)
                else:
                    o_ref[(j + 1) * nb + c, 0, :, ls] = blk.astype(o_ref.dtype)


def _dil_proj(h, gain, wqk, wv, cos_t, sin_t, *, batch, seq, group, d, tm):
    t, dm = h.shape
    ns = seq // tm
    n_g = len(DIL_PATTERNS)
    n = wv.shape[1] // n_g
    nb = 4 * n // LANES
    return pl.pallas_call(
        functools.partial(_dil_proj_kernel, d=d, tm=tm),
        grid=(batch, ns),
        in_specs=[
            pl.BlockSpec((tm, dm), lambda b, i: (b * ns + i, 0)),
            pl.BlockSpec((1, dm), lambda b, i: (0, 0)),
            pl.BlockSpec((dm, n), lambda b, i: (0, group)),
            pl.BlockSpec((dm, n), lambda b, i: (0, n_g + group)),
            pl.BlockSpec((dm, n), lambda b, i: (0, group)),
            pl.BlockSpec((tm, LANES), lambda b, i: (i, 0)),
            pl.BlockSpec((tm, LANES), lambda b, i: (i, 0)),
        ],
        out_specs=pl.BlockSpec((nb, 1, tm // d, d * LANES), lambda b, i: (0, b, i, 0)),
        out_shape=jax.ShapeDtypeStruct((nb, batch, seq // d, d * LANES), BF16),
        scratch_shapes=[pltpu.VMEM((dm // LANES, tm, LANES), F32), pltpu.VMEM((tm, dm), BF16)],
        compiler_params=_params("parallel", "parallel"),
        name="dil_proj_d%d" % d,
    )(h, gain.reshape(1, dm), wqk, wqk, wv, cos_t, sin_t)


def _mla_proj_kernel(x_ref, g_ref, wa_ref, qn_ref, kvn_ref, wq_ref, wk_ref, wvt_ref, cos_ref, sin_ref,
                     qk_ref, vt_ref):
    cos, sin = cos_ref[...], sin_ref[...]
    xn = _rms(x_ref[...], g_ref[...]).astype(BF16)
    y = _dot(xn, wa_ref[...])
    cq = _rms(y[:, :MLA_Q_RANK], qn_ref[...]).astype(BF16)
    ckv = _rms(y[:, MLA_Q_RANK:MLA_Q_RANK + MLA_KV_RANK], kvn_ref[...]).astype(BF16)
    kpe = _rope(y[:, MLA_Q_RANK + MLA_KV_RANK:], cos, sin)
    q = _dot(cq, wq_ref[...])
    for c in range(MLA_HEADS):
        qk_ref[c] = _rope(q[:, c * LANES:(c + 1) * LANES], cos, sin).astype(BF16)
    k = _dot(ckv, wk_ref[...])
    for c in range(MLA_HEADS):
        qk_ref[MLA_HEADS + c] = (k[:, c * LANES:(c + 1) * LANES] + kpe).astype(BF16)
    vt = _dot_nt(wvt_ref[...], ckv)
    for c in range(MLA_HEADS // 2):
        vt_ref[c] = vt[c * LANES:(c + 1) * LANES, :].astype(BF16)


def _mla_proj(h, gain, w_a, q_norm, kv_norm, wq, wk, wvt, cos_t, sin_t, *, seq, tm=512):
    t, d = h.shape
    ns = seq // tm
    fixed = lambda shape: pl.BlockSpec(shape, lambda i: (0, 0))
    tab = pl.BlockSpec((tm, LANES), lambda i: (i % ns, 0))
    return pl.pallas_call(
        _mla_proj_kernel,
        grid=(t // tm,),
        in_specs=[
            pl.BlockSpec((tm, d), lambda i: (i, 0)),
            fixed((1, d)),
            fixed(w_a.shape),
            fixed((1, MLA_Q_RANK)),
            fixed((1, MLA_KV_RANK)),
            fixed(wq.shape),
            fixed(wk.shape),
            fixed(wvt.shape),
            tab,
            tab,
        ],
        out_specs=[
            pl.BlockSpec((2 * MLA_HEADS, tm, LANES), lambda i: (0, i, 0)),
            pl.BlockSpec((MLA_HEADS // 2, LANES, tm), lambda i: (0, 0, i)),
        ],
        out_shape=[
            jax.ShapeDtypeStruct((2 * MLA_HEADS, t, LANES), BF16),
            jax.ShapeDtypeStruct((MLA_HEADS // 2, LANES, t), BF16),
        ],
        compiler_params=_params("parallel"),
        name="mla_proj",
    )(h, gain.reshape(1, d), w_a, q_norm.reshape(1, -1), kv_norm.reshape(1, -1), wq, wk, wvt, cos_t, sin_t)


def _fox_proj_kernel(x_ref, g_ref, wqk_ref, wvt_ref, wf_ref, bf_ref, tri_ref, scat_ref,
                     qk_ref, vt_ref, kd_ref, carry_ref, *, tm):
    @pl.when(pl.program_id(1) == 0)
    def _():
        carry_ref[...] = jnp.zeros_like(carry_ref)

    xn = _rms(x_ref[...], g_ref[...]).astype(BF16)
    qk = _dot(xn, wqk_ref[...])
    for c in range(qk_ref.shape[0]):
        qk_ref[c] = qk[:, c * LANES:(c + 1) * LANES].astype(BF16)
    vt = _dot_nt(wvt_ref[...], xn)
    for c in range(vt_ref.shape[0]):
        vt_ref[c] = vt[c * LANES:(c + 1) * LANES, :].astype(BF16)

    heads = jnp.where(lax.broadcasted_iota(jnp.int32, (1, LANES), 1) < FOX_HEADS, 1.0, 0.0)

    def pack3(x):
        hi, mid, lo = (p.astype(F32) for p in _split3(x * heads))
        return (hi + pltpu.roll(mid, FOX_HEADS, axis=1) + pltpu.roll(lo, 2 * FOX_HEADS, axis=1)).astype(BF16)

    z = _dot(xn, wf_ref[...]) + bf_ref[...]
    logf = jnp.minimum(z, 0.0) - jnp.log(1.0 + jnp.exp(-jnp.abs(z)))
    c3 = _dot(tri_ref[...], pack3(logf))
    cum = (c3 + pltpu.roll(c3, LANES - FOX_HEADS, axis=1)
           + pltpu.roll(c3, LANES - 2 * FOX_HEADS, axis=1)) * heads + carry_ref[...]
    carry_ref[...] = cum[tm - 1:tm, :]
    kd = _dot(pack3(cum * (-LOG2E)), scat_ref[...])
    for c in range(kd_ref.shape[0]):
        kd_ref[c] = kd[:, c * LANES:(c + 1) * LANES].astype(BF16)


def _fox_proj(h, gain, wqk, wvt, wf, bf, *, batch, seq, tm=512):
    t, d = h.shape
    ns = seq // tm
    n_pairs = FOX_HEADS // 2
    tri = jnp.asarray(np.tril(np.ones((tm, tm), np.float32)), BF16)
    scat = np.zeros((LANES, n_pairs * LANES), np.float32)
    for p in range(GATE_PARTS):
        for hd in range(FOX_HEADS):
            scat[p * FOX_HEADS + hd, (hd // 2) * LANES + GATE_PARTS * (hd % 2) + p] = 1.0
    fixed2 = lambda b, i: (0, 0)
    rows = lambda b, i: (0, b * ns + i, 0)
    return pl.pallas_call(
        functools.partial(_fox_proj_kernel, tm=tm),
        grid=(batch, ns),
        in_specs=[
            pl.BlockSpec((tm, d), lambda b, i: (b * ns + i, 0)),
            pl.BlockSpec((1, d), fixed2),
            pl.BlockSpec(wqk.shape, fixed2),
            pl.BlockSpec(wvt.shape, fixed2),
            pl.BlockSpec(wf.shape, fixed2),
            pl.BlockSpec((1, LANES), fixed2),
            pl.BlockSpec((tm, tm), fixed2),
            pl.BlockSpec(scat.shape, fixed2),
        ],
        out_specs=[
            pl.BlockSpec((2 * n_pairs, tm, LANES), rows),
            pl.BlockSpec((n_pairs, LANES, tm), lambda b, i: (0, 0, b * ns + i)),
            pl.BlockSpec((n_pairs, tm, LANES), rows),
        ],
        out_shape=[
            jax.ShapeDtypeStruct((2 * n_pairs, t, LANES), BF16),
            jax.ShapeDtypeStruct((n_pairs, LANES, t), BF16),
            jax.ShapeDtypeStruct((n_pairs, t, LANES), BF16),
        ],
        scratch_shapes=[pltpu.VMEM((1, LANES), F32)],
        compiler_params=_params("arbitrary", "arbitrary"),
        name="fox_proj",
    )(h, gain.reshape(1, d), wqk, wvt, wf, bf, tri, jnp.asarray(scat, BF16))


def _flash_kernel(*refs, sep, has_bias, tq, unroll):
    n_in = 4 if has_bias else 3
    q_ref, k_ref, vt_ref = refs[:3]
    kd_ref = refs[3] if has_bias else None
    o_ref, qs_ref, vta_ref, acc_ref, s_ref, mx_ref, m_ref = refs[n_in:]
    half = LANES // 2
    nq = acc_ref.shape[0]
    seq = nq * tq

    vt = vt_ref[0]
    r = lax.broadcasted_iota(jnp.int32, vt.shape, 0)
    one = jnp.ones_like(vt)
    vta_ref[0, :, 0:seq] = jnp.where(r < half, vt, one)
    vta_ref[1, :, 0:seq] = jnp.where(r < half, one, vt)
    row = lax.broadcasted_iota(jnp.int32, (LANES, tq), 0)
    for hh in range(2):
        for t in range(nq):
            cols = slice(t * tq, (t + 1) * tq)
            qt = q_ref[hh if sep else 0, cols, :].astype(F32).T
            if not sep:
                qt = jnp.where((row < half) if hh == 0 else (row >= half), qt, 0.0)
            qs_ref[hh, 0:LANES, cols] = qt.astype(BF16)
        if has_bias:
            r2 = lax.broadcasted_iota(jnp.int32, (LANES, seq), 0)
            sel = jnp.logical_and(r2 >= GATE_PARTS * hh, r2 < GATE_PARTS * (hh + 1))
            qs_ref[hh, LANES:2 * LANES, 0:seq] = jnp.where(sel, 1.0, 0.0).astype(BF16)

    def queries(hh, start, n):
        return qs_ref[hh, :, pl.ds(start, n)]

    def keys(hh, start, n):
        k_blk = k_ref[hh if sep else 0, pl.ds(start, n), :]
        if has_bias:
            k_blk = jnp.concatenate([k_blk, kd_ref[0, pl.ds(start, n), :]], axis=1)
        return k_blk

    def scores(i, kb, slot):
        qstart, kstart = pl.multiple_of(i * tq, tq), pl.multiple_of(kb * tq, tq)
        for hh in range(2):
            st = _dot(keys(hh, kstart, tq), queries(hh, qstart, tq))
            s_ref[slot, hh, :, 0:tq] = st
            mx_ref[slot, hh] = _colmax(st)

    def consume(i, kb, slot):
        kstart = pl.multiple_of(kb * tq, tq)
        for hh in range(2):
            m_old = m_ref[i, hh]
            m_new = jnp.maximum(m_old, mx_ref[slot, hh])
            m_ref[i, hh] = m_new
            pt = jnp.exp2(s_ref[slot, hh, :, 0:tq] - m_new).astype(BF16)
            pv = _dot(vta_ref[hh, :, pl.ds(kstart, tq)], pt)
            acc_ref[i, hh, :, 0:tq] = jnp.exp2(m_old - m_new) * acc_ref[i, hh, :, 0:tq] + pv

    hq = tq // 2
    halves = ((0, hq), (hq, tq))

    def scores_diag(i, slot):
        for hh in range(2):
            for q0, nk in halves:
                st = _dot(keys(hh, i * tq, nk), queries(hh, i * tq + q0, hq))
                key = lax.broadcasted_iota(jnp.int32, (nk, hq), 0)
                qry = lax.broadcasted_iota(jnp.int32, (nk, hq), 1) + q0
                st = jnp.where(key <= qry, st, NEG_INF)
                s_ref[slot, hh, 0:nk, q0:q0 + hq] = st
                mx_ref[slot, hh, :, q0:q0 + hq] = _colmax(st)

    def consume_diag(i, slot):
        for hh in range(2):
            for q0, nk in halves:
                m_new = mx_ref[slot, hh, :, q0:q0 + hq]
                m_ref[i, hh, :, q0:q0 + hq] = m_new
                pt = jnp.exp2(s_ref[slot, hh, 0:nk, q0:q0 + hq] - m_new).astype(BF16)
                acc_ref[i, hh, :, q0:q0 + hq] = _dot(vta_ref[hh, :, i * tq:i * tq + nk], pt)

    scores_diag(0, 0)
    for i in range(nq):
        if i + 1 < nq:
            scores_diag(i + 1, (i + 1) % 2)
        elif nq > 1:
            scores(1, 0, (i + 1) % 2)
        consume_diag(i, i % 2)

    def advance(i, kb):
        last = kb + 1 == i
        return jnp.where(last, i + 1, i), jnp.where(last, 0, kb + 1)

    def trip(_, carry):
        i, kb = carry
        for u in range(unroll):
            ni, nkb = advance(i, kb)
            scores(jnp.minimum(ni, nq - 1), nkb, (nq + u + 1) % 2)
            consume(i, kb, (nq + u) % 2)
            i, kb = ni, nkb
        return i, kb

    n_items = nq * (nq - 1) // 2
    assert unroll % 2 == 0 and n_items % unroll == 0
    lax.fori_loop(0, n_items // unroll, trip, (jnp.int32(1), jnp.int32(0)))

    r = lax.broadcasted_iota(jnp.int32, (LANES, tq), 0)
    for i in range(nq):
        acc_a, acc_b = acc_ref[i, 0, :, 0:tq], acc_ref[i, 1, :, 0:tq]
        o_t = jnp.where(r < half, acc_a / acc_a[half:half + 1, :], acc_b / acc_b[0:1, :])
        o_ref[0, i * tq:(i + 1) * tq, :] = o_t.T.astype(o_ref.dtype)


def _flash(qk, vt, kd, *, batch, seq, n_pairs, q_blk, k_blk, sep, tq=512):
    t = qk.shape[1]
    nq = seq // tq
    w = 2 if sep else 1
    kc = 2 * LANES if kd is not None else LANES
    n_items = nq * (nq - 1) // 2
    unroll = max(u for u in (14, 8, 6, 4, 2) if n_items % u == 0)
    in_specs = [
        pl.BlockSpec((w, seq, LANES), lambda b, hp: (q_blk // w + hp, b, 0)),
        pl.BlockSpec((w, seq, LANES), lambda b, hp: (k_blk // w + hp, b, 0)),
        pl.BlockSpec((1, LANES, seq), lambda b, hp: (hp, 0, b)),
    ]
    args = [qk, qk, vt]
    if kd is not None:
        in_specs.append(pl.BlockSpec((1, seq, LANES), lambda b, hp: (hp, b, 0)))
        args.append(kd)
    return pl.pallas_call(
        functools.partial(_flash_kernel, sep=sep, has_bias=kd is not None, tq=tq, unroll=unroll),
        grid=(batch, n_pairs),
        in_specs=in_specs,
        out_specs=pl.BlockSpec((1, seq, LANES), lambda b, hp: (hp, b, 0)),
        out_shape=jax.ShapeDtypeStruct((n_pairs, t, LANES), BF16),
        scratch_shapes=[
            pltpu.VMEM((2, kc, seq + LANES), BF16),
            pltpu.VMEM((2, LANES, seq + LANES), BF16),
            pltpu.VMEM((nq, 2, LANES, tq + LANES), F32),
            pltpu.VMEM((2, 2, tq, tq + LANES), F32),
            pltpu.VMEM((2, 2, 1, tq), F32),
            pltpu.VMEM((nq, 2, 1, tq), F32),
        ],
        compiler_params=_params("parallel", "parallel"),
        name="flash_sep" if sep else "flash_shared",
    )(*args)


def _dil_attn_kernel(*refs):
    n_in = 5 * len(DIL_PATTERNS)
    o_ref, og_ref, lg_ref, band_ref = refs[n_in:]
    i = pl.program_id(2)
    w = DIL_WINDOW_KEYS
    v_first = lax.broadcasted_iota(jnp.int32, (w, LANES), 1) < 64
    qi = lax.broadcasted_iota(jnp.int32, (2 * w, 2 * w), 0) % w
    kk = lax.broadcasted_iota(jnp.int32, (2 * w, 2 * w), 1)
    band = jnp.logical_and(kk >= qi, kk <= qi + w)
    band_ref[0] = jnp.where(band, 0.0, NEG_INF)
    band_ref[1] = jnp.where(jnp.logical_and(band, kk >= jnp.where(i > 0, 0, w)), 0.0, NEG_INF)
    ones = jnp.ones((2 * w, LANES), BF16)
    for g, (_, d) in enumerate(DIL_PATTERNS):
        q_ref, k_ref, kp_ref, v_ref, vp_ref = refs[5 * g:5 * g + 5]
        for r in range(d):
            ls = slice(r * LANES, (r + 1) * LANES)
            for sub in range(DIL_TOKENS // d // w):
                rows = slice(sub * w, (sub + 1) * w)
                q_u = jnp.concatenate([q_ref[0, 0, rows, ls], q_ref[1, 0, rows, ls]], axis=0)
                if sub == 0:
                    k_u = jnp.concatenate([kp_ref[0, 0, :, ls], k_ref[0, 0, 0:w, ls]], axis=0)
                    v_u = jnp.concatenate([vp_ref[0, 0, :, ls], v_ref[0, 0, 0:w, ls]], axis=0)
                else:
                    k_u = k_ref[0, 0, (sub - 1) * w:(sub + 1) * w, ls]
                    v_u = v_ref[0, 0, (sub - 1) * w:(sub + 1) * w, ls]
                s = _dot_nt(q_u, k_u) + band_ref[1 if sub == 0 else 0]
                m = jnp.max(s, axis=-1, keepdims=True)
                p = jnp.exp2(s - m).astype(BF16)
                ov = _dot(p, jnp.concatenate([v_u, ones], axis=1))
                num = jnp.where(v_first, ov[0:w, 0:LANES], ov[w:2 * w, 0:LANES])
                den = jnp.where(v_first, ov[0:w, LANES:2 * LANES], ov[w:2 * w, LANES:2 * LANES])
                top = jnp.where(v_first, jnp.broadcast_to(m[0:w], (w, LANES)),
                                jnp.broadcast_to(m[w:2 * w], (w, LANES)))
                tok = pl.ds(sub * w * d + r, w, stride=d) if d > 1 else pl.ds(sub * w, w)
                og_ref[g, tok, :] = num / den
                lg_ref[g, tok, :] = top + jnp.log(den) * LOG2E
    lse = [lg_ref[g] for g in range(len(DIL_PATTERNS))]
    top = functools.reduce(jnp.maximum, lse)
    e = [jnp.exp2(x - top) for x in lse]
    num = sum(e[g] * og_ref[g] for g in range(len(DIL_PATTERNS)))
    o_ref[0] = (num / sum(e)).astype(o_ref.dtype)


def _dil_attn(groups, *, batch, seq):
    n_pairs = DIL_HEADS // 2
    n_tok_blk = seq // DIL_TOKENS
    in_specs, args = [], []
    for arr, (_, d) in zip(groups, DIL_PATTERNS):
        rows = DIL_TOKENS // d
        sub_per_blk = rows // DIL_WINDOW_KEYS

        def cur(base, rows=rows, d=d):
            return pl.BlockSpec((1, 1, rows, d * LANES), lambda b, hp, i: (base + hp, b, i, 0))

        def prev(base, spb=sub_per_blk, d=d):
            return pl.BlockSpec((1, 1, DIL_WINDOW_KEYS, d * LANES),
                                lambda b, hp, i: (base + hp, b, jnp.maximum(i * spb - 1, 0), 0))

        q_spec = pl.BlockSpec((2, 1, rows, d * LANES), lambda b, hp, i: (hp, b, i, 0))
        in_specs += [q_spec, cur(2 * n_pairs), prev(2 * n_pairs), cur(3 * n_pairs), prev(3 * n_pairs)]
        args += [arr] * 5
    n_g = len(DIL_PATTERNS)
    w2 = 2 * DIL_WINDOW_KEYS
    return pl.pallas_call(
        _dil_attn_kernel,
        grid=(batch, n_pairs, n_tok_blk),
        in_specs=in_specs,
        out_specs=pl.BlockSpec((1, DIL_TOKENS, LANES), lambda b, hp, i: (hp, b * n_tok_blk + i, 0)),
        out_shape=jax.ShapeDtypeStruct((n_pairs, batch * seq, LANES), BF16),
        scratch_shapes=[pltpu.VMEM((n_g, DIL_TOKENS, LANES), F32), pltpu.VMEM((n_g, DIL_TOKENS, LANES), F32),
                        pltpu.VMEM((2, w2, w2), F32)],
        compiler_params=_params("parallel", "parallel", "arbitrary"),
        name="dil_attn",
    )(*args)


def _mlp_kernel(h_ref, o_ref, wo_ref, g_ref, wu_ref, wd_ref, fg_ref, out_ref, xn_ref, *, final_norm, tf):
    a = jnp.concatenate([o_ref[k] for k in range(o_ref.shape[0])], axis=-1)
    h1 = h_ref[...] + _dot(a, wo_ref[...])
    xn_ref[...] = _rms(h1, g_ref[...]).astype(BF16)
    y = h1
    for c in range(wu_ref.shape[1] // tf):
        u = jnp.maximum(_dot(xn_ref[...], wu_ref[:, c * tf:(c + 1) * tf]), 0.0)
        y = y + _dot((u * u).astype(BF16), wd_ref[c * tf:(c + 1) * tf, :])
    out_ref[...] = _rms(y, fg_ref[...]) if final_norm else y


def _mlp(h, o, wo, gain, w_up, w_down, final_gain, *, final_norm, tm=512, tf=1024):
    t, d = h.shape
    fixed = lambda shape: pl.BlockSpec(shape, lambda i: (0, 0), pipeline_mode=pl.Buffered(1))
    return pl.pallas_call(
        functools.partial(_mlp_kernel, final_norm=final_norm, tf=tf),
        grid=(t // tm,),
        in_specs=[
            pl.BlockSpec((tm, d), lambda i: (i, 0)),
            pl.BlockSpec((o.shape[0], tm, LANES), lambda i: (0, i, 0)),
            fixed(wo.shape),
            fixed((1, d)),
            fixed(w_up.shape),
            fixed(w_down.shape),
            fixed((1, d)),
        ],
        out_specs=pl.BlockSpec((tm, d), lambda i: (i, 0)),
        out_shape=jax.ShapeDtypeStruct((t, d), F32),
        scratch_shapes=[pltpu.VMEM((tm, d), BF16)],
        compiler_params=_params("parallel"),
        name="mlp_final" if final_norm else "mlp",
    )(h, o, wo, gain.reshape(1, d), w_up, w_down, final_gain.reshape(1, d))


def _rope_halves(seq, dim):
    inv = (1.0 / (np.float32(ROPE_THETA) ** (np.arange(0, dim, 2, dtype=np.float32) / np.float32(dim))))
    ang = np.arange(seq, dtype=np.float32)[:, None] * inv.astype(np.float32)[None, :]
    return np.cos(ang).astype(np.float32), np.sin(ang).astype(np.float32)


def _mla_tables(seq):
    cos, sin = _rope_halves(seq, MLA_ROPE)
    one = lambda n: np.ones((seq, n), np.float32)
    zero = lambda n: np.zeros((seq, n), np.float32)
    cos_t = np.concatenate([one(32), cos, one(16), one(32), cos, one(16)], axis=1)
    sin_t = np.concatenate([zero(32), -sin, zero(16), zero(32), sin, zero(16)], axis=1)
    return jnp.asarray(cos_t), jnp.asarray(sin_t)


def _dil_tables(seq, d, tm):
    cos, sin = _rope_halves(seq, DIL_DIM)
    order = lambda x: x.reshape(seq // tm, tm // d, d, LANES).transpose(0, 2, 1, 3).reshape(seq, LANES)
    return (jnp.asarray(order(np.concatenate([cos] * 4, axis=1))),
            jnp.asarray(order(np.concatenate([-sin, -sin, sin, sin], axis=1))))


def _relayout(w3, moves, n_dst):
    p = np.zeros((w3.shape[-1], n_dst), np.float32)
    for src, dst in moves:
        p[src, dst] = 1.0
    return jnp.einsum('...c,cl->...l', w3, jnp.asarray(p), preferred_element_type=F32).astype(BF16)


def _mla_weights(wq_a, wq_b, wkv_a, wkv_b):
    lanes = lambda c: c if c < 32 else 32 + c if c < 64 else c - 32 if c < 80 else c + 16
    kpe_moves = [(c, c) for c in range(MLA_KV_RANK)] + [
        (MLA_KV_RANK + c, MLA_KV_RANK + lanes(MLA_NOPE + c)) for c in range(MLA_ROPE)]
    w_a = jnp.concatenate([wq_a.astype(BF16), _relayout(wkv_a, kpe_moves, MLA_KV_RANK + LANES)], axis=1)
    scale = (MLA_NOPE + MLA_ROPE) ** -0.5 * LOG2E
    q3 = wq_b.reshape(MLA_Q_RANK, MLA_HEADS, MLA_NOPE + MLA_ROPE) * scale
    wq = _relayout(q3, [(c, lanes(c)) for c in range(MLA_NOPE + MLA_ROPE)], LANES)
    kv3 = wkv_b.reshape(MLA_KV_RANK, MLA_HEADS, MLA_NOPE + MLA_V)
    wk = _relayout(kv3, [(c, lanes(c)) for c in range(MLA_NOPE)], LANES)
    wvt = kv3[..., MLA_NOPE:].reshape(MLA_KV_RANK, -1).T.astype(BF16)
    return w_a, wq.reshape(MLA_Q_RANK, -1), wk.reshape(MLA_KV_RANK, -1), wvt


def _dil_weights(w_qkv):
    d = w_qkv.shape[0]
    n_qk = 2 * len(DIL_PATTERNS) * DIL_HEADS * DIL_DIM
    qk = w_qkv[:, :n_qk].reshape(d, 2, n_qk // (2 * LANES), LANES)
    qk = qk * jnp.array([DIL_DIM ** -0.5 * LOG2E, 1.0], F32).reshape(1, 2, 1, 1)
    old = np.arange(LANES).reshape(2, 2, 32).transpose(1, 0, 2).reshape(LANES)
    qk = _relayout(qk, [(int(old[j]), j) for j in range(LANES)], LANES)
    return qk.reshape(d, n_qk), w_qkv[:, n_qk:].astype(BF16)


def _mla_attn(h, attn_norm, wq_a, q_norm, wq_b, wkv_a, kv_norm, wkv_b, tables, *, batch, seq):
    w_a, wq, wk, wvt = _mla_weights(wq_a, wq_b, wkv_a, wkv_b)
    qk, vt = _mla_proj(h, attn_norm, w_a, q_norm, kv_norm, wq, wk, wvt, *tables, seq=seq)
    return _flash(qk, vt, None, batch=batch, seq=seq, n_pairs=MLA_HEADS // 2,
                  q_blk=0, k_blk=MLA_HEADS, sep=True)


def _fox_attn(h, attn_norm, w_qkv, w_f, b_f, *, batch, seq):
    n = FOX_HEADS * FOX_DIM
    d = w_qkv.shape[0]
    wqk = jnp.concatenate([w_qkv[:, :n] * (FOX_DIM ** -0.5 * LOG2E), w_qkv[:, n:2 * n]], axis=1)
    wvt = w_qkv[:, 2 * n:].T
    wf = jnp.concatenate([w_f, jnp.zeros((d, LANES - FOX_HEADS), F32)], axis=1)
    bf = jnp.concatenate([b_f.astype(F32), jnp.zeros((LANES - FOX_HEADS,), F32)]).reshape(1, LANES)
    qk, vt, kd = _fox_proj(h, attn_norm, wqk.astype(BF16), wvt.astype(BF16), wf.astype(BF16), bf,
                           batch=batch, seq=seq)
    n_pairs = FOX_HEADS // 2
    return _flash(qk, vt, kd, batch=batch, seq=seq, n_pairs=n_pairs, q_blk=0, k_blk=n_pairs, sep=False)


def _dil_attn_layer(h, attn_norm, w_qkv, *, batch, seq, tm=512):
    wqk, wv = _dil_weights(w_qkv)
    groups = [_dil_proj(h, attn_norm, wqk, wv, *_dil_tables(seq, d, tm),
                        batch=batch, seq=seq, group=g, d=d, tm=tm)
              for g, (_, d) in enumerate(DIL_PATTERNS)]
    return _dil_attn(groups, batch=batch, seq=seq)


def kernel(x, l0_attn_norm, l0_mla_wq_a, l0_mla_q_norm, l0_mla_wq_b, l0_mla_wkv_a, l0_mla_kv_norm, l0_mla_wkv_b, l0_mla_wo, l0_mlp_norm, l0_w_up, l0_w_down, l1_attn_norm, l1_fox_w_qkv, l1_fox_w_f, l1_fox_b_f, l1_fox_wo, l1_mlp_norm, l1_w_up, l1_w_down, l2_attn_norm, l2_dil_w_qkv, l2_dil_wo, l2_mlp_norm, l2_w_up, l2_w_down, l3_attn_norm, l3_mla_wq_a, l3_mla_q_norm, l3_mla_wq_b, l3_mla_wkv_a, l3_mla_kv_norm, l3_mla_wkv_b, l3_mla_wo, l3_mlp_norm, l3_w_up, l3_w_down, final_norm):
    batch, seq, d = x.shape
    assert seq % DIL_TOKENS == 0 and (batch * seq) % 1024 == 0
    for window, dil in DIL_PATTERNS:
        assert window // dil == DIL_WINDOW_KEYS
    mla_t = _mla_tables(seq)
    kw = dict(batch=batch, seq=seq)
    bf = lambda w: w.astype(BF16)
    h = x.reshape(batch * seq, d)

    o = _mla_attn(h, l0_attn_norm, l0_mla_wq_a, l0_mla_q_norm, l0_mla_wq_b, l0_mla_wkv_a,
                  l0_mla_kv_norm, l0_mla_wkv_b, mla_t, **kw)
    h = _mlp(h, o, bf(l0_mla_wo), l0_mlp_norm, bf(l0_w_up), bf(l0_w_down), final_norm, final_norm=False)

    o = _fox_attn(h, l1_attn_norm, l1_fox_w_qkv, l1_fox_w_f, l1_fox_b_f, **kw)
    h = _mlp(h, o, bf(l1_fox_wo), l1_mlp_norm, bf(l1_w_up), bf(l1_w_down), final_norm, final_norm=False)

    o = _dil_attn_layer(h, l2_attn_norm, l2_dil_w_qkv, **kw)
    h = _mlp(h, o, bf(l2_dil_wo), l2_mlp_norm, bf(l2_w_up), bf(l2_w_down), final_norm, final_norm=False)

    o = _mla_attn(h, l3_attn_norm, l3_mla_wq_a, l3_mla_q_norm, l3_mla_wq_b, l3_mla_wkv_a,
                  l3_mla_kv_norm, l3_mla_wkv_b, mla_t, **kw)
    h = _mlp(h, o, bf(l3_mla_wo), l3_mlp_norm, bf(l3_w_up), bf(l3_w_down), final_norm, final_norm=True)
    return h.reshape(batch, seq, d)
```

```python
import functools
import math

import numpy as np
import jax
import jax.numpy as jnp
from jax import lax
from jax.experimental import pallas as pl
from jax.experimental.pallas import tpu as pltpu

F32 = jnp.float32
BF16 = jnp.bfloat16

LANES = 128
NORM_EPS = 1e-6
NEG_INF = -1e30
ROPE_THETA = 10000.0
LOG2E = math.log2(math.e)

MLA_HEADS = 16
MLA_Q_RANK = 384
MLA_KV_RANK = 256
MLA_NOPE = 64
MLA_ROPE = 32
MLA_V = 64
FOX_HEADS = 16
FOX_DIM = 64
DIL_PATTERNS = ((128, 1), (512, 4), (2048, 16))
DIL_HEADS = 16
DIL_DIM = 64
DIL_WINDOW_KEYS = 128
DIL_TOKENS = 2048
GATE_PARTS = 3

VMEM_LIMIT = 56 * 2**20


def _params(*sem):
    return pltpu.CompilerParams(dimension_semantics=sem, vmem_limit_bytes=VMEM_LIMIT)


def _rms(x, g):
    ms = jnp.mean(x * x, axis=-1, keepdims=True)
    return x * lax.rsqrt(ms + NORM_EPS) * g


def _dot(a, b):
    return jnp.dot(a, b, preferred_element_type=F32)


def _dot_nt(a, b):
    return lax.dot_general(a, b, (((1,), (1,)), ((), ())), preferred_element_type=F32)


def _rope(blk, cos, sin):
    return blk * cos + pltpu.roll(blk, 64, axis=1) * sin


def _colmax(st):
    n, c = st.shape
    return jnp.max(jnp.max(st.reshape(8, n // 8, c), axis=0), axis=0, keepdims=True)


def _split3(x):
    hi = x.astype(BF16)
    r1 = x - hi.astype(F32)
    mid = r1.astype(BF16)
    lo = (r1 - mid.astype(F32)).astype(BF16)
    return hi, mid, lo


def _dil_proj_kernel(x_ref, g_ref, wq_ref, wk_ref, wv_ref, cos_ref, sin_ref, o_ref, xs_ref, xn_ref, *, d, tm):
    rpr = tm // d
    nc = x_ref.shape[1] // LANES
    xn = _rms(x_ref[...], g_ref[...])
    if d == 1:
        xn_ref[...] = xn.astype(BF16)
    else:
        for c in range(nc):
            xs_ref[c] = xn[:, c * LANES:(c + 1) * LANES]
        for c in range(nc):
            for r in range(d):
                xn_ref[r * rpr:(r + 1) * rpr, c * LANES:(c + 1) * LANES] = (
                    xs_ref[c, pl.ds(r, rpr, stride=d), :].astype(BF16))

    nb = o_ref.shape[0] // 4
    lane = lax.broadcasted_iota(jnp.int32, (1, LANES), 1)
    first = jnp.where((lane % 64) < 32, 1.0, 0.0)
    for j in range(3):
        acc = _dot(xn_ref[...], (wq_ref, wk_ref, wv_ref)[j][...])
        for c in range(nb):
            for r in range(d):
                rs = slice(r * rpr, (r + 1) * rpr)
                ls = slice(r * LANES, (r + 1) * LANES)
                blk = acc[rs, c * LANES:(c + 1) * LANES]
                if j < 2:
                    blk = _rope(blk, cos_ref[rs, :], sin_ref[rs, :])
                if j == 0:
                    o_ref[2 * c, 0, :, ls] = (blk * first).astype(o_ref.dtype)
                    o_ref[2 * c + 1, 0, :, ls] = (blk * (1.0 - first)).astype(o_ref.dtype)
                else:
                    o_ref[(j + 1) * nb + c, 0, :, ls] = blk.astype(o_ref.dtype)


def _dil_proj(h, gain, wqk, wv, cos_t, sin_t, *, batch, seq, group, d, tm):
    t, dm = h.shape
    ns = seq // tm
    n_g = len(DIL_PATTERNS)
    n = wv.shape[1] // n_g
    nb = 4 * n // LANES
    return pl.pallas_call(
        functools.partial(_dil_proj_kernel, d=d, tm=tm),
        grid=(batch, ns),
        in_specs=[
            pl.BlockSpec((tm, dm), lambda b, i: (b * ns + i, 0)),
            pl.BlockSpec((1, dm), lambda b, i: (0, 0)),
            pl.BlockSpec((dm, n), lambda b, i: (0, group)),
            pl.BlockSpec((dm, n), lambda b, i: (0, n_g + group)),
            pl.BlockSpec((dm, n), lambda b, i: (0, group)),
            pl.BlockSpec((tm, LANES), lambda b, i: (i, 0)),
            pl.BlockSpec((tm, LANES), lambda b, i: (i, 0)),
        ],
        out_specs=pl.BlockSpec((nb, 1, tm // d, d * LANES), lambda b, i: (0, b, i, 0)),
        out_shape=jax.ShapeDtypeStruct((nb, batch, seq // d, d * LANES), BF16),
        scratch_shapes=[pltpu.VMEM((dm // LANES, tm, LANES), F32), pltpu.VMEM((tm, dm), BF16)],
        compiler_params=_params("parallel", "parallel"),
        name="dil_proj_d%d" % d,
    )(h, gain.reshape(1, dm), wqk, wqk, wv, cos_t, sin_t)


def _mla_proj_kernel(x_ref, g_ref, wa_ref, qn_ref, kvn_ref, wq_ref, wk_ref, wvt_ref, cos_ref, sin_ref,
                     qk_ref, vt_ref):
    cos, sin = cos_ref[...], sin_ref[...]
    xn = _rms(x_ref[...], g_ref[...]).astype(BF16)
    y = _dot(xn, wa_ref[...])
    cq = _rms(y[:, :MLA_Q_RANK], qn_ref[...]).astype(BF16)
    ckv = _rms(y[:, MLA_Q_RANK:MLA_Q_RANK + MLA_KV_RANK], kvn_ref[...]).astype(BF16)
    kpe = _rope(y[:, MLA_Q_RANK + MLA_KV_RANK:], cos, sin)
    q = _dot(cq, wq_ref[...])
    for c in range(MLA_HEADS):
        qk_ref[c] = _rope(q[:, c * LANES:(c + 1) * LANES], cos, sin).astype(BF16)
    k = _dot(ckv, wk_ref[...])
    for c in range(MLA_HEADS):
        qk_ref[MLA_HEADS + c] = (k[:, c * LANES:(c + 1) * LANES] + kpe).astype(BF16)
    vt = _dot_nt(wvt_ref[...], ckv)
    for c in range(MLA_HEADS // 2):
        vt_ref[c] = vt[c * LANES:(c + 1) * LANES, :].astype(BF16)


def _mla_proj(h, gain, w_a, q_norm, kv_norm, wq, wk, wvt, cos_t, sin_t, *, seq, tm=512):
    t, d = h.shape
    ns = seq // tm
    fixed = lambda shape: pl.BlockSpec(shape, lambda i: (0, 0))
    tab = pl.BlockSpec((tm, LANES), lambda i: (i % ns, 0))
    return pl.pallas_call(
        _mla_proj_kernel,
        grid=(t // tm,),
        in_specs=[
            pl.BlockSpec((tm, d), lambda i: (i, 0)),
            fixed((1, d)),
            fixed(w_a.shape),
            fixed((1, MLA_Q_RANK)),
            fixed((1, MLA_KV_RANK)),
            fixed(wq.shape),
            fixed(wk.shape),
            fixed(wvt.shape),
            tab,
            tab,
        ],
        out_specs=[
            pl.BlockSpec((2 * MLA_HEADS, tm, LANES), lambda i: (0, i, 0)),
            pl.BlockSpec((MLA_HEADS // 2, LANES, tm), lambda i: (0, 0, i)),
        ],
        out_shape=[
            jax.ShapeDtypeStruct((2 * MLA_HEADS, t, LANES), BF16),
            jax.ShapeDtypeStruct((MLA_HEADS // 2, LANES, t), BF16),
        ],
        compiler_params=_params("parallel"),
        name="mla_proj",
    )(h, gain.reshape(1, d), w_a, q_norm.reshape(1, -1), kv_norm.reshape(1, -1), wq, wk, wvt, cos_t, sin_t)


def _fox_proj_kernel(x_ref, g_ref, wqk_ref, wvt_ref, wf_ref, bf_ref, tri_ref, scat_ref,
                     qk_ref, vt_ref, kd_ref, carry_ref, *, tm):
    @pl.when(pl.program_id(1) == 0)
    def _():
        carry_ref[...] = jnp.zeros_like(carry_ref)

    xn = _rms(x_ref[...], g_ref[...]).astype(BF16)
    qk = _dot(xn, wqk_ref[...])
    for c in range(qk_ref.shape[0]):
        qk_ref[c] = qk[:, c * LANES:(c + 1) * LANES].astype(BF16)
    vt = _dot_nt(wvt_ref[...], xn)
    for c in range(vt_ref.shape[0]):
        vt_ref[c] = vt[c * LANES:(c + 1) * LANES, :].astype(BF16)

    heads = jnp.where(lax.broadcasted_iota(jnp.int32, (1, LANES), 1) < FOX_HEADS, 1.0, 0.0)

    def pack3(x):
        hi, mid, lo = (p.astype(F32) for p in _split3(x * heads))
        return (hi + pltpu.roll(mid, FOX_HEADS, axis=1) + pltpu.roll(lo, 2 * FOX_HEADS, axis=1)).astype(BF16)

    z = _dot(xn, wf_ref[...]) + bf_ref[...]
    logf = jnp.minimum(z, 0.0) - jnp.log(1.0 + jnp.exp(-jnp.abs(z)))
    c3 = _dot(tri_ref[...], pack3(logf))
    cum = (c3 + pltpu.roll(c3, LANES - FOX_HEADS, axis=1)
           + pltpu.roll(c3, LANES - 2 * FOX_HEADS, axis=1)) * heads + carry_ref[...]
    carry_ref[...] = cum[tm - 1:tm, :]
    kd = _dot(pack3(cum * (-LOG2E)), scat_ref[...])
    for c in range(kd_ref.shape[0]):
        kd_ref[c] = kd[:, c * LANES:(c + 1) * LANES].astype(BF16)


def _fox_proj(h, gain, wqk, wvt, wf, bf, *, batch, seq, tm=512):
    t, d = h.shape
    ns = seq // tm
    n_pairs = FOX_HEADS // 2
    tri = jnp.asarray(np.tril(np.ones((tm, tm), np.float32)), BF16)
    scat = np.zeros((LANES, n_pairs * LANES), np.float32)
    for p in range(GATE_PARTS):
        for hd in range(FOX_HEADS):
            scat[p * FOX_HEADS + hd, (hd // 2) * LANES + GATE_PARTS * (hd % 2) + p] = 1.0
    fixed2 = lambda b, i: (0, 0)
    rows = lambda b, i: (0, b * ns + i, 0)
    return pl.pallas_call(
        functools.partial(_fox_proj_kernel, tm=tm),
        grid=(batch, ns),
        in_specs=[
            pl.BlockSpec((tm, d), lambda b, i: (b * ns + i, 0)),
            pl.BlockSpec((1, d), fixed2),
            pl.BlockSpec(wqk.shape, fixed2),
            pl.BlockSpec(wvt.shape, fixed2),
            pl.BlockSpec(wf.shape, fixed2),
            pl.BlockSpec((1, LANES), fixed2),
            pl.BlockSpec((tm, tm), fixed2),
            pl.BlockSpec(scat.shape, fixed2),
        ],
        out_specs=[
            pl.BlockSpec((2 * n_pairs, tm, LANES), rows),
            pl.BlockSpec((n_pairs, LANES, tm), lambda b, i: (0, 0, b * ns + i)),
            pl.BlockSpec((n_pairs, tm, LANES), rows),
        ],
        out_shape=[
            jax.ShapeDtypeStruct((2 * n_pairs, t, LANES), BF16),
            jax.ShapeDtypeStruct((n_pairs, LANES, t), BF16),
            jax.ShapeDtypeStruct((n_pairs, t, LANES), BF16),
        ],
        scratch_shapes=[pltpu.VMEM((1, LANES), F32)],
        compiler_params=_params("arbitrary", "arbitrary"),
        name="fox_proj",
    )(h, gain.reshape(1, d), wqk, wvt, wf, bf, tri, jnp.asarray(scat, BF16))


def _flash_kernel(*refs, sep, has_bias, tq, unroll):
    n_in = 4 if has_bias else 3
    q_ref, k_ref, vt_ref = refs[:3]
    kd_ref = refs[3] if has_bias else None
    o_ref, qs_ref, vta_ref, acc_ref, s_ref, mx_ref, m_ref = refs[n_in:]
    half = LANES // 2
    nq = acc_ref.shape[0]
    seq = nq * tq

    ones = jnp.ones((vta_ref.shape[1] - half, seq), BF16)
    vta_ref[0] = jnp.concatenate([vt_ref[0, 0:half, :], ones], axis=0)
    vta_ref[1] = jnp.concatenate([vt_ref[0, half:LANES, :], ones], axis=0)
    row = lax.broadcasted_iota(jnp.int32, (LANES, tq), 0)
    for hh in range(2):
        for t in range(nq):
            cols = slice(t * tq, (t + 1) * tq)
            qt = q_ref[hh if sep else 0, cols, :].astype(F32).T
            if not sep:
                qt = jnp.where((row < half) if hh == 0 else (row >= half), qt, 0.0)
            qs_ref[hh, 0:LANES, cols] = qt.astype(BF16)
        if has_bias:
            r2 = lax.broadcasted_iota(jnp.int32, (LANES, seq), 0)
            sel = jnp.logical_and(r2 >= GATE_PARTS * hh, r2 < GATE_PARTS * (hh + 1))
            qs_ref[hh, LANES:2 * LANES, :] = jnp.where(sel, 1.0, 0.0).astype(BF16)

    def queries(hh, start, n):
        return qs_ref[hh, :, pl.ds(start, n)]

    def keys(hh, start, n):
        k_blk = k_ref[hh if sep else 0, pl.ds(start, n), :]
        if has_bias:
            k_blk = jnp.concatenate([k_blk, kd_ref[0, pl.ds(start, n), :]], axis=1)
        return k_blk

    def scores(i, kb, slot):
        qstart, kstart = pl.multiple_of(i * tq, tq), pl.multiple_of(kb * tq, tq)
        for hh in range(2):
            st = _dot(keys(hh, kstart, tq), queries(hh, qstart, tq))
            s_ref[slot, hh] = st
            mx_ref[slot, hh] = _colmax(st)

    def consume(i, kb, slot):
        kstart = pl.multiple_of(kb * tq, tq)
        for hh in range(2):
            m_old = m_ref[i, hh]
            m_new = jnp.maximum(m_old, mx_ref[slot, hh])
            m_ref[i, hh] = m_new
            pt = jnp.exp2(s_ref[slot, hh] - m_new).astype(BF16)
            pv = _dot(vta_ref[hh, :, pl.ds(kstart, tq)], pt)
            acc_ref[i, hh] = jnp.exp2(m_old - m_new) * acc_ref[i, hh] + pv

    hq = tq // 2
    halves = ((0, hq), (hq, tq))

    def scores_diag(i, slot):
        for hh in range(2):
            for q0, nk in halves:
                st = _dot(keys(hh, i * tq, nk), queries(hh, i * tq + q0, hq))
                key = lax.broadcasted_iota(jnp.int32, (nk, hq), 0)
                qry = lax.broadcasted_iota(jnp.int32, (nk, hq), 1) + q0
                st = jnp.where(key <= qry, st, NEG_INF)
                s_ref[slot, hh, 0:nk, q0:q0 + hq] = st
                mx_ref[slot, hh, :, q0:q0 + hq] = _colmax(st)

    def consume_diag(i, slot):
        for hh in range(2):
            for q0, nk in halves:
                m_new = mx_ref[slot, hh, :, q0:q0 + hq]
                m_ref[i, hh, :, q0:q0 + hq] = m_new
                pt = jnp.exp2(s_ref[slot, hh, 0:nk, q0:q0 + hq] - m_new).astype(BF16)
                acc_ref[i, hh, :, q0:q0 + hq] = _dot(vta_ref[hh, :, i * tq:i * tq + nk], pt)

    scores_diag(0, 0)
    for i in range(nq):
        if i + 1 < nq:
            scores_diag(i + 1, (i + 1) % 2)
        elif nq > 1:
            scores(1, 0, (i + 1) % 2)
        consume_diag(i, i % 2)

    def advance(i, kb):
        last = kb + 1 == i
        return jnp.where(last, i + 1, i), jnp.where(last, 0, kb + 1)

    def trip(_, carry):
        i, kb = carry
        for u in range(unroll):
            ni, nkb = advance(i, kb)
            scores(jnp.minimum(ni, nq - 1), nkb, (nq + u + 1) % 2)
            consume(i, kb, (nq + u) % 2)
            i, kb = ni, nkb
        return i, kb

    n_items = nq * (nq - 1) // 2
    assert unroll % 2 == 0 and n_items % unroll == 0
    lax.fori_loop(0, n_items // unroll, trip, (jnp.int32(1), jnp.int32(0)))

    for i in range(nq):
        acc_a, acc_b = acc_ref[i, 0], acc_ref[i, 1]
        o_t = jnp.concatenate([acc_a[0:half] / acc_a[half:half + 1], acc_b[0:half] / acc_b[half:half + 1]], axis=0)
        o_ref[0, i * tq:(i + 1) * tq, :] = o_t.T.astype(o_ref.dtype)


def _flash(qk, vt, kd, *, batch, seq, n_pairs, q_blk, k_blk, sep, tq=512):
    t = qk.shape[1]
    nq = seq // tq
    w = 2 if sep else 1
    kc = 2 * LANES if kd is not None else LANES
    n_items = nq * (nq - 1) // 2
    unroll = max(u for u in (14, 8, 6, 4, 2) if n_items % u == 0)
    va_rows = LANES // 2 + 16
    in_specs = [
        pl.BlockSpec((w, seq, LANES), lambda b, hp: (q_blk // w + hp, b, 0)),
        pl.BlockSpec((w, seq, LANES), lambda b, hp: (k_blk // w + hp, b, 0)),
        pl.BlockSpec((1, LANES, seq), lambda b, hp: (hp, 0, b)),
    ]
    args = [qk, qk, vt]
    if kd is not None:
        in_specs.append(pl.BlockSpec((1, seq, LANES), lambda b, hp: (hp, b, 0)))
        args.append(kd)
    return pl.pallas_call(
        functools.partial(_flash_kernel, sep=sep, has_bias=kd is not None, tq=tq, unroll=unroll),
        grid=(batch, n_pairs),
        in_specs=in_specs,
        out_specs=pl.BlockSpec((1, seq, LANES), lambda b, hp: (hp, b, 0)),
        out_shape=jax.ShapeDtypeStruct((n_pairs, t, LANES), BF16),
        scratch_shapes=[
            pltpu.VMEM((2, kc, seq), BF16),
            pltpu.VMEM((2, va_rows, seq), BF16),
            pltpu.VMEM((nq, 2, va_rows, tq), F32),
            pltpu.VMEM((2, 2, tq, tq), F32),
            pltpu.VMEM((2, 2, 1, tq), F32),
            pltpu.VMEM((nq, 2, 1, tq), F32),
        ],
        compiler_params=_params("parallel", "parallel"),
        name="flash_sep" if sep else "flash_shared",
    )(*args)


def _dil_attn_kernel(*refs):
    n_in = 5 * len(DIL_PATTERNS)
    o_ref, og_ref, lg_ref, band_ref = refs[n_in:]
    i = pl.program_id(2)
    w = DIL_WINDOW_KEYS
    v_first = lax.broadcasted_iota(jnp.int32, (w, LANES), 1) < 64
    qi = lax.broadcasted_iota(jnp.int32, (2 * w, 2 * w), 0) % w
    kk = lax.broadcasted_iota(jnp.int32, (2 * w, 2 * w), 1)
    band = jnp.logical_and(kk >= qi, kk <= qi + w)
    band_ref[0] = jnp.where(band, 0.0, NEG_INF)
    band_ref[1] = jnp.where(jnp.logical_and(band, kk >= jnp.where(i > 0, 0, w)), 0.0, NEG_INF)
    ones = jnp.ones((2 * w, LANES), BF16)
    for g, (_, d) in enumerate(DIL_PATTERNS):
        q_ref, k_ref, kp_ref, v_ref, vp_ref = refs[5 * g:5 * g + 5]
        for r in range(d):
            ls = slice(r * LANES, (r + 1) * LANES)
            for sub in range(DIL_TOKENS // d // w):
                rows = slice(sub * w, (sub + 1) * w)
                q_u = jnp.concatenate([q_ref[0, 0, rows, ls], q_ref[1, 0, rows, ls]], axis=0)
                if sub == 0:
                    k_u = jnp.concatenate([kp_ref[0, 0, :, ls], k_ref[0, 0, 0:w, ls]], axis=0)
                    v_u = jnp.concatenate([vp_ref[0, 0, :, ls], v_ref[0, 0, 0:w, ls]], axis=0)
                else:
                    k_u = k_ref[0, 0, (sub - 1) * w:(sub + 1) * w, ls]
                    v_u = v_ref[0, 0, (sub - 1) * w:(sub + 1) * w, ls]
                s = _dot_nt(q_u, k_u) + band_ref[1 if sub == 0 else 0]
                m = jnp.max(s, axis=-1, keepdims=True)
                p = jnp.exp2(s - m).astype(BF16)
                ov = _dot(p, jnp.concatenate([v_u, ones], axis=1))
                num = jnp.where(v_first, ov[0:w, 0:LANES], ov[w:2 * w, 0:LANES])
                den = jnp.where(v_first, ov[0:w, LANES:2 * LANES], ov[w:2 * w, LANES:2 * LANES])
                top = jnp.where(v_first, jnp.broadcast_to(m[0:w], (w, LANES)),
                                jnp.broadcast_to(m[w:2 * w], (w, LANES)))
                tok = pl.ds(sub * w * d + r, w, stride=d) if d > 1 else pl.ds(sub * w, w)
                og_ref[g, tok, :] = num / den
                lg_ref[g, tok, :] = top + jnp.log(den) * LOG2E
    lse = [lg_ref[g] for g in range(len(DIL_PATTERNS))]
    top = functools.reduce(jnp.maximum, lse)
    e = [jnp.exp2(x - top) for x in lse]
    num = sum(e[g] * og_ref[g] for g in range(len(DIL_PATTERNS)))
    o_ref[0] = (num / sum(e)).astype(o_ref.dtype)


def _dil_attn(groups, *, batch, seq):
    n_pairs = DIL_HEADS // 2
    n_tok_blk = seq // DIL_TOKENS
    in_specs, args = [], []
    for arr, (_, d) in zip(groups, DIL_PATTERNS):
        rows = DIL_TOKENS // d
        sub_per_blk = rows // DIL_WINDOW_KEYS

        def cur(base, rows=rows, d=d):
            return pl.BlockSpec((1, 1, rows, d * LANES), lambda b, hp, i: (base + hp, b, i, 0))

        def prev(base, spb=sub_per_blk, d=d):
            return pl.BlockSpec((1, 1, DIL_WINDOW_KEYS, d * LANES),
                                lambda b, hp, i: (base + hp, b, jnp.maximum(i * spb - 1, 0), 0))

        q_spec = pl.BlockSpec((2, 1, rows, d * LANES), lambda b, hp, i: (hp, b, i, 0))
        in_specs += [q_spec, cur(2 * n_pairs), prev(2 * n_pairs), cur(3 * n_pairs), prev(3 * n_pairs)]
        args += [arr] * 5
    n_g = len(DIL_PATTERNS)
    w2 = 2 * DIL_WINDOW_KEYS
    return pl.pallas_call(
        _dil_attn_kernel,
        grid=(batch, n_pairs, n_tok_blk),
        in_specs=in_specs,
        out_specs=pl.BlockSpec((1, DIL_TOKENS, LANES), lambda b, hp, i: (hp, b * n_tok_blk + i, 0)),
        out_shape=jax.ShapeDtypeStruct((n_pairs, batch * seq, LANES), BF16),
        scratch_shapes=[pltpu.VMEM((n_g, DIL_TOKENS, LANES), F32), pltpu.VMEM((n_g, DIL_TOKENS, LANES), F32),
                        pltpu.VMEM((2, w2, w2), F32)],
        compiler_params=_params("parallel", "parallel", "arbitrary"),
        name="dil_attn",
    )(*args)


def _mlp_kernel(h_ref, o_ref, wo_ref, g_ref, wu_ref, wd_ref, fg_ref, out_ref, xn_ref, *, final_norm, tf):
    a = jnp.concatenate([o_ref[k] for k in range(o_ref.shape[0])], axis=-1)
    h1 = h_ref[...] + _dot(a, wo_ref[...])
    xn_ref[...] = _rms(h1, g_ref[...]).astype(BF16)
    y = h1
    for c in range(wu_ref.shape[1] // tf):
        u = jnp.maximum(_dot(xn_ref[...], wu_ref[:, c * tf:(c + 1) * tf]), 0.0)
        y = y + _dot((u * u).astype(BF16), wd_ref[c * tf:(c + 1) * tf, :])
    out_ref[...] = _rms(y, fg_ref[...]) if final_norm else y


def _mlp(h, o, wo, gain, w_up, w_down, final_gain, *, final_norm, tm=512, tf=1024):
    t, d = h.shape
    fixed = lambda shape: pl.BlockSpec(shape, lambda i: (0, 0), pipeline_mode=pl.Buffered(1))
    return pl.pallas_call(
        functools.partial(_mlp_kernel, final_norm=final_norm, tf=tf),
        grid=(t // tm,),
        in_specs=[
            pl.BlockSpec((tm, d), lambda i: (i, 0)),
            pl.BlockSpec((o.shape[0], tm, LANES), lambda i: (0, i, 0)),
            fixed(wo.shape),
            fixed((1, d)),
            fixed(w_up.shape),
            fixed(w_down.shape),
            fixed((1, d)),
        ],
        out_specs=pl.BlockSpec((tm, d), lambda i: (i, 0)),
        out_shape=jax.ShapeDtypeStruct((t, d), F32),
        scratch_shapes=[pltpu.VMEM((tm, d), BF16)],
        compiler_params=_params("parallel"),
        name="mlp_final" if final_norm else "mlp",
    )(h, o, wo, gain.reshape(1, d), w_up, w_down, final_gain.reshape(1, d))


def _rope_halves(seq, dim):
    inv = (1.0 / (np.float32(ROPE_THETA) ** (np.arange(0, dim, 2, dtype=np.float32) / np.float32(dim))))
    ang = np.arange(seq, dtype=np.float32)[:, None] * inv.astype(np.float32)[None, :]
    return np.cos(ang).astype(np.float32), np.sin(ang).astype(np.float32)


def _mla_tables(seq):
    cos, sin = _rope_halves(seq, MLA_ROPE)
    one = lambda n: np.ones((seq, n), np.float32)
    zero = lambda n: np.zeros((seq, n), np.float32)
    cos_t = np.concatenate([one(32), cos, one(16), one(32), cos, one(16)], axis=1)
    sin_t = np.concatenate([zero(32), -sin, zero(16), zero(32), sin, zero(16)], axis=1)
    return jnp.asarray(cos_t), jnp.asarray(sin_t)


def _dil_tables(seq, d, tm):
    cos, sin = _rope_halves(seq, DIL_DIM)
    order = lambda x: x.reshape(seq // tm, tm // d, d, LANES).transpose(0, 2, 1, 3).reshape(seq, LANES)
    return (jnp.asarray(order(np.concatenate([cos] * 4, axis=1))),
            jnp.asarray(order(np.concatenate([-sin, -sin, sin, sin], axis=1))))


def _relayout(w3, moves, n_dst):
    p = np.zeros((w3.shape[-1], n_dst), np.float32)
    for src, dst in moves:
        p[src, dst] = 1.0
    return jnp.einsum('...c,cl->...l', w3, jnp.asarray(p), preferred_element_type=F32).astype(BF16)


def _mla_weights(wq_a, wq_b, wkv_a, wkv_b):
    lanes = lambda c: c if c < 32 else 32 + c if c < 64 else c - 32 if c < 80 else c + 16
    kpe_moves = [(c, c) for c in range(MLA_KV_RANK)] + [
        (MLA_KV_RANK + c, MLA_KV_RANK + lanes(MLA_NOPE + c)) for c in range(MLA_ROPE)]
    w_a = jnp.concatenate([wq_a.astype(BF16), _relayout(wkv_a, kpe_moves, MLA_KV_RANK + LANES)], axis=1)
    scale = (MLA_NOPE + MLA_ROPE) ** -0.5 * LOG2E
    q3 = wq_b.reshape(MLA_Q_RANK, MLA_HEADS, MLA_NOPE + MLA_ROPE) * scale
    wq = _relayout(q3, [(c, lanes(c)) for c in range(MLA_NOPE + MLA_ROPE)], LANES)
    kv3 = wkv_b.reshape(MLA_KV_RANK, MLA_HEADS, MLA_NOPE + MLA_V)
    wk = _relayout(kv3, [(c, lanes(c)) for c in range(MLA_NOPE)], LANES)
    wvt = kv3[..., MLA_NOPE:].reshape(MLA_KV_RANK, -1).T.astype(BF16)
    return w_a, wq.reshape(MLA_Q_RANK, -1), wk.reshape(MLA_KV_RANK, -1), wvt


def _dil_weights(w_qkv):
    d = w_qkv.shape[0]
    n_qk = 2 * len(DIL_PATTERNS) * DIL_HEADS * DIL_DIM
    qk = w_qkv[:, :n_qk].reshape(d, 2, n_qk // (2 * LANES), LANES)
    qk = qk * jnp.array([DIL_DIM ** -0.5 * LOG2E, 1.0], F32).reshape(1, 2, 1, 1)
    old = np.arange(LANES).reshape(2, 2, 32).transpose(1, 0, 2).reshape(LANES)
    qk = _relayout(qk, [(int(old[j]), j) for j in range(LANES)], LANES)
    return qk.reshape(d, n_qk), w_qkv[:, n_qk:].astype(BF16)


def _mla_attn(h, attn_norm, wq_a, q_norm, wq_b, wkv_a, kv_norm, wkv_b, tables, *, batch, seq):
    w_a, wq, wk, wvt = _mla_weights(wq_a, wq_b, wkv_a, wkv_b)
    qk, vt = _mla_proj(h, attn_norm, w_a, q_norm, kv_norm, wq, wk, wvt, *tables, seq=seq)
    return _flash(qk, vt, None, batch=batch, seq=seq, n_pairs=MLA_HEADS // 2,
                  q_blk=0, k_blk=MLA_HEADS, sep=True)


def _fox_attn(h, attn_norm, w_qkv, w_f, b_f, *, batch, seq):
    n = FOX_HEADS * FOX_DIM
    d = w_qkv.shape[0]
    wqk = jnp.concatenate([w_qkv[:, :n] * (FOX_DIM ** -0.5 * LOG2E), w_qkv[:, n:2 * n]], axis=1)
    wvt = w_qkv[:, 2 * n:].T
    wf = jnp.concatenate([w_f, jnp.zeros((d, LANES - FOX_HEADS), F32)], axis=1)
    bf = jnp.concatenate([b_f.astype(F32), jnp.zeros((LANES - FOX_HEADS,), F32)]).reshape(1, LANES)
    qk, vt, kd = _fox_proj(h, attn_norm, wqk.astype(BF16), wvt.astype(BF16), wf.astype(BF16), bf,
                           batch=batch, seq=seq)
    n_pairs = FOX_HEADS // 2
    return _flash(qk, vt, kd, batch=batch, seq=seq, n_pairs=n_pairs, q_blk=0, k_blk=n_pairs, sep=False)


def _dil_attn_layer(h, attn_norm, w_qkv, *, batch, seq, tm=512):
    wqk, wv = _dil_weights(w_qkv)
    groups = [_dil_proj(h, attn_norm, wqk, wv, *_dil_tables(seq, d, tm),
                        batch=batch, seq=seq, group=g, d=d, tm=tm)
              for g, (_, d) in enumerate(DIL_PATTERNS)]
    return _dil_attn(groups, batch=batch, seq=seq)


def kernel(x, l0_attn_norm, l0_mla_wq_a, l0_mla_q_norm, l0_mla_wq_b, l0_mla_wkv_a, l0_mla_kv_norm, l0_mla_wkv_b, l0_mla_wo, l0_mlp_norm, l0_w_up, l0_w_down, l1_attn_norm, l1_fox_w_qkv, l1_fox_w_f, l1_fox_b_f, l1_fox_wo, l1_mlp_norm, l1_w_up, l1_w_down, l2_attn_norm, l2_dil_w_qkv, l2_dil_wo, l2_mlp_norm, l2_w_up, l2_w_down, l3_attn_norm, l3_mla_wq_a, l3_mla_q_norm, l3_mla_wq_b, l3_mla_wkv_a, l3_mla_kv_norm, l3_mla_wkv_b, l3_mla_wo, l3_mlp_norm, l3_w_up, l3_w_down, final_norm):
    batch, seq, d = x.shape
    assert seq % DIL_TOKENS == 0 and (batch * seq) % 1024 == 0
    for window, dil in DIL_PATTERNS:
        assert window // dil == DIL_WINDOW_KEYS
    mla_t = _mla_tables(seq)
    kw = dict(batch=batch, seq=seq)
    bf = lambda w: w.astype(BF16)
    h = x.reshape(batch * seq, d)

    o = _mla_attn(h, l0_attn_norm, l0_mla_wq_a, l0_mla_q_norm, l0_mla_wq_b, l0_mla_wkv_a,
                  l0_mla_kv_norm, l0_mla_wkv_b, mla_t, **kw)
    h = _mlp(h, o, bf(l0_mla_wo), l0_mlp_norm, bf(l0_w_up), bf(l0_w_down), final_norm, final_norm=False)

    o = _fox_attn(h, l1_attn_norm, l1_fox_w_qkv, l1_fox_w_f, l1_fox_b_f, **kw)
    h = _mlp(h, o, bf(l1_fox_wo), l1_mlp_norm, bf(l1_w_up), bf(l1_w_down), final_norm, final_norm=False)

    o = _dil_attn_layer(h, l2_attn_norm, l2_dil_w_qkv, **kw)
    h = _mlp(h, o, bf(l2_dil_wo), l2_mlp_norm, bf(l2_w_up), bf(l2_w_down), final_norm, final_norm=False)

    o = _mla_attn(h, l3_attn_norm, l3_mla_wq_a, l3_mla_q_norm, l3_mla_wq_b, l3_mla_wkv_a,
                  l3_mla_kv_norm, l3_mla_wkv_b, mla_t, **kw)
    h = _mlp(h, o, bf(l3_mla_wo), l3_mlp_norm, bf(l3_w_up), bf(l3_w_down), final_norm, final_norm=True)
    return h.reshape(batch, seq, d)
```

```python
import functools
import math

import numpy as np
import jax
import jax.numpy as jnp
from jax import lax
from jax.experimental import pallas as pl
from jax.experimental.pallas import tpu as pltpu

F32 = jnp.float32
BF16 = jnp.bfloat16

LANES = 128
NORM_EPS = 1e-6
NEG_INF = -1e30
ROPE_THETA = 10000.0
LOG2E = math.log2(math.e)

MLA_HEADS = 16
MLA_Q_RANK = 384
MLA_KV_RANK = 256
MLA_NOPE = 64
MLA_ROPE = 32
MLA_V = 64
FOX_HEADS = 16
FOX_DIM = 64
DIL_PATTERNS = ((128, 1), (512, 4), (2048, 16))
DIL_HEADS = 16
DIL_DIM = 64
DIL_WINDOW_KEYS = 128
DIL_TOKENS = 2048
GATE_PARTS = 3

VMEM_LIMIT = 56 * 2**20


def _params(*sem):
    return pltpu.CompilerParams(dimension_semantics=sem, vmem_limit_bytes=VMEM_LIMIT)


def _rms(x, g):
    ms = jnp.mean(x * x, axis=-1, keepdims=True)
    return x * lax.rsqrt(ms + NORM_EPS) * g


def _dot(a, b):
    return jnp.dot(a, b, preferred_element_type=F32)


def _dot_nt(a, b):
    return lax.dot_general(a, b, (((1,), (1,)), ((), ())), preferred_element_type=F32)


def _rope(blk, cos, sin):
    return blk * cos + pltpu.roll(blk, 64, axis=1) * sin


def _colmax(st):
    n, c = st.shape
    return jnp.max(jnp.max(st.reshape(8, n // 8, c), axis=0), axis=0, keepdims=True)


def _split3(x):
    hi = x.astype(BF16)
    r1 = x - hi.astype(F32)
    mid = r1.astype(BF16)
    lo = (r1 - mid.astype(F32)).astype(BF16)
    return hi, mid, lo


def _dil_proj_kernel(*refs, tm):
    n_g = len(DIL_PATTERNS)
    x_ref, g_ref, wqk_ref, wv_ref = refs[:4]
    tabs = refs[4:4 + 2 * n_g]
    outs = refs[4 + 2 * n_g:4 + 3 * n_g]
    xs_ref, xn_ref = refs[4 + 3 * n_g:]
    nc = x_ref.shape[1] // LANES
    n = wv_ref.shape[1] // n_g
    nb = n // LANES
    xn = _rms(x_ref[...], g_ref[...])
    for c in range(nc):
        xs_ref[c] = xn[:, c * LANES:(c + 1) * LANES]
    lane = lax.broadcasted_iota(jnp.int32, (1, LANES), 1)
    first = jnp.where((lane % 64) < 32, 1.0, 0.0)
    for g, (_, d) in enumerate(DIL_PATTERNS):
        rpr = tm // d
        cos_ref, sin_ref, o_ref = tabs[2 * g], tabs[2 * g + 1], outs[g]
        if d == 1:
            xn_ref[g] = xn.astype(BF16)
        else:
            for c in range(nc):
                for r in range(d):
                    xn_ref[g, r * rpr:(r + 1) * rpr, c * LANES:(c + 1) * LANES] = (
                        xs_ref[c, pl.ds(r, rpr, stride=d), :].astype(BF16))
        weights = (wqk_ref[:, g * n:(g + 1) * n], wqk_ref[:, (n_g + g) * n:(n_g + g + 1) * n],
                   wv_ref[:, g * n:(g + 1) * n])
        for j in range(3):
            acc = _dot(xn_ref[g], weights[j])
            for c in range(nb):
                for r in range(d):
                    rs = slice(r * rpr, (r + 1) * rpr)
                    ls = slice(r * LANES, (r + 1) * LANES)
                    blk = acc[rs, c * LANES:(c + 1) * LANES]
                    if j < 2:
                        blk = _rope(blk, cos_ref[rs, :], sin_ref[rs, :])
                    if j == 0:
                        o_ref[2 * c, 0, :, ls] = (blk * first).astype(o_ref.dtype)
                        o_ref[2 * c + 1, 0, :, ls] = (blk * (1.0 - first)).astype(o_ref.dtype)
                    else:
                        o_ref[(j + 1) * nb + c, 0, :, ls] = blk.astype(o_ref.dtype)


def _dil_proj(h, gain, wqk, wv, tables, *, batch, seq, tm):
    t, dm = h.shape
    ns = seq // tm
    n_g = len(DIL_PATTERNS)
    nb = 4 * wv.shape[1] // (n_g * LANES)
    fixed = lambda shape: pl.BlockSpec(shape, lambda b, i: (0, 0), pipeline_mode=pl.Buffered(1))
    tab = pl.BlockSpec((tm, LANES), lambda b, i: (i, 0))
    return pl.pallas_call(
        functools.partial(_dil_proj_kernel, tm=tm),
        grid=(batch, ns),
        in_specs=[pl.BlockSpec((tm, dm), lambda b, i: (b * ns + i, 0)), fixed((1, dm)),
                  fixed(wqk.shape), fixed(wv.shape)] + [tab] * (2 * n_g),
        out_specs=[pl.BlockSpec((nb, 1, tm // d, d * LANES), lambda b, i: (0, b, i, 0))
                   for _, d in DIL_PATTERNS],
        out_shape=[jax.ShapeDtypeStruct((nb, batch, seq // d, d * LANES), BF16) for _, d in DIL_PATTERNS],
        scratch_shapes=[pltpu.VMEM((dm // LANES, tm, LANES), F32), pltpu.VMEM((n_g, tm, dm), BF16)],
        compiler_params=_params("parallel", "parallel"),
        name="dil_proj",
    )(h, gain.reshape(1, dm), wqk, wv, *[x for pair in tables for x in pair])


def _mla_proj_kernel(x_ref, g_ref, wa_ref, qn_ref, kvn_ref, wq_ref, wk_ref, wvt_ref, cos_ref, sin_ref,
                     qk_ref, vt_ref):
    cos, sin = cos_ref[...], sin_ref[...]
    xn = _rms(x_ref[...], g_ref[...]).astype(BF16)
    y = _dot(xn, wa_ref[...])
    cq = _rms(y[:, :MLA_Q_RANK], qn_ref[...]).astype(BF16)
    ckv = _rms(y[:, MLA_Q_RANK:MLA_Q_RANK + MLA_KV_RANK], kvn_ref[...]).astype(BF16)
    kpe = _rope(y[:, MLA_Q_RANK + MLA_KV_RANK:], cos, sin)
    q = _dot(cq, wq_ref[...])
    for c in range(MLA_HEADS):
        qk_ref[c] = _rope(q[:, c * LANES:(c + 1) * LANES], cos, sin).astype(BF16)
    k = _dot(ckv, wk_ref[...])
    for c in range(MLA_HEADS):
        qk_ref[MLA_HEADS + c] = (k[:, c * LANES:(c + 1) * LANES] + kpe).astype(BF16)
    vt = _dot_nt(wvt_ref[...], ckv)
    for c in range(MLA_HEADS // 2):
        vt_ref[c] = vt[c * LANES:(c + 1) * LANES, :].astype(BF16)


def _mla_proj(h, gain, w_a, q_norm, kv_norm, wq, wk, wvt, cos_t, sin_t, *, seq, tm=512):
    t, d = h.shape
    ns = seq // tm
    fixed = lambda shape: pl.BlockSpec(shape, lambda i: (0, 0))
    tab = pl.BlockSpec((tm, LANES), lambda i: (i % ns, 0))
    return pl.pallas_call(
        _mla_proj_kernel,
        grid=(t // tm,),
        in_specs=[
            pl.BlockSpec((tm, d), lambda i: (i, 0)),
            fixed((1, d)),
            fixed(w_a.shape),
            fixed((1, MLA_Q_RANK)),
            fixed((1, MLA_KV_RANK)),
            fixed(wq.shape),
            fixed(wk.shape),
            fixed(wvt.shape),
            tab,
            tab,
        ],
        out_specs=[
            pl.BlockSpec((2 * MLA_HEADS, tm, LANES), lambda i: (0, i, 0)),
            pl.BlockSpec((MLA_HEADS // 2, LANES, tm), lambda i: (0, 0, i)),
        ],
        out_shape=[
            jax.ShapeDtypeStruct((2 * MLA_HEADS, t, LANES), BF16),
            jax.ShapeDtypeStruct((MLA_HEADS // 2, LANES, t), BF16),
        ],
        compiler_params=_params("parallel"),
        name="mla_proj",
    )(h, gain.reshape(1, d), w_a, q_norm.reshape(1, -1), kv_norm.reshape(1, -1), wq, wk, wvt, cos_t, sin_t)


def _fox_proj_kernel(x_ref, g_ref, wqk_ref, wvt_ref, wf_ref, bf_ref, tri_ref, scat_ref,
                     qk_ref, vt_ref, kd_ref, carry_ref, *, tm):
    @pl.when(pl.program_id(1) == 0)
    def _():
        carry_ref[...] = jnp.zeros_like(carry_ref)

    xn = _rms(x_ref[...], g_ref[...]).astype(BF16)
    qk = _dot(xn, wqk_ref[...])
    for c in range(qk_ref.shape[0]):
        qk_ref[c] = qk[:, c * LANES:(c + 1) * LANES].astype(BF16)
    vt = _dot_nt(wvt_ref[...], xn)
    for c in range(vt_ref.shape[0]):
        vt_ref[c] = vt[c * LANES:(c + 1) * LANES, :].astype(BF16)

    heads = jnp.where(lax.broadcasted_iota(jnp.int32, (1, LANES), 1) < FOX_HEADS, 1.0, 0.0)

    def pack3(x):
        hi, mid, lo = (p.astype(F32) for p in _split3(x * heads))
        return (hi + pltpu.roll(mid, FOX_HEADS, axis=1) + pltpu.roll(lo, 2 * FOX_HEADS, axis=1)).astype(BF16)

    z = _dot(xn, wf_ref[...]) + bf_ref[...]
    logf = jnp.minimum(z, 0.0) - jnp.log(1.0 + jnp.exp(-jnp.abs(z)))
    c3 = _dot(tri_ref[...], pack3(logf))
    cum = (c3 + pltpu.roll(c3, LANES - FOX_HEADS, axis=1)
           + pltpu.roll(c3, LANES - 2 * FOX_HEADS, axis=1)) * heads + carry_ref[...]
    carry_ref[...] = cum[tm - 1:tm, :]
    kd = _dot(pack3(cum * (-LOG2E)), scat_ref[...])
    for c in range(kd_ref.shape[0]):
        kd_ref[c] = kd[:, c * LANES:(c + 1) * LANES].astype(BF16)


def _fox_proj(h, gain, wqk, wvt, wf, bf, *, batch, seq, tm=512):
    t, d = h.shape
    ns = seq // tm
    n_pairs = FOX_HEADS // 2
    tri = jnp.asarray(np.tril(np.ones((tm, tm), np.float32)), BF16)
    scat = np.zeros((LANES, n_pairs * LANES), np.float32)
    for p in range(GATE_PARTS):
        for hd in range(FOX_HEADS):
            scat[p * FOX_HEADS + hd, (hd // 2) * LANES + GATE_PARTS * (hd % 2) + p] = 1.0
    fixed2 = lambda b, i: (0, 0)
    rows = lambda b, i: (0, b * ns + i, 0)
    return pl.pallas_call(
        functools.partial(_fox_proj_kernel, tm=tm),
        grid=(batch, ns),
        in_specs=[
            pl.BlockSpec((tm, d), lambda b, i: (b * ns + i, 0)),
            pl.BlockSpec((1, d), fixed2),
            pl.BlockSpec(wqk.shape, fixed2),
            pl.BlockSpec(wvt.shape, fixed2),
            pl.BlockSpec(wf.shape, fixed2),
            pl.BlockSpec((1, LANES), fixed2),
            pl.BlockSpec((tm, tm), fixed2),
            pl.BlockSpec(scat.shape, fixed2),
        ],
        out_specs=[
            pl.BlockSpec((2 * n_pairs, tm, LANES), rows),
            pl.BlockSpec((n_pairs, LANES, tm), lambda b, i: (0, 0, b * ns + i)),
            pl.BlockSpec((n_pairs, tm, LANES), rows),
        ],
        out_shape=[
            jax.ShapeDtypeStruct((2 * n_pairs, t, LANES), BF16),
            jax.ShapeDtypeStruct((n_pairs, LANES, t), BF16),
            jax.ShapeDtypeStruct((n_pairs, t, LANES), BF16),
        ],
        scratch_shapes=[pltpu.VMEM((1, LANES), F32)],
        compiler_params=_params("arbitrary", "arbitrary"),
        name="fox_proj",
    )(h, gain.reshape(1, d), wqk, wvt, wf, bf, tri, jnp.asarray(scat, BF16))


def _flash_kernel(*refs, sep, has_bias, tq, unroll):
    n_in = 4 if has_bias else 3
    q_ref, k_ref, vt_ref = refs[:3]
    kd_ref = refs[3] if has_bias else None
    o_ref, qs_ref, vta_ref, acc_ref, s_ref, mx_ref, m_ref = refs[n_in:]
    half = LANES // 2
    nq = acc_ref.shape[0]
    seq = nq * tq

    vt = vt_ref[0]
    r = lax.broadcasted_iota(jnp.int32, vt.shape, 0)
    one = jnp.ones_like(vt)
    vta_ref[0] = jnp.where(r < half, vt, one)
    vta_ref[1] = jnp.where(r < half, one, vt)
    row = lax.broadcasted_iota(jnp.int32, (LANES, tq), 0)
    for hh in range(2):
        for t in range(nq):
            cols = slice(t * tq, (t + 1) * tq)
            qt = q_ref[hh if sep else 0, cols, :].astype(F32).T
            if not sep:
                qt = jnp.where((row < half) if hh == 0 else (row >= half), qt, 0.0)
            qs_ref[hh, 0:LANES, cols] = qt.astype(BF16)
        if has_bias:
            r2 = lax.broadcasted_iota(jnp.int32, (LANES, seq), 0)
            sel = jnp.logical_and(r2 >= GATE_PARTS * hh, r2 < GATE_PARTS * (hh + 1))
            qs_ref[hh, LANES:2 * LANES, :] = jnp.where(sel, 1.0, 0.0).astype(BF16)

    def queries(hh, start, n):
        return qs_ref[hh, :, pl.ds(start, n)]

    def keys(hh, start, n):
        k_blk = k_ref[hh if sep else 0, pl.ds(start, n), :]
        if has_bias:
            k_blk = jnp.concatenate([k_blk, kd_ref[0, pl.ds(start, n), :]], axis=1)
        return k_blk

    def scores(i, kb, slot):
        qstart, kstart = pl.multiple_of(i * tq, tq), pl.multiple_of(kb * tq, tq)
        for hh in range(2):
            st = _dot(keys(hh, kstart, tq), queries(hh, qstart, tq))
            s_ref[slot, hh] = st
            mx_ref[slot, hh] = _colmax(st)

    def consume(i, kb, slot):
        kstart = pl.multiple_of(kb * tq, tq)
        for hh in range(2):
            m_old = m_ref[i, hh]
            m_new = jnp.maximum(m_old, mx_ref[slot, hh])
            m_ref[i, hh] = m_new
            pt = jnp.exp2(s_ref[slot, hh] - m_new).astype(BF16)
            pv = _dot(vta_ref[hh, :, pl.ds(kstart, tq)], pt)
            acc_ref[i, hh] = jnp.exp2(m_old - m_new) * acc_ref[i, hh] + pv

    hq = tq // 2
    halves = ((0, hq), (hq, tq))

    def scores_diag(i, slot):
        for hh in range(2):
            for q0, nk in halves:
                st = _dot(keys(hh, i * tq, nk), queries(hh, i * tq + q0, hq))
                key = lax.broadcasted_iota(jnp.int32, (nk, hq), 0)
                qry = lax.broadcasted_iota(jnp.int32, (nk, hq), 1) + q0
                st = jnp.where(key <= qry, st, NEG_INF)
                s_ref[slot, hh, 0:nk, q0:q0 + hq] = st
                mx_ref[slot, hh, :, q0:q0 + hq] = _colmax(st)

    def consume_diag(i, slot):
        for hh in range(2):
            for q0, nk in halves:
                m_new = mx_ref[slot, hh, :, q0:q0 + hq]
                m_ref[i, hh, :, q0:q0 + hq] = m_new
                pt = jnp.exp2(s_ref[slot, hh, 0:nk, q0:q0 + hq] - m_new).astype(BF16)
                acc_ref[i, hh, :, q0:q0 + hq] = _dot(vta_ref[hh, :, i * tq:i * tq + nk], pt)

    scores_diag(0, 0)
    for i in range(nq):
        if i + 1 < nq:
            scores_diag(i + 1, (i + 1) % 2)
        elif nq > 1:
            scores(1, 0, (i + 1) % 2)
        consume_diag(i, i % 2)

    def advance(i, kb):
        last = kb + 1 == i
        return jnp.where(last, i + 1, i), jnp.where(last, 0, kb + 1)

    def trip(_, carry):
        i, kb = carry
        for u in range(unroll):
            ni, nkb = advance(i, kb)
            scores(jnp.minimum(ni, nq - 1), nkb, (nq + u + 1) % 2)
            consume(i, kb, (nq + u) % 2)
            i, kb = ni, nkb
        return i, kb

    n_items = nq * (nq - 1) // 2
    assert unroll % 2 == 0 and n_items % unroll == 0
    lax.fori_loop(0, n_items // unroll, trip, (jnp.int32(1), jnp.int32(0)))

    r = lax.broadcasted_iota(jnp.int32, (LANES, tq), 0)
    for i in range(nq):
        acc_a, acc_b = acc_ref[i, 0], acc_ref[i, 1]
        o_t = jnp.where(r < half, acc_a / acc_a[half:half + 1, :], acc_b / acc_b[0:1, :])
        o_ref[0, i * tq:(i + 1) * tq, :] = o_t.T.astype(o_ref.dtype)


def _flash(qk, vt, kd, *, batch, seq, n_pairs, q_blk, k_blk, sep, tq=512):
    t = qk.shape[1]
    nq = seq // tq
    w = 2 if sep else 1
    kc = 2 * LANES if kd is not None else LANES
    n_items = nq * (nq - 1) // 2
    unroll = max(u for u in (14, 8, 6, 4, 2) if n_items % u == 0)
    in_specs = [
        pl.BlockSpec((w, seq, LANES), lambda b, hp: (q_blk // w + hp, b, 0)),
        pl.BlockSpec((w, seq, LANES), lambda b, hp: (k_blk // w + hp, b, 0)),
        pl.BlockSpec((1, LANES, seq), lambda b, hp: (hp, 0, b)),
    ]
    args = [qk, qk, vt]
    if kd is not None:
        in_specs.append(pl.BlockSpec((1, seq, LANES), lambda b, hp: (hp, b, 0)))
        args.append(kd)
    return pl.pallas_call(
        functools.partial(_flash_kernel, sep=sep, has_bias=kd is not None, tq=tq, unroll=unroll),
        grid=(batch, n_pairs),
        in_specs=in_specs,
        out_specs=pl.BlockSpec((1, seq, LANES), lambda b, hp: (hp, b, 0)),
        out_shape=jax.ShapeDtypeStruct((n_pairs, t, LANES), BF16),
        scratch_shapes=[
            pltpu.VMEM((2, kc, seq), BF16),
            pltpu.VMEM((2, LANES, seq), BF16),
            pltpu.VMEM((nq, 2, LANES, tq), F32),
            pltpu.VMEM((2, 2, tq, tq), F32),
            pltpu.VMEM((2, 2, 1, tq), F32),
            pltpu.VMEM((nq, 2, 1, tq), F32),
        ],
        compiler_params=_params("parallel", "parallel"),
        name="flash_sep" if sep else "flash_shared",
    )(*args)


def _dil_attn_kernel(*refs):
    n_in = 5 * len(DIL_PATTERNS)
    o_ref, og_ref, lg_ref, band_ref = refs[n_in:]
    i = pl.program_id(2)
    w = DIL_WINDOW_KEYS
    v_first = lax.broadcasted_iota(jnp.int32, (w, LANES), 1) < 64
    qi = lax.broadcasted_iota(jnp.int32, (2 * w, 2 * w), 0) % w
    kk = lax.broadcasted_iota(jnp.int32, (2 * w, 2 * w), 1)
    band = jnp.logical_and(kk >= qi, kk <= qi + w)
    band_ref[0] = jnp.where(band, 0.0, NEG_INF)
    band_ref[1] = jnp.where(jnp.logical_and(band, kk >= jnp.where(i > 0, 0, w)), 0.0, NEG_INF)
    ones = jnp.ones((2 * w, LANES), BF16)
    for g, (_, d) in enumerate(DIL_PATTERNS):
        q_ref, k_ref, kp_ref, v_ref, vp_ref = refs[5 * g:5 * g + 5]
        for r in range(d):
            ls = slice(r * LANES, (r + 1) * LANES)
            for sub in range(DIL_TOKENS // d // w):
                rows = slice(sub * w, (sub + 1) * w)
                q_u = jnp.concatenate([q_ref[0, 0, rows, ls], q_ref[1, 0, rows, ls]], axis=0)
                if sub == 0:
                    k_u = jnp.concatenate([kp_ref[0, 0, :, ls], k_ref[0, 0, 0:w, ls]], axis=0)
                    v_u = jnp.concatenate([vp_ref[0, 0, :, ls], v_ref[0, 0, 0:w, ls]], axis=0)
                else:
                    k_u = k_ref[0, 0, (sub - 1) * w:(sub + 1) * w, ls]
                    v_u = v_ref[0, 0, (sub - 1) * w:(sub + 1) * w, ls]
                s = _dot_nt(q_u, k_u) + band_ref[1 if sub == 0 else 0]
                m = jnp.max(s, axis=-1, keepdims=True)
                p = jnp.exp2(s - m).astype(BF16)
                ov = _dot(p, jnp.concatenate([v_u, ones], axis=1))
                num = jnp.where(v_first, ov[0:w, 0:LANES], ov[w:2 * w, 0:LANES])
                den = jnp.where(v_first, ov[0:w, LANES:2 * LANES], ov[w:2 * w, LANES:2 * LANES])
                top = jnp.where(v_first, jnp.broadcast_to(m[0:w], (w, LANES)),
                                jnp.broadcast_to(m[w:2 * w], (w, LANES)))
                tok = pl.ds(sub * w * d + r, w, stride=d) if d > 1 else pl.ds(sub * w, w)
                og_ref[g, tok, :] = num / den
                lg_ref[g, tok, :] = top + jnp.log(den) * LOG2E
    lse = [lg_ref[g] for g in range(len(DIL_PATTERNS))]
    top = functools.reduce(jnp.maximum, lse)
    e = [jnp.exp2(x - top) for x in lse]
    num = sum(e[g] * og_ref[g] for g in range(len(DIL_PATTERNS)))
    o_ref[0] = (num / sum(e)).astype(o_ref.dtype)


def _dil_attn(groups, *, batch, seq):
    n_pairs = DIL_HEADS // 2
    n_tok_blk = seq // DIL_TOKENS
    in_specs, args = [], []
    for arr, (_, d) in zip(groups, DIL_PATTERNS):
        rows = DIL_TOKENS // d
        sub_per_blk = rows // DIL_WINDOW_KEYS

        def cur(base, rows=rows, d=d):
            return pl.BlockSpec((1, 1, rows, d * LANES), lambda b, hp, i: (base + hp, b, i, 0))

        def prev(base, spb=sub_per_blk, d=d):
            return pl.BlockSpec((1, 1, DIL_WINDOW_KEYS, d * LANES),
                                lambda b, hp, i: (base + hp, b, jnp.maximum(i * spb - 1, 0), 0))

        q_spec = pl.BlockSpec((2, 1, rows, d * LANES), lambda b, hp, i: (hp, b, i, 0))
        in_specs += [q_spec, cur(2 * n_pairs), prev(2 * n_pairs), cur(3 * n_pairs), prev(3 * n_pairs)]
        args += [arr] * 5
    n_g = len(DIL_PATTERNS)
    w2 = 2 * DIL_WINDOW_KEYS
    return pl.pallas_call(
        _dil_attn_kernel,
        grid=(batch, n_pairs, n_tok_blk),
        in_specs=in_specs,
        out_specs=pl.BlockSpec((1, DIL_TOKENS, LANES), lambda b, hp, i: (hp, b * n_tok_blk + i, 0)),
        out_shape=jax.ShapeDtypeStruct((n_pairs, batch * seq, LANES), BF16),
        scratch_shapes=[pltpu.VMEM((n_g, DIL_TOKENS, LANES), F32), pltpu.VMEM((n_g, DIL_TOKENS, LANES), F32),
                        pltpu.VMEM((2, w2, w2), F32)],
        compiler_params=_params("parallel", "parallel", "arbitrary"),
        name="dil_attn",
    )(*args)


def _mlp_kernel(h_ref, o_ref, wo_ref, g_ref, wu_ref, wd_ref, fg_ref, out_ref, xn_ref, *, final_norm, tf):
    a = jnp.concatenate([o_ref[k] for k in range(o_ref.shape[0])], axis=-1)
    h1 = h_ref[...] + _dot(a, wo_ref[...])
    xn_ref[...] = _rms(h1, g_ref[...]).astype(BF16)
    y = h1
    for c in range(wu_ref.shape[1] // tf):
        u = jnp.maximum(_dot(xn_ref[...], wu_ref[:, c * tf:(c + 1) * tf]), 0.0)
        y = y + _dot((u * u).astype(BF16), wd_ref[c * tf:(c + 1) * tf, :])
    out_ref[...] = _rms(y, fg_ref[...]) if final_norm else y


def _mlp(h, o, wo, gain, w_up, w_down, final_gain, *, final_norm, tm=512, tf=1024):
    t, d = h.shape
    fixed = lambda shape: pl.BlockSpec(shape, lambda i: (0, 0), pipeline_mode=pl.Buffered(1))
    return pl.pallas_call(
        functools.partial(_mlp_kernel, final_norm=final_norm, tf=tf),
        grid=(t // tm,),
        in_specs=[
            pl.BlockSpec((tm, d), lambda i: (i, 0)),
            pl.BlockSpec((o.shape[0], tm, LANES), lambda i: (0, i, 0)),
            fixed(wo.shape),
            fixed((1, d)),
            fixed(w_up.shape),
            fixed(w_down.shape),
            fixed((1, d)),
        ],
        out_specs=pl.BlockSpec((tm, d), lambda i: (i, 0)),
        out_shape=jax.ShapeDtypeStruct((t, d), F32),
        scratch_shapes=[pltpu.VMEM((tm, d), BF16)],
        compiler_params=_params("parallel"),
        name="mlp_final" if final_norm else "mlp",
    )(h, o, wo, gain.reshape(1, d), w_up, w_down, final_gain.reshape(1, d))


def _rope_halves(seq, dim):
    inv = (1.0 / (np.float32(ROPE_THETA) ** (np.arange(0, dim, 2, dtype=np.float32) / np.float32(dim))))
    ang = np.arange(seq, dtype=np.float32)[:, None] * inv.astype(np.float32)[None, :]
    return np.cos(ang).astype(np.float32), np.sin(ang).astype(np.float32)


def _mla_tables(seq):
    cos, sin = _rope_halves(seq, MLA_ROPE)
    one = lambda n: np.ones((seq, n), np.float32)
    zero = lambda n: np.zeros((seq, n), np.float32)
    cos_t = np.concatenate([one(32), cos, one(16), one(32), cos, one(16)], axis=1)
    sin_t = np.concatenate([zero(32), -sin, zero(16), zero(32), sin, zero(16)], axis=1)
    return jnp.asarray(cos_t), jnp.asarray(sin_t)


def _dil_tables(seq, d, tm):
    cos, sin = _rope_halves(seq, DIL_DIM)
    order = lambda x: x.reshape(seq // tm, tm // d, d, LANES).transpose(0, 2, 1, 3).reshape(seq, LANES)
    return (jnp.asarray(order(np.concatenate([cos] * 4, axis=1))),
            jnp.asarray(order(np.concatenate([-sin, -sin, sin, sin], axis=1))))


def _relayout(w3, moves, n_dst):
    p = np.zeros((w3.shape[-1], n_dst), np.float32)
    for src, dst in moves:
        p[src, dst] = 1.0
    return jnp.einsum('...c,cl->...l', w3, jnp.asarray(p), preferred_element_type=F32).astype(BF16)


def _mla_weights(wq_a, wq_b, wkv_a, wkv_b):
    lanes = lambda c: c if c < 32 else 32 + c if c < 64 else c - 32 if c < 80 else c + 16
    kpe_moves = [(c, c) for c in range(MLA_KV_RANK)] + [
        (MLA_KV_RANK + c, MLA_KV_RANK + lanes(MLA_NOPE + c)) for c in range(MLA_ROPE)]
    w_a = jnp.concatenate([wq_a.astype(BF16), _relayout(wkv_a, kpe_moves, MLA_KV_RANK + LANES)], axis=1)
    scale = (MLA_NOPE + MLA_ROPE) ** -0.5 * LOG2E
    q3 = wq_b.reshape(MLA_Q_RANK, MLA_HEADS, MLA_NOPE + MLA_ROPE) * scale
    wq = _relayout(q3, [(c, lanes(c)) for c in range(MLA_NOPE + MLA_ROPE)], LANES)
    kv3 = wkv_b.reshape(MLA_KV_RANK, MLA_HEADS, MLA_NOPE + MLA_V)
    wk = _relayout(kv3, [(c, lanes(c)) for c in range(MLA_NOPE)], LANES)
    wvt = kv3[..., MLA_NOPE:].reshape(MLA_KV_RANK, -1).T.astype(BF16)
    return w_a, wq.reshape(MLA_Q_RANK, -1), wk.reshape(MLA_KV_RANK, -1), wvt


def _dil_weights(w_qkv):
    d = w_qkv.shape[0]
    n_qk = 2 * len(DIL_PATTERNS) * DIL_HEADS * DIL_DIM
    qk = w_qkv[:, :n_qk].reshape(d, 2, n_qk // (2 * LANES), LANES)
    qk = qk * jnp.array([DIL_DIM ** -0.5 * LOG2E, 1.0], F32).reshape(1, 2, 1, 1)
    old = np.arange(LANES).reshape(2, 2, 32).transpose(1, 0, 2).reshape(LANES)
    qk = _relayout(qk, [(int(old[j]), j) for j in range(LANES)], LANES)
    return qk.reshape(d, n_qk), w_qkv[:, n_qk:].astype(BF16)


def _mla_attn(h, attn_norm, wq_a, q_norm, wq_b, wkv_a, kv_norm, wkv_b, tables, *, batch, seq):
    w_a, wq, wk, wvt = _mla_weights(wq_a, wq_b, wkv_a, wkv_b)
    qk, vt = _mla_proj(h, attn_norm, w_a, q_norm, kv_norm, wq, wk, wvt, *tables, seq=seq)
    return _flash(qk, vt, None, batch=batch, seq=seq, n_pairs=MLA_HEADS // 2,
                  q_blk=0, k_blk=MLA_HEADS, sep=True)


def _fox_attn(h, attn_norm, w_qkv, w_f, b_f, *, batch, seq):
    n = FOX_HEADS * FOX_DIM
    d = w_qkv.shape[0]
    wqk = jnp.concatenate([w_qkv[:, :n] * (FOX_DIM ** -0.5 * LOG2E), w_qkv[:, n:2 * n]], axis=1)
    wvt = w_qkv[:, 2 * n:].T
    wf = jnp.concatenate([w_f, jnp.zeros((d, LANES - FOX_HEADS), F32)], axis=1)
    bf = jnp.concatenate([b_f.astype(F32), jnp.zeros((LANES - FOX_HEADS,), F32)]).reshape(1, LANES)
    qk, vt, kd = _fox_proj(h, attn_norm, wqk.astype(BF16), wvt.astype(BF16), wf.astype(BF16), bf,
                           batch=batch, seq=seq)
    n_pairs = FOX_HEADS // 2
    return _flash(qk, vt, kd, batch=batch, seq=seq, n_pairs=n_pairs, q_blk=0, k_blk=n_pairs, sep=False)


def _dil_attn_layer(h, attn_norm, w_qkv, *, batch, seq, tm=256):
    wqk, wv = _dil_weights(w_qkv)
    tables = [_dil_tables(seq, d, tm) for _, d in DIL_PATTERNS]
    groups = _dil_proj(h, attn_norm, wqk, wv, tables, batch=batch, seq=seq, tm=tm)
    return _dil_attn(groups, batch=batch, seq=seq)


def kernel(x, l0_attn_norm, l0_mla_wq_a, l0_mla_q_norm, l0_mla_wq_b, l0_mla_wkv_a, l0_mla_kv_norm, l0_mla_wkv_b, l0_mla_wo, l0_mlp_norm, l0_w_up, l0_w_down, l1_attn_norm, l1_fox_w_qkv, l1_fox_w_f, l1_fox_b_f, l1_fox_wo, l1_mlp_norm, l1_w_up, l1_w_down, l2_attn_norm, l2_dil_w_qkv, l2_dil_wo, l2_mlp_norm, l2_w_up, l2_w_down, l3_attn_norm, l3_mla_wq_a, l3_mla_q_norm, l3_mla_wq_b, l3_mla_wkv_a, l3_mla_kv_norm, l3_mla_wkv_b, l3_mla_wo, l3_mlp_norm, l3_w_up, l3_w_down, final_norm):
    batch, seq, d = x.shape
    assert seq % DIL_TOKENS == 0 and (batch * seq) % 1024 == 0
    for window, dil in DIL_PATTERNS:
        assert window // dil == DIL_WINDOW_KEYS
    mla_t = _mla_tables(seq)
    kw = dict(batch=batch, seq=seq)
    bf = lambda w: w.astype(BF16)
    h = x.reshape(batch * seq, d)

    o = _mla_attn(h, l0_attn_norm, l0_mla_wq_a, l0_mla_q_norm, l0_mla_wq_b, l0_mla_wkv_a,
                  l0_mla_kv_norm, l0_mla_wkv_b, mla_t, **kw)
    h = _mlp(h, o, bf(l0_mla_wo), l0_mlp_norm, bf(l0_w_up), bf(l0_w_down), final_norm, final_norm=False)

    o = _fox_attn(h, l1_attn_norm, l1_fox_w_qkv, l1_fox_w_f, l1_fox_b_f, **kw)
    h = _mlp(h, o, bf(l1_fox_wo), l1_mlp_norm, bf(l1_w_up), bf(l1_w_down), final_norm, final_norm=False)

    o = _dil_attn_layer(h, l2_attn_norm, l2_dil_w_qkv, **kw)
    h = _mlp(h, o, bf(l2_dil_wo), l2_mlp_norm, bf(l2_w_up), bf(l2_w_down), final_norm, final_norm=False)

    o = _mla_attn(h, l3_attn_norm, l3_mla_wq_a, l3_mla_q_norm, l3_mla_wq_b, l3_mla_wkv_a,
                  l3_mla_kv_norm, l3_mla_wkv_b, mla_t, **kw)
    h = _mlp(h, o, bf(l3_mla_wo), l3_mlp_norm, bf(l3_w_up), bf(l3_w_down), final_norm, final_norm=True)
    return h.reshape(batch, seq, d)
```

```python
import functools
import math

import numpy as np
import jax
import jax.numpy as jnp
from jax import lax
from jax.experimental import pallas as pl
from jax.experimental.pallas import tpu as pltpu

F32 = jnp.float32
BF16 = jnp.bfloat16

LANES = 128
NORM_EPS = 1e-6
NEG_INF = -1e30
ROPE_THETA = 10000.0
LOG2E = math.log2(math.e)

MLA_HEADS = 16
MLA_Q_RANK = 384
MLA_KV_RANK = 256
MLA_NOPE = 64
MLA_ROPE = 32
MLA_V = 64
FOX_HEADS = 16
FOX_DIM = 64
DIL_PATTERNS = ((128, 1), (512, 4), (2048, 16))
DIL_HEADS = 16
DIL_DIM = 64
DIL_WINDOW_KEYS = 128
DIL_TOKENS = 2048
GATE_PARTS = 3

VMEM_LIMIT = 56 * 2**20


def _params(*sem):
    return pltpu.CompilerParams(dimension_semantics=sem, vmem_limit_bytes=VMEM_LIMIT)


def _rms(x, g):
    ms = jnp.mean(x * x, axis=-1, keepdims=True)
    return x * lax.rsqrt(ms + NORM_EPS) * g


def _dot(a, b):
    return jnp.dot(a, b, preferred_element_type=F32)


def _dot_nt(a, b):
    return lax.dot_general(a, b, (((1,), (1,)), ((), ())), preferred_element_type=F32)


def _rope(blk, cos, sin):
    return blk * cos + pltpu.roll(blk, 64, axis=1) * sin


def _colmax(st):
    n, c = st.shape
    return jnp.max(jnp.max(st.reshape(8, n // 8, c), axis=0), axis=0, keepdims=True)


def _split3(x):
    hi = x.astype(BF16)
    r1 = x - hi.astype(F32)
    mid = r1.astype(BF16)
    lo = (r1 - mid.astype(F32)).astype(BF16)
    return hi, mid, lo


def _dil_proj_kernel(*refs, tm):
    n_g = len(DIL_PATTERNS)
    x_ref, g_ref, wqk_ref, wv_ref = refs[:4]
    tabs = refs[4:4 + 2 * n_g]
    outs = refs[4 + 2 * n_g:4 + 3 * n_g]
    xs_ref, xn_ref = refs[4 + 3 * n_g:]
    nc = x_ref.shape[1] // LANES
    n = wv_ref.shape[1] // n_g
    nb = n // LANES
    xn = _rms(x_ref[...], g_ref[...])
    for c in range(nc):
        xs_ref[c] = xn[:, c * LANES:(c + 1) * LANES]
    lane = lax.broadcasted_iota(jnp.int32, (1, LANES), 1)
    first = jnp.where((lane % 64) < 32, 1.0, 0.0)
    for g, (_, d) in enumerate(DIL_PATTERNS):
        rpr = tm // d
        cos_ref, sin_ref, o_ref = tabs[2 * g], tabs[2 * g + 1], outs[g]
        if d == 1:
            xn_ref[g] = xn.astype(BF16)
        else:
            for c in range(nc):
                for r in range(d):
                    xn_ref[g, r * rpr:(r + 1) * rpr, c * LANES:(c + 1) * LANES] = (
                        xs_ref[c, pl.ds(r, rpr, stride=d), :].astype(BF16))
        weights = (wqk_ref[:, g * n:(g + 1) * n], wqk_ref[:, (n_g + g) * n:(n_g + g + 1) * n],
                   wv_ref[:, g * n:(g + 1) * n])
        for j in range(3):
            acc = _dot(xn_ref[g], weights[j])
            for c in range(nb):
                for r in range(d):
                    rs = slice(r * rpr, (r + 1) * rpr)
                    ls = slice(r * LANES, (r + 1) * LANES)
                    blk = acc[rs, c * LANES:(c + 1) * LANES]
                    if j < 2:
                        blk = _rope(blk, cos_ref[rs, :], sin_ref[rs, :])
                    if j == 0:
                        o_ref[2 * c, 0, :, ls] = (blk * first).astype(o_ref.dtype)
                        o_ref[2 * c + 1, 0, :, ls] = (blk * (1.0 - first)).astype(o_ref.dtype)
                    else:
                        o_ref[(j + 1) * nb + c, 0, :, ls] = blk.astype(o_ref.dtype)


def _dil_proj(h, gain, wqk, wv, tables, *, batch, seq, tm):
    t, dm = h.shape
    ns = seq // tm
    n_g = len(DIL_PATTERNS)
    nb = 4 * wv.shape[1] // (n_g * LANES)
    fixed = lambda shape: pl.BlockSpec(shape, lambda b, i: (0, 0), pipeline_mode=pl.Buffered(1))
    tab = pl.BlockSpec((tm, LANES), lambda b, i: (i, 0))
    return pl.pallas_call(
        functools.partial(_dil_proj_kernel, tm=tm),
        grid=(batch, ns),
        in_specs=[pl.BlockSpec((tm, dm), lambda b, i: (b * ns + i, 0)), fixed((1, dm)),
                  fixed(wqk.shape), fixed(wv.shape)] + [tab] * (2 * n_g),
        out_specs=[pl.BlockSpec((nb, 1, tm // d, d * LANES), lambda b, i: (0, b, i, 0))
                   for _, d in DIL_PATTERNS],
        out_shape=[jax.ShapeDtypeStruct((nb, batch, seq // d, d * LANES), BF16) for _, d in DIL_PATTERNS],
        scratch_shapes=[pltpu.VMEM((dm // LANES, tm, LANES), F32), pltpu.VMEM((n_g, tm, dm), BF16)],
        compiler_params=_params("parallel", "parallel"),
        name="dil_proj",
    )(h, gain.reshape(1, dm), wqk, wv, *[x for pair in tables for x in pair])


def _mla_proj_kernel(x_ref, g_ref, wa_ref, qn_ref, kvn_ref, wq_ref, wk_ref, wvt_ref, cos_ref, sin_ref,
                     qk_ref, vt_ref):
    cos, sin = cos_ref[...], sin_ref[...]
    xn = _rms(x_ref[...], g_ref[...]).astype(BF16)
    y = _dot(xn, wa_ref[...])
    cq = _rms(y[:, :MLA_Q_RANK], qn_ref[...]).astype(BF16)
    ckv = _rms(y[:, MLA_Q_RANK:MLA_Q_RANK + MLA_KV_RANK], kvn_ref[...]).astype(BF16)
    kpe = _rope(y[:, MLA_Q_RANK + MLA_KV_RANK:], cos, sin)
    q = _dot(cq, wq_ref[...])
    for c in range(MLA_HEADS):
        qk_ref[c] = _rope(q[:, c * LANES:(c + 1) * LANES], cos, sin).astype(BF16)
    k = _dot(ckv, wk_ref[...])
    for c in range(MLA_HEADS):
        qk_ref[MLA_HEADS + c] = (k[:, c * LANES:(c + 1) * LANES] + kpe).astype(BF16)
    vt = _dot_nt(wvt_ref[...], ckv)
    for c in range(MLA_HEADS // 2):
        vt_ref[c] = vt[c * LANES:(c + 1) * LANES, :].astype(BF16)


def _mla_proj(h, gain, w_a, q_norm, kv_norm, wq, wk, wvt, cos_t, sin_t, *, seq, tm=512):
    t, d = h.shape
    ns = seq // tm
    fixed = lambda shape: pl.BlockSpec(shape, lambda i: (0, 0))
    tab = pl.BlockSpec((tm, LANES), lambda i: (i % ns, 0))
    return pl.pallas_call(
        _mla_proj_kernel,
        grid=(t // tm,),
        in_specs=[
            pl.BlockSpec((tm, d), lambda i: (i, 0)),
            fixed((1, d)),
            fixed(w_a.shape),
            fixed((1, MLA_Q_RANK)),
            fixed((1, MLA_KV_RANK)),
            fixed(wq.shape),
            fixed(wk.shape),
            fixed(wvt.shape),
            tab,
            tab,
        ],
        out_specs=[
            pl.BlockSpec((2 * MLA_HEADS, tm, LANES), lambda i: (0, i, 0)),
            pl.BlockSpec((MLA_HEADS // 2, LANES, tm), lambda i: (0, 0, i)),
        ],
        out_shape=[
            jax.ShapeDtypeStruct((2 * MLA_HEADS, t, LANES), BF16),
            jax.ShapeDtypeStruct((MLA_HEADS // 2, LANES, t), BF16),
        ],
        compiler_params=_params("parallel"),
        name="mla_proj",
    )(h, gain.reshape(1, d), w_a, q_norm.reshape(1, -1), kv_norm.reshape(1, -1), wq, wk, wvt, cos_t, sin_t)


def _fox_proj_kernel(x_ref, g_ref, wqk_ref, wvt_ref, wf_ref, bf_ref, tri_ref, scat_ref,
                     qk_ref, vt_ref, kd_ref, carry_ref, *, tm):
    @pl.when(pl.program_id(1) == 0)
    def _():
        carry_ref[...] = jnp.zeros_like(carry_ref)

    xn = _rms(x_ref[...], g_ref[...]).astype(BF16)
    qk = _dot(xn, wqk_ref[...])
    for c in range(qk_ref.shape[0]):
        qk_ref[c] = qk[:, c * LANES:(c + 1) * LANES].astype(BF16)
    vt = _dot_nt(wvt_ref[...], xn)
    for c in range(vt_ref.shape[0]):
        vt_ref[c] = vt[c * LANES:(c + 1) * LANES, :].astype(BF16)

    heads = jnp.where(lax.broadcasted_iota(jnp.int32, (1, LANES), 1) < FOX_HEADS, 1.0, 0.0)

    def pack3(x):
        hi, mid, lo = (p.astype(F32) for p in _split3(x * heads))
        return (hi + pltpu.roll(mid, FOX_HEADS, axis=1) + pltpu.roll(lo, 2 * FOX_HEADS, axis=1)).astype(BF16)

    z = _dot(xn, wf_ref[...]) + bf_ref[...]
    logf = jnp.minimum(z, 0.0) - jnp.log(1.0 + jnp.exp(-jnp.abs(z)))
    c3 = _dot(tri_ref[...], pack3(logf))
    cum = (c3 + pltpu.roll(c3, LANES - FOX_HEADS, axis=1)
           + pltpu.roll(c3, LANES - 2 * FOX_HEADS, axis=1)) * heads + carry_ref[...]
    carry_ref[...] = cum[tm - 1:tm, :]
    kd = _dot(pack3(cum * (-LOG2E)), scat_ref[...])
    for c in range(kd_ref.shape[0]):
        kd_ref[c] = kd[:, c * LANES:(c + 1) * LANES].astype(BF16)


def _fox_proj(h, gain, wqk, wvt, wf, bf, *, batch, seq, tm=512):
    t, d = h.shape
    ns = seq // tm
    n_pairs = FOX_HEADS // 2
    tri = jnp.asarray(np.tril(np.ones((tm, tm), np.float32)), BF16)
    scat = np.zeros((LANES, n_pairs * LANES), np.float32)
    for p in range(GATE_PARTS):
        for hd in range(FOX_HEADS):
            scat[p * FOX_HEADS + hd, (hd // 2) * LANES + GATE_PARTS * (hd % 2) + p] = 1.0
    fixed2 = lambda b, i: (0, 0)
    rows = lambda b, i: (0, b * ns + i, 0)
    return pl.pallas_call(
        functools.partial(_fox_proj_kernel, tm=tm),
        grid=(batch, ns),
        in_specs=[
            pl.BlockSpec((tm, d), lambda b, i: (b * ns + i, 0)),
            pl.BlockSpec((1, d), fixed2),
            pl.BlockSpec(wqk.shape, fixed2),
            pl.BlockSpec(wvt.shape, fixed2),
            pl.BlockSpec(wf.shape, fixed2),
            pl.BlockSpec((1, LANES), fixed2),
            pl.BlockSpec((tm, tm), fixed2),
            pl.BlockSpec(scat.shape, fixed2),
        ],
        out_specs=[
            pl.BlockSpec((2 * n_pairs, tm, LANES), rows),
            pl.BlockSpec((n_pairs, LANES, tm), lambda b, i: (0, 0, b * ns + i)),
            pl.BlockSpec((n_pairs, tm, LANES), rows),
        ],
        out_shape=[
            jax.ShapeDtypeStruct((2 * n_pairs, t, LANES), BF16),
            jax.ShapeDtypeStruct((n_pairs, LANES, t), BF16),
            jax.ShapeDtypeStruct((n_pairs, t, LANES), BF16),
        ],
        scratch_shapes=[pltpu.VMEM((1, LANES), F32)],
        compiler_params=_params("arbitrary", "arbitrary"),
        name="fox_proj",
    )(h, gain.reshape(1, d), wqk, wvt, wf, bf, tri, jnp.asarray(scat, BF16))


def _flash_kernel(*refs, sep, has_bias, tq, unroll):
    n_in = 4 if has_bias else 3
    q_ref, k_ref, vt_ref = refs[:3]
    kd_ref = refs[3] if has_bias else None
    o_ref, qs_ref, vta_ref, acc_ref, s_ref, mx_ref, m_ref = refs[n_in:]
    half = LANES // 2
    nq = acc_ref.shape[0]
    seq = nq * tq

    vt = vt_ref[0]
    r = lax.broadcasted_iota(jnp.int32, vt.shape, 0)
    one = jnp.ones_like(vt)
    vta_ref[0] = jnp.where(r < half, vt, one)
    vta_ref[1] = jnp.where(r < half, one, vt)
    row = lax.broadcasted_iota(jnp.int32, (LANES, tq), 0)
    for hh in range(2):
        for t in range(nq):
            cols = slice(t * tq, (t + 1) * tq)
            qt = q_ref[hh if sep else 0, cols, :].astype(F32).T
            if not sep:
                qt = jnp.where((row < half) if hh == 0 else (row >= half), qt, 0.0)
            qs_ref[hh, 0:LANES, cols] = qt.astype(BF16)
        if has_bias:
            r2 = lax.broadcasted_iota(jnp.int32, (LANES, seq), 0)
            sel = jnp.logical_and(r2 >= GATE_PARTS * hh, r2 < GATE_PARTS * (hh + 1))
            qs_ref[hh, LANES:2 * LANES, :] = jnp.where(sel, 1.0, 0.0).astype(BF16)

    def queries(hh, start, n):
        return qs_ref[hh, :, pl.ds(start, n)]

    def keys(hh, start, n):
        k_blk = k_ref[hh if sep else 0, pl.ds(start, n), :]
        if has_bias:
            k_blk = jnp.concatenate([k_blk, kd_ref[0, pl.ds(start, n), :]], axis=1)
        return k_blk

    def scores(i, kb, slot):
        qstart, kstart = pl.multiple_of(i * tq, tq), pl.multiple_of(kb * tq, tq)
        for hh in range(2):
            st = _dot(keys(hh, kstart, tq), queries(hh, qstart, tq))
            s_ref[slot, hh] = st
            mx_ref[slot, hh] = _colmax(st)

    def consume(i, kb, slot):
        kstart = pl.multiple_of(kb * tq, tq)
        for hh in range(2):
            m_old = m_ref[i, hh]
            m_new = jnp.maximum(m_old, mx_ref[slot, hh])
            m_ref[i, hh] = m_new
            pt = jnp.exp2(s_ref[slot, hh] - m_new).astype(BF16)
            pv = _dot(vta_ref[hh, :, pl.ds(kstart, tq)], pt)
            acc_ref[i, hh] = jnp.exp2(m_old - m_new) * acc_ref[i, hh] + pv

    hq = tq // 2
    halves = ((0, hq), (hq, tq))

    def scores_diag(i, slot):
        for hh in range(2):
            for q0, nk in halves:
                st = _dot(keys(hh, i * tq, nk), queries(hh, i * tq + q0, hq))
                key = lax.broadcasted_iota(jnp.int32, (nk, hq), 0)
                qry = lax.broadcasted_iota(jnp.int32, (nk, hq), 1) + q0
                st = jnp.where(key <= qry, st, NEG_INF)
                s_ref[slot, hh, 0:nk, q0:q0 + hq] = st
                mx_ref[slot, hh, :, q0:q0 + hq] = _colmax(st)

    def consume_diag(i, slot):
        for hh in range(2):
            for q0, nk in halves:
                m_new = mx_ref[slot, hh, :, q0:q0 + hq]
                m_ref[i, hh, :, q0:q0 + hq] = m_new
                pt = jnp.exp2(s_ref[slot, hh, 0:nk, q0:q0 + hq] - m_new).astype(BF16)
                acc_ref[i, hh, :, q0:q0 + hq] = _dot(vta_ref[hh, :, i * tq:i * tq + nk], pt)

    scores_diag(0, 0)
    for i in range(nq):
        if i + 1 < nq:
            scores_diag(i + 1, (i + 1) % 2)
        elif nq > 1:
            scores(1, 0, (i + 1) % 2)
        consume_diag(i, i % 2)

    def advance(i, kb):
        last = kb + 1 == i
        return jnp.where(last, i + 1, i), jnp.where(last, 0, kb + 1)

    def trip(_, carry):
        i, kb = carry
        for u in range(unroll):
            ni, nkb = advance(i, kb)
            scores(jnp.minimum(ni, nq - 1), nkb, (nq + u + 1) % 2)
            consume(i, kb, (nq + u) % 2)
            i, kb = ni, nkb
        return i, kb

    n_items = nq * (nq - 1) // 2
    assert unroll % 2 == 0 and n_items % unroll == 0
    lax.fori_loop(0, n_items // unroll, trip, (jnp.int32(1), jnp.int32(0)))

    r = lax.broadcasted_iota(jnp.int32, (LANES, tq), 0)
    for i in range(nq):
        acc_a, acc_b = acc_ref[i, 0], acc_ref[i, 1]
        o_t = jnp.where(r < half, acc_a / acc_a[half:half + 1, :], acc_b / acc_b[0:1, :])
        o_ref[0, i * tq:(i + 1) * tq, :] = o_t.T.astype(o_ref.dtype)


def _flash(qk, vt, kd, *, batch, seq, n_pairs, q_blk, k_blk, sep, tq=512):
    t = qk.shape[1]
    nq = seq // tq
    w = 2 if sep else 1
    kc = 2 * LANES if kd is not None else LANES
    n_items = nq * (nq - 1) // 2
    unroll = max(u for u in (14, 8, 6, 4, 2) if n_items % u == 0)
    in_specs = [
        pl.BlockSpec((w, seq, LANES), lambda b, hp: (q_blk // w + hp, b, 0)),
        pl.BlockSpec((w, seq, LANES), lambda b, hp: (k_blk // w + hp, b, 0)),
        pl.BlockSpec((1, LANES, seq), lambda b, hp: (hp, 0, b)),
    ]
    args = [qk, qk, vt]
    if kd is not None:
        in_specs.append(pl.BlockSpec((1, seq, LANES), lambda b, hp: (hp, b, 0)))
        args.append(kd)
    return pl.pallas_call(
        functools.partial(_flash_kernel, sep=sep, has_bias=kd is not None, tq=tq, unroll=unroll),
        grid=(batch, n_pairs),
        in_specs=in_specs,
        out_specs=pl.BlockSpec((1, seq, LANES), lambda b, hp: (hp, b, 0)),
        out_shape=jax.ShapeDtypeStruct((n_pairs, t, LANES), BF16),
        scratch_shapes=[
            pltpu.VMEM((2, kc, seq), BF16),
            pltpu.VMEM((2, LANES, seq), BF16),
            pltpu.VMEM((nq, 2, LANES, tq), F32),
            pltpu.VMEM((2, 2, tq, tq), F32),
            pltpu.VMEM((2, 2, 1, tq), F32),
            pltpu.VMEM((nq, 2, 1, tq), F32),
        ],
        compiler_params=_params("parallel", "parallel"),
        name="flash_sep" if sep else "flash_shared",
    )(*args)


def _dil_attn_kernel(*refs):
    n_in = 5 * len(DIL_PATTERNS)
    o_ref, og_ref, dg_ref, lg_ref, band_ref = refs[n_in:]
    i = pl.program_id(2)
    w = DIL_WINDOW_KEYS
    v_first = lax.broadcasted_iota(jnp.int32, (w, LANES), 1) < 64
    qi = lax.broadcasted_iota(jnp.int32, (2 * w, 2 * w), 0) % w
    kk = lax.broadcasted_iota(jnp.int32, (2 * w, 2 * w), 1)
    band = jnp.logical_and(kk >= qi, kk <= qi + w)
    band_ref[0] = jnp.where(band, 0.0, NEG_INF)
    band_ref[1] = jnp.where(jnp.logical_and(band, kk >= jnp.where(i > 0, 0, w)), 0.0, NEG_INF)
    ones = jnp.ones((2 * w, LANES), BF16)
    for g, (_, d) in enumerate(DIL_PATTERNS):
        q_ref, k_ref, kp_ref, v_ref, vp_ref = refs[5 * g:5 * g + 5]
        for r in range(d):
            ls = slice(r * LANES, (r + 1) * LANES)
            for sub in range(DIL_TOKENS // d // w):
                rows = slice(sub * w, (sub + 1) * w)
                q_u = jnp.concatenate([q_ref[0, 0, rows, ls], q_ref[1, 0, rows, ls]], axis=0)
                if sub == 0:
                    k_u = jnp.concatenate([kp_ref[0, 0, :, ls], k_ref[0, 0, 0:w, ls]], axis=0)
                    v_u = jnp.concatenate([vp_ref[0, 0, :, ls], v_ref[0, 0, 0:w, ls]], axis=0)
                else:
                    k_u = k_ref[0, 0, (sub - 1) * w:(sub + 1) * w, ls]
                    v_u = v_ref[0, 0, (sub - 1) * w:(sub + 1) * w, ls]
                s = _dot_nt(q_u, k_u) + band_ref[1 if sub == 0 else 0]
                m = jnp.max(s, axis=-1, keepdims=True)
                p = jnp.exp2(s - m).astype(BF16)
                ov = _dot(p, jnp.concatenate([v_u, ones], axis=1))
                num = jnp.where(v_first, ov[0:w, 0:LANES], ov[w:2 * w, 0:LANES])
                den = jnp.where(v_first, ov[0:w, LANES:2 * LANES], ov[w:2 * w, LANES:2 * LANES])
                top = jnp.where(v_first, jnp.broadcast_to(m[0:w], (w, LANES)),
                                jnp.broadcast_to(m[w:2 * w], (w, LANES)))
                tok = pl.ds(sub * w * d + r, w, stride=d) if d > 1 else pl.ds(sub * w, w)
                og_ref[g, tok, :] = num
                dg_ref[g, tok, :] = den
                lg_ref[g, tok, :] = top
    n_g = len(DIL_PATTERNS)
    tops = [lg_ref[g] for g in range(n_g)]
    top = functools.reduce(jnp.maximum, tops)
    e = [jnp.exp2(x - top) for x in tops]
    num = sum(e[g] * og_ref[g] for g in range(n_g))
    den = sum(e[g] * dg_ref[g] for g in range(n_g))
    o_ref[0] = (num / den).astype(o_ref.dtype)


def _dil_attn(groups, *, batch, seq):
    n_pairs = DIL_HEADS // 2
    n_tok_blk = seq // DIL_TOKENS
    in_specs, args = [], []
    for arr, (_, d) in zip(groups, DIL_PATTERNS):
        rows = DIL_TOKENS // d
        sub_per_blk = rows // DIL_WINDOW_KEYS

        def cur(base, rows=rows, d=d):
            return pl.BlockSpec((1, 1, rows, d * LANES), lambda b, hp, i: (base + hp, b, i, 0))

        def prev(base, spb=sub_per_blk, d=d):
            return pl.BlockSpec((1, 1, DIL_WINDOW_KEYS, d * LANES),
                                lambda b, hp, i: (base + hp, b, jnp.maximum(i * spb - 1, 0), 0))

        q_spec = pl.BlockSpec((2, 1, rows, d * LANES), lambda b, hp, i: (hp, b, i, 0))
        in_specs += [q_spec, cur(2 * n_pairs), prev(2 * n_pairs), cur(3 * n_pairs), prev(3 * n_pairs)]
        args += [arr] * 5
    n_g = len(DIL_PATTERNS)
    w2 = 2 * DIL_WINDOW_KEYS
    return pl.pallas_call(
        _dil_attn_kernel,
        grid=(batch, n_pairs, n_tok_blk),
        in_specs=in_specs,
        out_specs=pl.BlockSpec((1, DIL_TOKENS, LANES), lambda b, hp, i: (hp, b * n_tok_blk + i, 0)),
        out_shape=jax.ShapeDtypeStruct((n_pairs, batch * seq, LANES), BF16),
        scratch_shapes=[pltpu.VMEM((n_g, DIL_TOKENS, LANES), F32)] * 3 + [
                        pltpu.VMEM((2, w2, w2), F32)],
        compiler_params=_params("parallel", "parallel", "arbitrary"),
        name="dil_attn",
    )(*args)


def _mlp_kernel(h_ref, o_ref, wo_ref, g_ref, wu_ref, wd_ref, fg_ref, out_ref, xn_ref, *, final_norm, tf):
    a = jnp.concatenate([o_ref[k] for k in range(o_ref.shape[0])], axis=-1)
    h1 = h_ref[...] + _dot(a, wo_ref[...])
    xn_ref[...] = _rms(h1, g_ref[...]).astype(BF16)
    y = h1
    for c in range(wu_ref.shape[1] // tf):
        u = jnp.maximum(_dot(xn_ref[...], wu_ref[:, c * tf:(c + 1) * tf]), 0.0)
        y = y + _dot((u * u).astype(BF16), wd_ref[c * tf:(c + 1) * tf, :])
    out_ref[...] = _rms(y, fg_ref[...]) if final_norm else y


def _mlp(h, o, wo, gain, w_up, w_down, final_gain, *, final_norm, tm=512, tf=1024):
    t, d = h.shape
    fixed = lambda shape: pl.BlockSpec(shape, lambda i: (0, 0), pipeline_mode=pl.Buffered(1))
    return pl.pallas_call(
        functools.partial(_mlp_kernel, final_norm=final_norm, tf=tf),
        grid=(t // tm,),
        in_specs=[
            pl.BlockSpec((tm, d), lambda i: (i, 0)),
            pl.BlockSpec((o.shape[0], tm, LANES), lambda i: (0, i, 0)),
            fixed(wo.shape),
            fixed((1, d)),
            fixed(w_up.shape),
            fixed(w_down.shape),
            fixed((1, d)),
        ],
        out_specs=pl.BlockSpec((tm, d), lambda i: (i, 0)),
        out_shape=jax.ShapeDtypeStruct((t, d), F32),
        scratch_shapes=[pltpu.VMEM((tm, d), BF16)],
        compiler_params=_params("parallel"),
        name="mlp_final" if final_norm else "mlp",
    )(h, o, wo, gain.reshape(1, d), w_up, w_down, final_gain.reshape(1, d))


def _rope_halves(seq, dim):
    inv = (1.0 / (np.float32(ROPE_THETA) ** (np.arange(0, dim, 2, dtype=np.float32) / np.float32(dim))))
    ang = np.arange(seq, dtype=np.float32)[:, None] * inv.astype(np.float32)[None, :]
    return np.cos(ang).astype(np.float32), np.sin(ang).astype(np.float32)


def _mla_tables(seq):
    cos, sin = _rope_halves(seq, MLA_ROPE)
    one = lambda n: np.ones((seq, n), np.float32)
    zero = lambda n: np.zeros((seq, n), np.float32)
    cos_t = np.concatenate([one(32), cos, one(16), one(32), cos, one(16)], axis=1)
    sin_t = np.concatenate([zero(32), -sin, zero(16), zero(32), sin, zero(16)], axis=1)
    return jnp.asarray(cos_t), jnp.asarray(sin_t)


def _dil_tables(seq, d, tm):
    cos, sin = _rope_halves(seq, DIL_DIM)
    order = lambda x: x.reshape(seq // tm, tm // d, d, LANES).transpose(0, 2, 1, 3).reshape(seq, LANES)
    return (jnp.asarray(order(np.concatenate([cos] * 4, axis=1))),
            jnp.asarray(order(np.concatenate([-sin, -sin, sin, sin], axis=1))))


def _relayout(w3, moves, n_dst):
    p = np.zeros((w3.shape[-1], n_dst), np.float32)
    for src, dst in moves:
        p[src, dst] = 1.0
    return jnp.einsum('...c,cl->...l', w3, jnp.asarray(p), preferred_element_type=F32).astype(BF16)


def _mla_weights(wq_a, wq_b, wkv_a, wkv_b):
    lanes = lambda c: c if c < 32 else 32 + c if c < 64 else c - 32 if c < 80 else c + 16
    kpe_moves = [(c, c) for c in range(MLA_KV_RANK)] + [
        (MLA_KV_RANK + c, MLA_KV_RANK + lanes(MLA_NOPE + c)) for c in range(MLA_ROPE)]
    w_a = jnp.concatenate([wq_a.astype(BF16), _relayout(wkv_a, kpe_moves, MLA_KV_RANK + LANES)], axis=1)
    scale = (MLA_NOPE + MLA_ROPE) ** -0.5 * LOG2E
    q3 = wq_b.reshape(MLA_Q_RANK, MLA_HEADS, MLA_NOPE + MLA_ROPE) * scale
    wq = _relayout(q3, [(c, lanes(c)) for c in range(MLA_NOPE + MLA_ROPE)], LANES)
    kv3 = wkv_b.reshape(MLA_KV_RANK, MLA_HEADS, MLA_NOPE + MLA_V)
    wk = _relayout(kv3, [(c, lanes(c)) for c in range(MLA_NOPE)], LANES)
    wvt = kv3[..., MLA_NOPE:].reshape(MLA_KV_RANK, -1).T.astype(BF16)
    return w_a, wq.reshape(MLA_Q_RANK, -1), wk.reshape(MLA_KV_RANK, -1), wvt


def _dil_weights(w_qkv):
    d = w_qkv.shape[0]
    n_qk = 2 * len(DIL_PATTERNS) * DIL_HEADS * DIL_DIM
    qk = w_qkv[:, :n_qk].reshape(d, 2, n_qk // (2 * LANES), LANES)
    qk = qk * jnp.array([DIL_DIM ** -0.5 * LOG2E, 1.0], F32).reshape(1, 2, 1, 1)
    old = np.arange(LANES).reshape(2, 2, 32).transpose(1, 0, 2).reshape(LANES)
    qk = _relayout(qk, [(int(old[j]), j) for j in range(LANES)], LANES)
    return qk.reshape(d, n_qk), w_qkv[:, n_qk:].astype(BF16)


def _mla_attn(h, attn_norm, wq_a, q_norm, wq_b, wkv_a, kv_norm, wkv_b, tables, *, batch, seq):
    w_a, wq, wk, wvt = _mla_weights(wq_a, wq_b, wkv_a, wkv_b)
    qk, vt = _mla_proj(h, attn_norm, w_a, q_norm, kv_norm, wq, wk, wvt, *tables, seq=seq)
    return _flash(qk, vt, None, batch=batch, seq=seq, n_pairs=MLA_HEADS // 2,
                  q_blk=0, k_blk=MLA_HEADS, sep=True)


def _fox_attn(h, attn_norm, w_qkv, w_f, b_f, *, batch, seq):
    n = FOX_HEADS * FOX_DIM
    d = w_qkv.shape[0]
    wqk = jnp.concatenate([w_qkv[:, :n] * (FOX_DIM ** -0.5 * LOG2E), w_qkv[:, n:2 * n]], axis=1)
    wvt = w_qkv[:, 2 * n:].T
    wf = jnp.concatenate([w_f, jnp.zeros((d, LANES - FOX_HEADS), F32)], axis=1)
    bf = jnp.concatenate([b_f.astype(F32), jnp.zeros((LANES - FOX_HEADS,), F32)]).reshape(1, LANES)
    qk, vt, kd = _fox_proj(h, attn_norm, wqk.astype(BF16), wvt.astype(BF16), wf.astype(BF16), bf,
                           batch=batch, seq=seq)
    n_pairs = FOX_HEADS // 2
    return _flash(qk, vt, kd, batch=batch, seq=seq, n_pairs=n_pairs, q_blk=0, k_blk=n_pairs, sep=False)


def _dil_attn_layer(h, attn_norm, w_qkv, *, batch, seq, tm=256):
    wqk, wv = _dil_weights(w_qkv)
    tables = [_dil_tables(seq, d, tm) for _, d in DIL_PATTERNS]
    groups = _dil_proj(h, attn_norm, wqk, wv, tables, batch=batch, seq=seq, tm=tm)
    return _dil_attn(groups, batch=batch, seq=seq)


def kernel(x, l0_attn_norm, l0_mla_wq_a, l0_mla_q_norm, l0_mla_wq_b, l0_mla_wkv_a, l0_mla_kv_norm, l0_mla_wkv_b, l0_mla_wo, l0_mlp_norm, l0_w_up, l0_w_down, l1_attn_norm, l1_fox_w_qkv, l1_fox_w_f, l1_fox_b_f, l1_fox_wo, l1_mlp_norm, l1_w_up, l1_w_down, l2_attn_norm, l2_dil_w_qkv, l2_dil_wo, l2_mlp_norm, l2_w_up, l2_w_down, l3_attn_norm, l3_mla_wq_a, l3_mla_q_norm, l3_mla_wq_b, l3_mla_wkv_a, l3_mla_kv_norm, l3_mla_wkv_b, l3_mla_wo, l3_mlp_norm, l3_w_up, l3_w_down, final_norm):
    batch, seq, d = x.shape
    assert seq % DIL_TOKENS == 0 and (batch * seq) % 1024 == 0
    for window, dil in DIL_PATTERNS:
        assert window // dil == DIL_WINDOW_KEYS
    mla_t = _mla_tables(seq)
    kw = dict(batch=batch, seq=seq)
    bf = lambda w: w.astype(BF16)
    h = x.reshape(batch * seq, d)

    o = _mla_attn(h, l0_attn_norm, l0_mla_wq_a, l0_mla_q_norm, l0_mla_wq_b, l0_mla_wkv_a,
                  l0_mla_kv_norm, l0_mla_wkv_b, mla_t, **kw)
    h = _mlp(h, o, bf(l0_mla_wo), l0_mlp_norm, bf(l0_w_up), bf(l0_w_down), final_norm, final_norm=False)

    o = _fox_attn(h, l1_attn_norm, l1_fox_w_qkv, l1_fox_w_f, l1_fox_b_f, **kw)
    h = _mlp(h, o, bf(l1_fox_wo), l1_mlp_norm, bf(l1_w_up), bf(l1_w_down), final_norm, final_norm=False)

    o = _dil_attn_layer(h, l2_attn_norm, l2_dil_w_qkv, **kw)
    h = _mlp(h, o, bf(l2_dil_wo), l2_mlp_norm, bf(l2_w_up), bf(l2_w_down), final_norm, final_norm=False)

    o = _mla_attn(h, l3_attn_norm, l3_mla_wq_a, l3_mla_q_norm, l3_mla_wq_b, l3_mla_wkv_a,
                  l3_mla_kv_norm, l3_mla_wkv_b, mla_t, **kw)
    h = _mlp(h, o, bf(l3_mla_wo), l3_mlp_norm, bf(l3_w_up), bf(l3_w_down), final_norm, final_norm=True)
    return h.reshape(batch, seq, d)
```
